```python
import math
import jax, jax.numpy as jnp
from jax import lax
import numpy as np

D_MODEL = 1024
BATCH = 2
SEQ = 8192
DEPTH = 2
DEC_BATCH = 16
DEC_SEQ = 64
PAST_LEN = 4096

CHUNK = 64
Q_BLOCK = 128
N_EVEN = (DEPTH + 1) // 2
N_ODD = DEPTH // 2
EPS = 1e-6

SB_HEADS = 8
SB_DIM = 64
SB_WIDTH = SB_HEADS * SB_DIM
HG_HEADS = 4
HG_DK = 128
HG_DV = 128
HG_KW = HG_HEADS * HG_DK
HG_VW = HG_HEADS * HG_DV
EVEN_SPLITS = (SB_WIDTH, 2 * SB_WIDTH, 3 * SB_WIDTH, 3 * SB_WIDTH + HG_KW,
               3 * SB_WIDTH + 2 * HG_KW, 3 * SB_WIDTH + 2 * HG_KW + HG_VW)
EVEN_IN = 3 * SB_WIDTH + 2 * HG_KW + 2 * HG_VW
EVEN_OUT = SB_WIDTH + HG_VW
LRU_WIDTH = D_MODEL
LRU_BLOCKS = 8
LRU_BLK = LRU_WIDTH // LRU_BLOCKS
CONV_W = 4
C_SCALE = 8.0
D_FF = -(-8 * D_MODEL // (3 * 256)) * 256
N_MOD = 6

kernel_name = "hybrid_stickbreak_hgrn2_rglru_stream_step"

f32 = jnp.float32


def rms_norm(x, g):
    xf = x.astype(f32)
    return xf * lax.rsqrt(jnp.mean(xf * xf, -1, keepdims=True) + EPS) * g.astype(f32)


def sb_block(q, k, v, q_pos, k_pos):
    z = jnp.einsum('bqhd,bkhd->bhqk', q.astype(f32), k.astype(f32)) / math.sqrt(SB_DIM)
    mask = (k_pos[None, :] < q_pos[:, None])[None, None]
    log_1mb = jnp.where(mask, jax.nn.log_sigmoid(-z), 0.0)
    after = lax.cumsum(log_1mb, axis=3, reverse=True) - log_1mb
    w = jnp.where(mask, jnp.exp(jax.nn.log_sigmoid(z) + after), 0.0)
    return jnp.einsum('bhqk,bkhd->bqhd', w, v.astype(f32))


def sb_prompt(q, k, v):
    B, T, H, d = q.shape
    nb = T // Q_BLOCK
    qb = jnp.moveaxis(q.reshape(B, nb, Q_BLOCK, H, d), 1, 0)
    k_pos = jnp.arange(T)

    def one_block(args):
        qi, i = args
        return sb_block(qi, k, v, i * Q_BLOCK + jnp.arange(Q_BLOCK), k_pos)

    o = lax.map(one_block, (qb, jnp.arange(nb)))
    return jnp.moveaxis(o, 0, 1).reshape(B, T, H, d)


def hgrn_chunk(S, q, k, v, logf):
    L = q.shape[1]
    b = jnp.cumsum(logf, axis=1)
    causal = jnp.tril(jnp.ones((L, L), bool))[None, :, :, None, None]
    decay = jnp.exp(jnp.where(causal, b[:, :, None] - b[:, None, :], -jnp.inf))
    attn = jnp.einsum('bthc,bshc,btshc->bhts', q, k, decay)
    o = jnp.einsum('bhts,bshv->bthv', attn, v) + jnp.einsum('bthc,bhcv->bthv', q * jnp.exp(b), S)
    bL = b[:, -1]
    S_new = jnp.exp(bL)[..., None] * S + jnp.einsum('bshc,bshv->bhcv', k * jnp.exp(bL[:, None] - b), v)
    return S_new, o


def hgrn_prompt(q, k, v, logf):
    B, T, H, dk = q.shape
    n = T // CHUNK
    to_chunks = lambda t: jnp.moveaxis(t.reshape(B, n, CHUNK, H, t.shape[-1]), 1, 0)
    S0 = jnp.zeros((B, H, HG_DK, HG_DV), f32)
    S, o = lax.scan(lambda S, xs: hgrn_chunk(S, *xs), S0,
                    (to_chunks(q), to_chunks(k), to_chunks(v), to_chunks(logf)))
    return S, jnp.moveaxis(o, 0, 1).reshape(B, T, H, HG_DV)


def even_mix(hn, w_in, w_out, g_norm, lb, past_k, past_v, S0):
    B, T, _ = hn.shape
    p = (hn @ w_in).astype(f32)
    q_a, k_a, v_a, q_b, f_b, i_b, g_b = jnp.split(p, EVEN_SPLITS, axis=-1)
    q_a = q_a.reshape(B, T, SB_HEADS, SB_DIM)
    k_a = k_a.reshape(B, T, SB_HEADS, SB_DIM)
    v_a = v_a.reshape(B, T, SB_HEADS, SB_DIM)
    if past_k is None:
        o_a = sb_prompt(q_a, k_a, v_a)
    else:
        P = past_k.shape[1]
        keys = jnp.concatenate([past_k.astype(f32), k_a], axis=1)
        vals = jnp.concatenate([past_v.astype(f32), v_a], axis=1)
        o_a = sb_block(q_a, keys, vals, P + jnp.arange(T), jnp.arange(P + T))
    lbh = lb.reshape(HG_HEADS, HG_DK)
    f = lbh + (1.0 - lbh) * jax.nn.sigmoid(f_b.reshape(B, T, HG_HEADS, HG_DK))
    logf = jnp.log(f)
    k_b = 1.0 - f
    q_b = jax.nn.silu(q_b).reshape(B, T, HG_HEADS, HG_DK) * HG_DK ** -0.5
    i_b = i_b.reshape(B, T, HG_HEADS, HG_DV)
    if S0 is None:
        S, o_b = hgrn_prompt(q_b, k_b, i_b, logf)
    else:
        S, o_b = hgrn_chunk(S0.astype(f32), q_b, k_b, i_b, logf)
    o_b = o_b * lax.rsqrt(jnp.mean(o_b * o_b, -1, keepdims=True) + EPS) * g_norm.astype(f32).reshape(HG_HEADS, HG_DV)
    o_b = o_b.reshape(B, T, HG_VW) * jax.nn.silu(g_b)
    o = jnp.concatenate([o_a.reshape(B, T, SB_WIDTH), o_b], axis=-1).astype(hn.dtype) @ w_out
    return o, k_a.astype(hn.dtype), v_a.astype(hn.dtype), S


def _lin_comb(l, r):
    a_l, b_l = l
    a_r, b_r = r
    return a_l * a_r, a_r * b_l + b_r


def odd_mix(hn, w_in, conv_w, conv_b, wa, ba, wx, bx, lam, w_out, conv_past, h0):
    B, T, _ = hn.shape
    p = (hn @ w_in).astype(f32)
    gate_br, x_br = jnp.split(p, 2, axis=-1)
    pad = jnp.zeros((B, CONV_W - 1, LRU_WIDTH), f32) if conv_past is None else conv_past.astype(f32)
    xp = jnp.concatenate([pad, x_br], axis=1)
    cw = conv_w.astype(f32)
    xc = conv_b.astype(f32) + xp[:, 0:T] * cw[0]
    for j in range(1, CONV_W):
        xc = xc + xp[:, j:j + T] * cw[j]
    xb = xc.reshape(B, T, LRU_BLOCKS, LRU_BLK)
    r = jax.nn.sigmoid(jnp.einsum('bthi,hij->bthj', xb, wa.astype(f32)).reshape(B, T, LRU_WIDTH) + ba)
    gi = jax.nn.sigmoid(jnp.einsum('bthi,hij->bthj', xb, wx.astype(f32)).reshape(B, T, LRU_WIDTH) + bx)
    log_a = C_SCALE * r * jax.nn.log_sigmoid(lam.astype(f32))
    a = jnp.exp(log_a)
    mult = jnp.sqrt(-jnp.expm1(2.0 * log_a))
    if h0 is None:
        mult = jnp.where(jnp.arange(T)[None, :, None] == 0, 1.0, mult)
    u = mult * gi * xc
    if h0 is not None:
        u = u.at[:, 0].add(a[:, 0] * h0.astype(f32))
    _, h = lax.associative_scan(_lin_comb, (a, u), axis=1)
    y = (jax.nn.gelu(gate_br) * h).astype(hn.dtype) @ w_out
    return y, xp[:, -(CONV_W - 1):].astype(hn.dtype), h[:, -1]


def swiglu(h, wg, wu, wd):
    return (jax.nn.silu(h @ wg) * (h @ wu)) @ wd


def trunk(x, c, past_k, past_v, past_S, past_conv, past_h,
          norm_mix, norm_ffn, w_ada, b_ada, w_in_even, w_out_even, hg_gnorm, hg_lb_logits,
          w_in_odd, conv_w, conv_b, lru_wa, lru_ba, lru_wx, lru_bx, lru_lambda, w_out_odd,
          ffn_wg, ffn_wu, ffn_wd, final_norm):
    sample = past_k is not None
    B = x.shape[0]
    lb_all = jnp.cumsum(jax.nn.softmax(hg_lb_logits.astype(f32), axis=0), axis=0)
    sc = jax.nn.silu(c.astype(f32))
    new_k, new_v, new_S, new_conv, new_h = [], [], [], [], []
    for l in range(DEPTH):
        e = l // 2
        m = (sc @ w_ada[l].astype(f32) + b_ada[l]).reshape(B, N_MOD, D_MODEL)[:, :, None, :]
        sh1, sc1, g1, sh2, sc2, g2 = (m[:, i] for i in range(N_MOD))
        hn = (rms_norm(x, norm_mix[l]) * (1.0 + sc1) + sh1).astype(x.dtype)
        if l % 2 == 0:
            o, kn, vn, S = even_mix(hn, w_in_even[e], w_out_even[e], hg_gnorm[e], lb_all[l],
                                    past_k[e] if sample else None,
                                    past_v[e] if sample else None,
                                    past_S[e] if sample else None)
            new_k.append(kn)
            new_v.append(vn)
            new_S.append(S)
        else:
            o, cs, hl = odd_mix(hn, w_in_odd[e], conv_w[e], conv_b[e], lru_wa[e], lru_ba[e],
                                lru_wx[e], lru_bx[e], lru_lambda[e], w_out_odd[e],
                                past_conv[e] if sample else None,
                                past_h[e] if sample else None)
            new_conv.append(cs)
            new_h.append(hl)
        x = x + ((1.0 + g1) * o).astype(x.dtype)
        hn = (rms_norm(x, norm_ffn[l]) * (1.0 + sc2) + sh2).astype(x.dtype)
        x = x + ((1.0 + g2) * swiglu(hn, ffn_wg[l], ffn_wu[l], ffn_wd[l])).astype(x.dtype)
    y = rms_norm(x, final_norm).astype(x.dtype)
    return y, jnp.stack(new_k), jnp.stack(new_v), jnp.stack(new_S), jnp.stack(new_conv), jnp.stack(new_h)


def setup_inputs(seed: int = 0) -> dict:
    key = jax.random.key(seed)
    ks = iter(jax.random.split(key, 32))
    nrm = lambda shape, scale: jax.random.normal(next(ks), shape, f32) * scale
    a0 = jax.random.uniform(next(ks), (N_ODD, LRU_WIDTH), f32, minval=0.9, maxval=0.999)
    p = a0 ** (1.0 / C_SCALE)
    lru_lambda = jnp.log(p) - jnp.log1p(-p)
    return {
        "x_prompt": nrm((BATCH, SEQ, D_MODEL), 1.0),
        "x_sample": nrm((DEC_BATCH, DEC_SEQ, D_MODEL), 1.0),
        "cache_sb_k": nrm((N_EVEN, DEC_BATCH, PAST_LEN, SB_HEADS, SB_DIM), 1.0),
        "cache_sb_v": nrm((N_EVEN, DEC_BATCH, PAST_LEN, SB_HEADS, SB_DIM), 1.0),
        "state_hgrn": nrm((N_EVEN, DEC_BATCH, HG_HEADS, HG_DK, HG_DV), 0.3),
        "state_conv": nrm((N_ODD, DEC_BATCH, CONV_W - 1, LRU_WIDTH), 1.0),
        "state_lru": nrm((N_ODD, DEC_BATCH, LRU_WIDTH), 0.5),
        "c_prompt": nrm((BATCH, D_MODEL), 1.0),
        "c_sample": nrm((DEC_BATCH, D_MODEL), 1.0),
        "norm_mix": 1.0 + nrm((DEPTH, D_MODEL), 0.02),
        "norm_ffn": 1.0 + nrm((DEPTH, D_MODEL), 0.02),
        "w_ada": nrm((DEPTH, D_MODEL, N_MOD * D_MODEL), 0.2 * D_MODEL ** -0.5),
        "b_ada": nrm((DEPTH, N_MOD * D_MODEL), 0.02),
        "w_in_even": nrm((N_EVEN, D_MODEL, EVEN_IN), D_MODEL ** -0.5),
        "w_out_even": nrm((N_EVEN, EVEN_OUT, D_MODEL), EVEN_OUT ** -0.5),
        "hg_gnorm": 1.0 + nrm((N_EVEN, HG_VW), 0.02),
        "hg_lb_logits": nrm((DEPTH + 1, HG_KW), 0.5),
        "w_in_odd": nrm((N_ODD, D_MODEL, 2 * LRU_WIDTH), D_MODEL ** -0.5),
        "conv_w": nrm((N_ODD, CONV_W, LRU_WIDTH), CONV_W ** -0.5),
        "conv_b": nrm((N_ODD, LRU_WIDTH), 0.02),
        "lru_wa": nrm((N_ODD, LRU_BLOCKS, LRU_BLK, LRU_BLK), LRU_BLK ** -0.5),
        "lru_ba": nrm((N_ODD, LRU_WIDTH), 0.02),
        "lru_wx": nrm((N_ODD, LRU_BLOCKS, LRU_BLK, LRU_BLK), LRU_BLK ** -0.5),
        "lru_bx": nrm((N_ODD, LRU_WIDTH), 0.02),
        "lru_lambda": lru_lambda,
        "w_out_odd": nrm((N_ODD, LRU_WIDTH, D_MODEL), LRU_WIDTH ** -0.5),
        "ffn_wg": nrm((DEPTH, D_MODEL, D_FF), D_MODEL ** -0.5),
        "ffn_wu": nrm((DEPTH, D_MODEL, D_FF), D_MODEL ** -0.5),
        "ffn_wd": nrm((DEPTH, D_FF, D_MODEL), D_FF ** -0.5),
        "final_norm": 1.0 + nrm((D_MODEL,), 0.02),
    }


def reference(x_prompt, x_sample, cache_sb_k, cache_sb_v, state_hgrn, state_conv, state_lru,
              c_prompt, c_sample, norm_mix, norm_ffn, w_ada, b_ada, w_in_even, w_out_even,
              hg_gnorm, hg_lb_logits, w_in_odd, conv_w, conv_b, lru_wa, lru_ba, lru_wx, lru_bx,
              lru_lambda, w_out_odd, ffn_wg, ffn_wu, ffn_wd, final_norm):
    weights = (norm_mix, norm_ffn, w_ada, b_ada, w_in_even, w_out_even, hg_gnorm, hg_lb_logits,
               w_in_odd, conv_w, conv_b, lru_wa, lru_ba, lru_wx, lru_bx, lru_lambda, w_out_odd,
               ffn_wg, ffn_wu, ffn_wd, final_norm)
    y_prompt, k_p, v_p, S_p, conv_p, h_p = trunk(x_prompt, c_prompt, None, None, None, None, None, *weights)
    y_sample, k_s, v_s, S_s, conv_s, h_s = trunk(x_sample, c_sample, cache_sb_k, cache_sb_v, state_hgrn,
                                                 state_conv, state_lru, *weights)
    return (y_prompt, y_sample, k_p, v_p, S_p, conv_p, h_p, k_s, v_s, S_s, conv_s, h_s)
```

```python
import functools
import math

import jax
import jax.numpy as jnp
from jax import lax
from jax.experimental import pallas as pl
from jax.experimental.pallas import tpu as pltpu

f32 = jnp.float32
bf16 = jnp.bfloat16

D_MODEL = 1024
DEPTH = 2
EPS = 1e-6
N_MOD = 6

SB_HEADS = 8
SB_DIM = 64
SB_WIDTH = SB_HEADS * SB_DIM
HG_HEADS = 4
HG_DK = 128
HG_DV = 128
HG_W = HG_HEADS * HG_DK
SEG = 512
N_SEG = 7
HG_CHUNK = 64
HG_SUB = 16
LRU_BLOCKS = 8
LRU_BLK = D_MODEL // LRU_BLOCKS
CONV_W = 4
C_SCALE = 8.0
D_FF = 2816
FF_TILE = 256

LANES = 128
HALO = 8
VMEM_LIMIT = 56 * 1024 * 1024

NT_DIMS = (((1,), (1,)), ((), ()))
TN_DIMS = (((0,), (0,)), ((), ()))


def _dot(a, b):
    return jnp.dot(a, b, preferred_element_type=f32)


def _dot_nt(a, b):
    return lax.dot_general(a, b, NT_DIMS, preferred_element_type=f32)


def _split3(x):
    hi = x.astype(bf16)
    r1 = x - hi.astype(f32)
    mid = r1.astype(bf16)
    lo = (r1 - mid.astype(f32)).astype(bf16)
    return hi, mid, lo


def _dot_exact_rhs(x, m):
    hi, mid, lo = _split3(x)
    return _dot(hi, m) + _dot(mid, m) + _dot(lo, m)


def _sigmoid(x):
    return jax.nn.sigmoid(x)


def _silu(x):
    return x * _sigmoid(x)


def _norm_mod(x, g, scale, shift):
    ms = jnp.mean(x * x, axis=-1, keepdims=True)
    return x * lax.rsqrt(ms + EPS) * g * (1.0 + scale) + shift


def _params(*sem):
    return pltpu.CompilerParams(dimension_semantics=sem, vmem_limit_bytes=VMEM_LIMIT)


def _ada_kernel(c_ref, w_ref, b_ref, o_ref):
    c = c_ref[...]
    sc = _silu(c)
    w = w_ref[...]
    s_hi = sc.astype(bf16)
    s_lo = (sc - s_hi.astype(f32)).astype(bf16)
    w_hi = w.astype(bf16)
    w_lo = (w - w_hi.astype(f32)).astype(bf16)
    o_ref[...] = _dot(s_hi, w_hi) + _dot(s_lo, w_hi) + _dot(s_hi, w_lo) + b_ref[...]


def _ada(c_all, w_ada, b_ada):
    rows = c_all.shape[0]
    n_out = N_MOD * D_MODEL
    tn = 1536
    return pl.pallas_call(
        _ada_kernel,
        grid=(DEPTH, n_out // tn),
        in_specs=[
            pl.BlockSpec((rows, D_MODEL), lambda l, j: (0, 0)),
            pl.BlockSpec((None, D_MODEL, tn), lambda l, j: (l, 0, j)),
            pl.BlockSpec((None, 1, tn), lambda l, j: (l, 0, j)),
        ],
        out_specs=pl.BlockSpec((None, rows, tn), lambda l, j: (l, 0, j)),
        out_shape=jax.ShapeDtypeStruct((DEPTH, rows, n_out), f32),
        compiler_params=_params("arbitrary", "arbitrary"),
        name="ada_mod",
    )(c_all, w_ada, b_ada.reshape(DEPTH, 1, n_out))


def _even_pre_kernel(layer, x_ref, mod_ref, g_ref, w_ref, lbl_ref,
                     qa_ref, ka_ref, va_ref, kab_ref, vab_ref,
                     qb_ref, lf_ref, kb_ref, ib_ref, gs_ref):
    nb, tt, d = x_ref.shape
    m = nb * tt
    hn = _norm_mod(x_ref[...], g_ref[...], mod_ref[:, 1:2, :], mod_ref[:, 0:1, :])
    hn = hn.reshape(m, d).astype(bf16)

    def seg(i):
        return _dot(hn, w_ref[:, i * SEG:(i + 1) * SEG]).reshape(nb, tt, SEG)

    qa_ref[...] = (seg(0) * (1.0 / math.sqrt(SB_DIM))).astype(bf16)
    ka = seg(1)
    ka_ref[...] = ka
    kab_ref[...] = ka.astype(bf16)
    va = seg(2)
    va_ref[...] = va
    vab_ref[...] = va.astype(bf16)
    qb_ref[...] = _silu(seg(3)) * (HG_DK ** -0.5)

    lg = lbl_ref[...]
    e = jnp.exp(lg - jnp.max(lg, axis=0, keepdims=True))
    lb = jnp.sum(e[:layer + 1], axis=0, keepdims=True) / jnp.sum(e, axis=0, keepdims=True)
    fb = seg(4)
    f = lb + (1.0 - lb) * _sigmoid(fb)
    lf_ref[...] = jnp.log(f)
    kb_ref[...] = (1.0 - lb) * _sigmoid(-fb)
    ib_ref[...] = seg(5)
    gs_ref[...] = _silu(seg(6))


def _even_pre(x, mod, g, w_bf, lb_logits, layer, nb, tt):
    b, t, d = x.shape
    grid = (b // nb, t // tt)
    blk = lambda width: pl.BlockSpec((nb, tt, width), lambda i, j: (i, j, 0))
    out_f = jax.ShapeDtypeStruct((b, t, SEG), f32)
    out_b = jax.ShapeDtypeStruct((b, t, SEG), bf16)
    return pl.pallas_call(
        functools.partial(_even_pre_kernel, layer),
        grid=grid,
        in_specs=[
            blk(d),
            pl.BlockSpec((nb, N_MOD, d), lambda i, j: (i, 0, 0)),
            pl.BlockSpec((1, d), lambda i, j: (0, 0)),
            pl.BlockSpec((d, N_SEG * SEG), lambda i, j: (0, 0)),
            pl.BlockSpec((DEPTH + 1, HG_W), lambda i, j: (0, 0)),
        ],
        out_specs=[blk(SEG)] * 10,
        out_shape=[out_b, out_f, out_f, out_b, out_b, out_f, out_f, out_f, out_f, out_f],
        compiler_params=_params("arbitrary", "arbitrary"),
        name="even_pre",
    )(x, mod, g, w_bf, lb_logits)


def _sb_consts(tq, tk):
    r = lax.broadcasted_iota(jnp.int32, (tk, LANES + tk), 0)
    c = lax.broadcasted_iota(jnp.int32, (tk, LANES + tk), 1)
    return jnp.where((c < LANES) | (r > c - LANES), 1.0, 0.0).astype(bf16)


def _sb_tile(q_pair, k_blk, v_blk, carry, acc, ones_upper, mask):
    tk = k_blk.shape[0]
    lane = lax.broadcasted_iota(jnp.int32, acc.shape, 1)
    outs = []
    new_carry = []
    for h in range(2):
        z = _dot_nt(q_pair[h], k_blk)
        l1mb = -(jnp.maximum(z, 0.0) + jnp.log(1.0 + jnp.exp(-jnp.abs(z))))
        if mask is not None:
            l1mb = jnp.where(mask, l1mb, 0.0)
        sums = _dot_exact_rhs(l1mb, ones_upper)
        after = carry[h][:, :tk] + sums[:, LANES:]
        w = jnp.exp(z + l1mb + after)
        if mask is not None:
            w = jnp.where(mask, w, 0.0)
        outs.append(_dot(w.astype(bf16), v_blk))
        new_carry.append(carry[h] + sums[:, :LANES])
    acc = acc + jnp.where(lane < SB_DIM, outs[0], outs[1])
    return tuple(new_carry), acc


def _split_heads(q):
    lane = lax.broadcasted_iota(jnp.int32, q.shape, 1)
    zero = jnp.zeros_like(q)
    return jnp.where(lane < SB_DIM, q, zero), jnp.where(lane >= SB_DIM, q, zero)


def _causal_mask(tq, tk):
    r = lax.broadcasted_iota(jnp.int32, (tq, tk), 0)
    c = lax.broadcasted_iota(jnp.int32, (tq, tk), 1)
    return c < r


def _sb_prompt_kernel(q_ref, k_ref, v_ref, o_ref):
    tq = q_ref.shape[0]
    i = pl.program_id(2)
    q_pair = _split_heads(q_ref[...])
    ones_upper = _sb_consts(tq, tq)
    zero = jnp.zeros((tq, LANES), f32)
    start = pl.multiple_of(i * tq, tq)
    carry, acc = _sb_tile(q_pair, k_ref[pl.ds(start, tq), :], v_ref[pl.ds(start, tq), :],
                          (zero, zero), zero, ones_upper, _causal_mask(tq, tq))

    def body(n, state):
        c0, c1, acc = state
        rows = pl.ds(pl.multiple_of((i - 1 - n) * tq, tq), tq)
        (c0, c1), acc = _sb_tile(q_pair, k_ref[rows, :], v_ref[rows, :], (c0, c1), acc,
                                 ones_upper, None)
        return c0, c1, acc

    _, _, acc = lax.fori_loop(0, i, body, (carry[0], carry[1], acc))
    o_ref[...] = acc.astype(o_ref.dtype)


def _sb_prompt(q, k, v, tq):
    b, t, _ = q.shape
    n_pair = SB_WIDTH // LANES
    qspec = pl.BlockSpec((None, tq, LANES), lambda bi, p, i: (bi, i, p))
    kvspec = pl.BlockSpec((None, t, LANES), lambda bi, p, i: (bi, 0, p))
    return pl.pallas_call(
        _sb_prompt_kernel,
        grid=(b, n_pair, t // tq),
        in_specs=[qspec, kvspec, kvspec],
        out_specs=qspec,
        out_shape=jax.ShapeDtypeStruct((b, t, SB_WIDTH), bf16),
        compiler_params=_params("arbitrary", "arbitrary", "arbitrary"),
        name="sb_prompt",
    )(q, k, v)


def _sb_sample_kernel(tk, q_ref, k_ref, v_ref, pk_ref, pv_ref, o_ref):
    tq = q_ref.shape[0]
    n_past = pk_ref.shape[0] // tk
    q_pair = _split_heads(q_ref[...])
    zero = jnp.zeros((tq, LANES), f32)
    carry, acc = _sb_tile(q_pair, k_ref[...], v_ref[...], (zero, zero), zero,
                          _sb_consts(tq, tq), _causal_mask(tq, tq))
    ones_upper = _sb_consts(tq, tk)

    def body(n, state):
        c0, c1, acc = state
        rows = pl.ds(pl.multiple_of((n_past - 1 - n) * tk, tk), tk)
        (c0, c1), acc = _sb_tile(q_pair, pk_ref[rows, :].astype(bf16), pv_ref[rows, :].astype(bf16),
                                 (c0, c1), acc, ones_upper, None)
        return c0, c1, acc

    _, _, acc = lax.fori_loop(0, n_past, body, (carry[0], carry[1], acc))
    o_ref[...] = acc.astype(o_ref.dtype)


def _sb_sample(q, k, v, past_k, past_v, tk):
    b, t, _ = q.shape
    p = past_k.shape[1]
    n_pair = SB_WIDTH // LANES
    qspec = pl.BlockSpec((None, t, LANES), lambda bi, pi: (bi, 0, pi))
    pspec = pl.BlockSpec((None, p, LANES), lambda bi, pi: (bi, 0, pi))
    return pl.pallas_call(
        functools.partial(_sb_sample_kernel, tk),
        grid=(b, n_pair),
        in_specs=[qspec, qspec, qspec, pspec, pspec],
        out_specs=qspec,
        out_shape=jax.ShapeDtypeStruct((b, t, SB_WIDTH), bf16),
        compiler_params=_params("arbitrary", "arbitrary"),
        name="sb_sample",
    )(q, k, v, past_k, past_v)


def _hgrn_head_chunk(q, k, v, b, st):
    n_sub = HG_CHUNK // HG_SUB
    v_bf = v.astype(bf16)
    o_inter = _dot_nt((q * jnp.exp(b)).astype(bf16), st.astype(bf16))
    t_idx = lax.broadcasted_iota(jnp.int32, (HG_SUB, HG_SUB, HG_DK), 0)
    s_idx = lax.broadcasted_iota(jnp.int32, (HG_SUB, HG_SUB, HG_DK), 1)
    causal = s_idx <= t_idx
    outs = []
    for i in range(n_sub):
        lo = i * HG_SUB
        rows = slice(lo, lo + HG_SUB)
        qi, ki, bi = q[rows], k[rows], b[rows]
        diff = bi[:, None, :] - bi[None, :, :]
        dec = jnp.exp(jnp.where(causal, diff, -1e30))
        a_diag = jnp.sum(qi[:, None, :] * ki[None, :, :] * dec, axis=-1)
        o_i = _dot(a_diag.astype(bf16), v_bf[rows])
        if i > 0:
            ref = b[lo - 1:lo]
            q_t = qi * jnp.exp(bi - ref)
            k_t = k[:lo] * jnp.exp(ref - b[:lo])
            a_off = _dot_nt(q_t.astype(bf16), k_t.astype(bf16))
            o_i = o_i + _dot(a_off.astype(bf16), v_bf[:lo])
        outs.append(o_inter[rows] + o_i)
    o = jnp.concatenate(outs, axis=0)
    b_last = b[HG_CHUNK - 1:HG_CHUNK]
    k_dec = (k * jnp.exp(b_last - b)).astype(bf16)
    st_new = jnp.exp(b_last) * st + lax.dot_general(v_bf, k_dec, TN_DIMS, preferred_element_type=f32)
    return o, st_new


def _hgrn_kernel(q_ref, lf_ref, k_ref, v_ref, gs_ref, gn_ref, s0_ref, o_ref, sout_ref, st_scr):
    tb = pl.program_id(1)
    n_tb = pl.num_programs(1)
    n_chunk = q_ref.shape[0] // HG_CHUNK

    @pl.when(tb == 0)
    def _():
        for h in range(HG_HEADS):
            st_scr[h] = s0_ref[h].T

    r = lax.broadcasted_iota(jnp.int32, (HG_CHUNK, HG_CHUNK), 0)
    c = lax.broadcasted_iota(jnp.int32, (HG_CHUNK, HG_CHUNK), 1)
    lower = jnp.where(c <= r, 1.0, 0.0).astype(bf16)

    def chunk(ci, carry):
        rows = pl.ds(pl.multiple_of(ci * HG_CHUNK, HG_CHUNK), HG_CHUNK)
        b_all = _dot_exact_rhs_lhs(lower, lf_ref[rows, :])
        for h in range(HG_HEADS):
            cols = slice(h * HG_DK, (h + 1) * HG_DK)
            o, st_new = _hgrn_head_chunk(q_ref[rows, cols], k_ref[rows, cols], v_ref[rows, cols],
                                         b_all[:, cols], st_scr[h])
            st_scr[h] = st_new
            o = o * lax.rsqrt(jnp.mean(o * o, axis=-1, keepdims=True) + EPS)
            o_ref[rows, cols] = (o * gn_ref[:, cols] * gs_ref[rows, cols]).astype(o_ref.dtype)
        return carry

    lax.fori_loop(0, n_chunk, chunk, 0)

    @pl.when(tb == n_tb - 1)
    def _():
        for h in range(HG_HEADS):
            sout_ref[h] = st_scr[h].T


def _dot_exact_rhs_lhs(m, x):
    hi, mid, lo = _split3(x)
    return _dot(m, hi) + _dot(m, mid) + _dot(m, lo)


def _hgrn(qb, lf, kb, ib, gs, gnorm, s0, tt):
    b, t, _ = qb.shape
    blk = pl.BlockSpec((None, tt, HG_W), lambda i, j: (i, j, 0))
    sspec = pl.BlockSpec((None, HG_HEADS, HG_DK, HG_DV), lambda i, j: (i, 0, 0, 0))
    return pl.pallas_call(
        _hgrn_kernel,
        grid=(b, t // tt),
        in_specs=[blk, blk, blk, blk, blk, pl.BlockSpec((1, HG_W), lambda i, j: (0, 0)), sspec],
        out_specs=[blk, sspec],
        out_shape=[jax.ShapeDtypeStruct((b, t, HG_W), bf16),
                   jax.ShapeDtypeStruct((b, HG_HEADS, HG_DK, HG_DV), f32)],
        scratch_shapes=[pltpu.VMEM((HG_HEADS, HG_DV, HG_DK), f32)],
        compiler_params=_params("arbitrary", "arbitrary"),
        name="hgrn",
    )(qb, lf, kb, ib, gs, gnorm, s0)


def _ffn_step(j, hn_scr, wg_ref, wu_ref, wd_ref, acc_scr):
    h = hn_scr[...]
    g = _dot(h, wg_ref[...])
    u = _dot(h, wu_ref[...])
    part = _dot((_silu(g) * u).astype(bf16), wd_ref[...])

    @pl.when(j == 0)
    def _():
        acc_scr[...] = part

    @pl.when(j > 0)
    def _():
        acc_scr[...] += part


def _ffn_specs():
    return [
        pl.BlockSpec((D_MODEL, FF_TILE), lambda i, t, j: (0, j)),
        pl.BlockSpec((D_MODEL, FF_TILE), lambda i, t, j: (0, j)),
        pl.BlockSpec((FF_TILE, D_MODEL), lambda i, t, j: (j, 0)),
    ]


def _even_post_kernel(x_ref, mod_ref, oa_ref, ob_ref, wo_ref, gf_ref, wg_ref, wu_ref, wd_ref,
                      out_ref, x1_scr, hn_scr, acc_scr):
    nb, tt, d = x_ref.shape
    m = nb * tt
    j = pl.program_id(2)
    n_j = pl.num_programs(2)

    @pl.when(j == 0)
    def _():
        o = (_dot(oa_ref[...].reshape(m, SB_WIDTH), wo_ref[:SB_WIDTH, :])
             + _dot(ob_ref[...].reshape(m, HG_W), wo_ref[SB_WIDTH:, :]))
        x1 = x_ref[...] + (1.0 + mod_ref[:, 2:3, :]) * o.reshape(nb, tt, d)
        x1_scr[...] = x1
        hn = _norm_mod(x1, gf_ref[...], mod_ref[:, 4:5, :], mod_ref[:, 3:4, :])
        hn_scr[...] = hn.reshape(m, d).astype(bf16)

    _ffn_step(j, hn_scr, wg_ref, wu_ref, wd_ref, acc_scr)

    @pl.when(j == n_j - 1)
    def _():
        out_ref[...] = x1_scr[...] + (1.0 + mod_ref[:, 5:6, :]) * acc_scr[...].reshape(nb, tt, d)


def _even_post(x, mod, oa, ob, wo, gf, wg, wu, wd, nb, tt):
    b, t, d = x.shape
    blk = lambda width: pl.BlockSpec((nb, tt, width), lambda i, tb, j: (i, tb, 0))
    const = lambda shape: pl.BlockSpec(shape, lambda i, tb, j: (0,) * len(shape))
    return pl.pallas_call(
        _even_post_kernel,
        grid=(b // nb, t // tt, D_FF // FF_TILE),
        in_specs=[blk(d), pl.BlockSpec((nb, N_MOD, d), lambda i, tb, j: (i, 0, 0)),
                  blk(SB_WIDTH), blk(HG_W), const((d, d)), const((1, d))] + _ffn_specs(),
        out_specs=blk(d),
        out_shape=jax.ShapeDtypeStruct((b, t, d), f32),
        scratch_shapes=[pltpu.VMEM((nb, tt, d), f32), pltpu.VMEM((nb * tt, d), bf16),
                        pltpu.VMEM((nb * tt, d), f32)],
        compiler_params=_params("arbitrary", "arbitrary", "arbitrary"),
        name="even_post_ffn",
    )(x, mod, oa, ob, wo, gf, wg, wu, wd)


def _linear_scan(a, u, tt):
    row = lax.broadcasted_iota(jnp.int32, a.shape, 0) & (tt - 1)
    d = 1
    while d < tt:
        keep = row >= d
        a_prev = jnp.where(keep, pltpu.roll(a, d, 0), 1.0)
        u_prev = jnp.where(keep, pltpu.roll(u, d, 0), 0.0)
        u = a * u_prev + u
        a = a * a_prev
        d *= 2
    return a, u


def _odd_kernel(stream_start, x_ref, mod_ref, gm_ref, win_ref, cw_ref, cb_ref, wax_ref, ba_ref, bx_ref,
                lam_ref, wo_ref, conv0_ref, h0_ref, gf_ref, fn_ref, wg_ref, wu_ref, wd_ref,
                y_ref, conv_ref, hout_ref,
                xs_scr, hprev_scr, x1_scr, hn_scr, acc_scr):
    nb, tt, d = x_ref.shape
    m = nb * tt
    tb = pl.program_id(1)
    j = pl.program_id(2)
    n_j = pl.num_programs(2)

    @pl.when(j == 0)
    def _():
        @pl.when(tb == 0)
        def _():
            xs_scr[:, HALO - (CONV_W - 1):HALO, :] = conv0_ref[...]
            hprev_scr[...] = h0_ref[...]

        x = x_ref[...]
        hn = _norm_mod(x, gm_ref[...], mod_ref[:, 1:2, :], mod_ref[:, 0:1, :])
        hn = hn.reshape(m, d).astype(bf16)
        gate = _dot(hn, win_ref[:, :d])
        xbr = _dot(hn, win_ref[:, d:]).reshape(nb, tt, d)
        xs_scr[:, HALO:HALO + tt, :] = xbr
        xc = cb_ref[...] + xbr * cw_ref[CONV_W - 1:CONV_W, :]
        for k in range(1, CONV_W):
            xc = xc + xs_scr[:, HALO - k:HALO - k + tt, :] * cw_ref[CONV_W - 1 - k:CONV_W - k, :]
        tail = xs_scr[:, HALO + tt - (CONV_W - 1):HALO + tt, :]
        conv_ref[...] = tail
        xs_scr[:, HALO - (CONV_W - 1):HALO, :] = tail

        xc = xc.reshape(m, d)
        xc_bf = xc.astype(bf16)
        ra, rx = [], []
        for blk in range(LRU_BLOCKS):
            both = _dot(xc_bf[:, blk * LRU_BLK:(blk + 1) * LRU_BLK], wax_ref[blk])
            ra.append(both[:, :LRU_BLK])
            rx.append(both[:, LRU_BLK:])
        r = _sigmoid(jnp.concatenate(ra, axis=1) + ba_ref[...])
        gi = _sigmoid(jnp.concatenate(rx, axis=1) + bx_ref[...])
        lam = lam_ref[...]
        log_sig_lam = -(jnp.maximum(-lam, 0.0) + jnp.log1p(jnp.exp(-jnp.abs(lam))))
        log_a = C_SCALE * r * log_sig_lam
        a = jnp.exp(log_a)
        mult = jnp.sqrt(-jnp.tanh(log_a) * (1.0 + a * a))
        if stream_start:
            row = (lax.broadcasted_iota(jnp.int32, (m, d), 0) & (tt - 1)) + tb * tt
            mult = jnp.where(row == 0, 1.0, mult)
        u = mult * gi * xc
        a_cum, h = _linear_scan(a, u, tt)
        h = h.reshape(nb, tt, d) + a_cum.reshape(nb, tt, d) * hprev_scr[...]
        h_last = h[:, tt - 1:tt, :]
        hprev_scr[...] = h_last
        hout_ref[...] = h_last

        y_in = (jax.nn.gelu(gate) * h.reshape(m, d)).astype(bf16)
        o = _dot(y_in, wo_ref[...])
        x1 = x + (1.0 + mod_ref[:, 2:3, :]) * o.reshape(nb, tt, d)
        x1_scr[...] = x1
        hn2 = _norm_mod(x1, gf_ref[...], mod_ref[:, 4:5, :], mod_ref[:, 3:4, :])
        hn_scr[...] = hn2.reshape(m, d).astype(bf16)

    _ffn_step(j, hn_scr, wg_ref, wu_ref, wd_ref, acc_scr)

    @pl.when(j == n_j - 1)
    def _():
        x2 = x1_scr[...] + (1.0 + mod_ref[:, 5:6, :]) * acc_scr[...].reshape(nb, tt, d)
        ms = jnp.mean(x2 * x2, axis=-1, keepdims=True)
        y_ref[...] = x2 * lax.rsqrt(ms + EPS) * fn_ref[...]


def _odd_layer(x, mod, gm, win, cw, cb, wax, ba, bx, lam, wo, conv0, h0, gf, fn, wg, wu, wd,
               nb, tt, stream_start):
    b, t, d = x.shape
    blk = pl.BlockSpec((nb, tt, d), lambda i, tb, j: (i, tb, 0))
    const = lambda shape: pl.BlockSpec(shape, lambda i, tb, j: (0,) * len(shape))
    per_seq = lambda rows: pl.BlockSpec((nb, rows, d), lambda i, tb, j: (i, 0, 0))
    return pl.pallas_call(
        functools.partial(_odd_kernel, stream_start),
        grid=(b // nb, t // tt, D_FF // FF_TILE),
        in_specs=[blk, per_seq(N_MOD), const((1, d)), const((d, 2 * d)), const((CONV_W, d)),
                  const((1, d)), const((LRU_BLOCKS, LRU_BLK, 2 * LRU_BLK)), const((1, d)),
                  const((1, d)), const((1, d)), const((d, d)), per_seq(CONV_W - 1), per_seq(1),
                  const((1, d)), const((1, d))] + _ffn_specs(),
        out_specs=[blk, per_seq(CONV_W - 1), per_seq(1)],
        out_shape=[jax.ShapeDtypeStruct((b, t, d), f32),
                   jax.ShapeDtypeStruct((b, CONV_W - 1, d), f32),
                   jax.ShapeDtypeStruct((b, 1, d), f32)],
        scratch_shapes=[pltpu.VMEM((nb, HALO + tt, d), f32), pltpu.VMEM((nb, 1, d), f32),
                        pltpu.VMEM((nb, tt, d), f32), pltpu.VMEM((nb * tt, d), bf16),
                        pltpu.VMEM((nb * tt, d), f32)],
        compiler_params=_params("arbitrary", "arbitrary", "arbitrary"),
        name="odd_layer_ffn",
    )(x, mod, gm, win, cw, cb, wax, ba, bx, lam, wo, conv0, h0, gf, fn, wg, wu, wd)


def _trunk(x, mods, past, weights, nb, tt, hg_tt):
    b, t, d = x.shape
    sample = past is not None
    row = lambda v: v.reshape(1, -1)

    qa, ka, va, ka_bf, va_bf, qb, lf, kb, ib, gs = _even_pre(
        x, mods[0], row(weights["norm_mix"][0]), weights["w_in_even"], weights["hg_lb_logits"],
        0, nb, tt)
    if sample:
        past_k, past_v, s0, conv0, h0 = past
        oa = _sb_sample(qa, ka_bf, va_bf, past_k, past_v, LANES)
    else:
        oa = _sb_prompt(qa, ka_bf, va_bf, LANES)
        s0 = jnp.zeros((b, HG_HEADS, HG_DK, HG_DV), f32)
        conv0 = jnp.zeros((b, CONV_W - 1, d), f32)
        h0 = jnp.zeros((b, 1, d), f32)
    ob, s_new = _hgrn(qb, lf, kb, ib, gs, row(weights["hg_gnorm"]), s0, hg_tt)
    x = _even_post(x, mods[0], oa, ob, weights["w_out_even"], row(weights["norm_ffn"][0]),
                   weights["ffn_wg"][0], weights["ffn_wu"][0], weights["ffn_wd"][0], nb, tt)

    y, conv_new, h_new = _odd_layer(
        x, mods[1], row(weights["norm_mix"][1]), weights["w_in_odd"], weights["conv_w"],
        row(weights["conv_b"]), weights["lru_wax"], row(weights["lru_ba"]), row(weights["lru_bx"]),
        row(weights["lru_lambda"]), weights["w_out_odd"], conv0, h0,
        row(weights["norm_ffn"][1]), row(weights["final_norm"]),
        weights["ffn_wg"][1], weights["ffn_wu"][1], weights["ffn_wd"][1],
        nb, tt, not sample)

    heads = lambda a: a.reshape(1, b, t, SB_HEADS, SB_DIM)
    return (y, heads(ka), heads(va), s_new[None], conv_new[None], h_new.reshape(1, b, d))


def kernel(x_prompt, x_sample, cache_sb_k, cache_sb_v, state_hgrn, state_conv, state_lru, c_prompt, c_sample, norm_mix, norm_ffn, w_ada, b_ada, w_in_even, w_out_even, hg_gnorm, hg_lb_logits, w_in_odd, conv_w, conv_b, lru_wa, lru_ba, lru_wx, lru_bx, lru_lambda, w_out_odd, ffn_wg, ffn_wu, ffn_wd, final_norm):
    bp = x_prompt.shape[0]
    bs, ts, d = x_sample.shape
    weights = {
        "norm_mix": norm_mix, "norm_ffn": norm_ffn,
        "w_in_even": w_in_even[0].astype(bf16), "w_out_even": w_out_even[0].astype(bf16),
        "hg_gnorm": hg_gnorm[0], "hg_lb_logits": hg_lb_logits,
        "w_in_odd": w_in_odd[0].astype(bf16), "conv_w": conv_w[0], "conv_b": conv_b[0],
        "lru_wax": jnp.concatenate([lru_wa[0], lru_wx[0]], axis=-1).astype(bf16),
        "lru_ba": lru_ba[0], "lru_bx": lru_bx[0], "lru_lambda": lru_lambda[0],
        "w_out_odd": w_out_odd[0].astype(bf16),
        "ffn_wg": ffn_wg.astype(bf16), "ffn_wu": ffn_wu.astype(bf16), "ffn_wd": ffn_wd.astype(bf16),
        "final_norm": final_norm,
    }
    mods = _ada(jnp.concatenate([c_prompt, c_sample], axis=0), w_ada, b_ada)
    mods = mods.reshape(DEPTH, bp + bs, N_MOD, d)

    out_p = _trunk(x_prompt, mods[:, :bp], None, weights, nb=1, tt=512, hg_tt=512)
    past = (cache_sb_k[0].reshape(bs, -1, SB_WIDTH), cache_sb_v[0].reshape(bs, -1, SB_WIDTH),
            state_hgrn[0], state_conv[0], state_lru[0].reshape(bs, 1, d))
    out_s = _trunk(x_sample, mods[:, bp:], past, weights, nb=bs, tt=ts, hg_tt=ts)

    y_p, k_p, v_p, s_p, conv_p, h_p = out_p
    y_s, k_s, v_s, s_s, conv_s, h_s = out_s
    return (y_p, y_s, k_p, v_p, s_p, conv_p, h_p, k_s, v_s, s_s, conv_s, h_s)
```

```python
import functools
import math

import jax
import jax.numpy as jnp
from jax import lax
from jax.experimental import pallas as pl
from jax.experimental.pallas import tpu as pltpu

f32 = jnp.float32
bf16 = jnp.bfloat16

D_MODEL = 1024
DEPTH = 2
EPS = 1e-6
N_MOD = 6

SB_HEADS = 8
SB_DIM = 64
SB_WIDTH = SB_HEADS * SB_DIM
HG_HEADS = 4
HG_DK = 128
HG_DV = 128
HG_W = HG_HEADS * HG_DK
SEG = 512
N_SEG = 7
HG_CHUNK = 64
HG_SUB = 16
LRU_BLOCKS = 8
LRU_BLK = D_MODEL // LRU_BLOCKS
CONV_W = 4
C_SCALE = 8.0
D_FF = 2816
FF_TILE = 256

SB_WINDOW = 2
SB_SKIP_LOG = -105.0

LANES = 128
HALO = 8
VMEM_LIMIT = 56 * 1024 * 1024

NT_DIMS = (((1,), (1,)), ((), ()))
TN_DIMS = (((0,), (0,)), ((), ()))


def _dot(a, b):
    return jnp.dot(a, b, preferred_element_type=f32)


def _dot_nt(a, b):
    return lax.dot_general(a, b, NT_DIMS, preferred_element_type=f32)


def _split3(x):
    hi = x.astype(bf16)
    r1 = x - hi.astype(f32)
    mid = r1.astype(bf16)
    lo = (r1 - mid.astype(f32)).astype(bf16)
    return hi, mid, lo


def _dot_exact_rhs(x, m):
    hi, mid, lo = _split3(x)
    return _dot(hi, m) + _dot(mid, m) + _dot(lo, m)


def _sigmoid(x):
    return jax.nn.sigmoid(x)


def _silu(x):
    return x * _sigmoid(x)


def _norm_mod(x, g, scale, shift):
    ms = jnp.mean(x * x, axis=-1, keepdims=True)
    return x * lax.rsqrt(ms + EPS) * g * (1.0 + scale) + shift


def _params(*sem):
    return pltpu.CompilerParams(dimension_semantics=sem, vmem_limit_bytes=VMEM_LIMIT)


def _ada_kernel(c_ref, w_ref, b_ref, o_ref):
    c = c_ref[...]
    sc = _silu(c)
    w = w_ref[...]
    s_hi = sc.astype(bf16)
    s_lo = (sc - s_hi.astype(f32)).astype(bf16)
    w_hi = w.astype(bf16)
    w_lo = (w - w_hi.astype(f32)).astype(bf16)
    o_ref[...] = _dot(s_hi, w_hi) + _dot(s_lo, w_hi) + _dot(s_hi, w_lo) + b_ref[...]


def _ada(c_all, w_ada, b_ada):
    rows = c_all.shape[0]
    n_out = N_MOD * D_MODEL
    tn = 1536
    return pl.pallas_call(
        _ada_kernel,
        grid=(DEPTH, n_out // tn),
        in_specs=[
            pl.BlockSpec((rows, D_MODEL), lambda l, j: (0, 0)),
            pl.BlockSpec((None, D_MODEL, tn), lambda l, j: (l, 0, j)),
            pl.BlockSpec((None, 1, tn), lambda l, j: (l, 0, j)),
        ],
        out_specs=pl.BlockSpec((None, rows, tn), lambda l, j: (l, 0, j)),
        out_shape=jax.ShapeDtypeStruct((DEPTH, rows, n_out), f32),
        compiler_params=_params("arbitrary", "arbitrary"),
        name="ada_mod",
    )(c_all, w_ada, b_ada.reshape(DEPTH, 1, n_out))


def _even_pre_kernel(layer, x_ref, mod_ref, g_ref, w_ref, lbl_ref,
                     qa_ref, ka_ref, va_ref, kab_ref, vab_ref,
                     qb_ref, lf_ref, kb_ref, ib_ref, gs_ref):
    nb, tt, d = x_ref.shape
    m = nb * tt
    hn = _norm_mod(x_ref[...], g_ref[...], mod_ref[:, 1:2, :], mod_ref[:, 0:1, :])
    hn = hn.reshape(m, d).astype(bf16)

    def seg(i):
        return _dot(hn, w_ref[:, i * SEG:(i + 1) * SEG]).reshape(nb, tt, SEG)

    qa_ref[...] = (seg(0) * (1.0 / math.sqrt(SB_DIM))).astype(bf16)
    ka = seg(1)
    ka_ref[...] = ka
    kab_ref[...] = ka.astype(bf16)
    va = seg(2)
    va_ref[...] = va
    vab_ref[...] = va.astype(bf16)
    qb_ref[...] = _silu(seg(3)) * (HG_DK ** -0.5)

    lg = lbl_ref[...]
    e = jnp.exp(lg - jnp.max(lg, axis=0, keepdims=True))
    lb = jnp.sum(e[:layer + 1], axis=0, keepdims=True) / jnp.sum(e, axis=0, keepdims=True)
    fb = seg(4)
    f = lb + (1.0 - lb) * _sigmoid(fb)
    lf_ref[...] = jnp.log(f)
    kb_ref[...] = (1.0 - lb) * _sigmoid(-fb)
    ib_ref[...] = seg(5)
    gs_ref[...] = _silu(seg(6))


def _even_pre(x, mod, g, w_bf, lb_logits, layer, nb, tt):
    b, t, d = x.shape
    grid = (b // nb, t // tt)
    blk = lambda width: pl.BlockSpec((nb, tt, width), lambda i, j: (i, j, 0))
    out_f = jax.ShapeDtypeStruct((b, t, SEG), f32)
    out_b = jax.ShapeDtypeStruct((b, t, SEG), bf16)
    return pl.pallas_call(
        functools.partial(_even_pre_kernel, layer),
        grid=grid,
        in_specs=[
            blk(d),
            pl.BlockSpec((nb, N_MOD, d), lambda i, j: (i, 0, 0)),
            pl.BlockSpec((1, d), lambda i, j: (0, 0)),
            pl.BlockSpec((d, N_SEG * SEG), lambda i, j: (0, 0)),
            pl.BlockSpec((DEPTH + 1, HG_W), lambda i, j: (0, 0)),
        ],
        out_specs=[blk(SEG)] * 10,
        out_shape=[out_b, out_f, out_f, out_b, out_b, out_f, out_f, out_f, out_f, out_f],
        compiler_params=_params("arbitrary", "arbitrary"),
        name="even_pre",
    )(x, mod, g, w_bf, lb_logits)


def _sb_consts(tk, scale=1.0):
    r = lax.broadcasted_iota(jnp.int32, (tk, LANES + tk), 0)
    c = lax.broadcasted_iota(jnp.int32, (tk, LANES + tk), 1)
    m = jnp.where((c < LANES) | (r > c - LANES), scale, 0.0).astype(bf16)
    return jnp.concatenate([m, m, m], axis=0)


def _sb_blocks(q2, blocks, carry, acc):
    zs = [_dot_nt(q2, k_blk) for k_blk, _, _, _ in blocks]
    l1mbs, sums = [], []
    for z, (_, _, sum_matrix, mask) in zip(zs, blocks):
        l1mb = -(jnp.maximum(z, 0.0) + jnp.log(1.0 + jnp.exp(-jnp.abs(z))))
        if mask is not None:
            l1mb = jnp.where(mask, l1mb, 0.0)
        l1mbs.append(l1mb)
        sums.append(_dot(jnp.concatenate(_split3(l1mb), axis=1), sum_matrix))
    for z, l1mb, s, (k_blk, v_blk, _, mask) in zip(zs, l1mbs, sums, blocks):
        tk = k_blk.shape[0]
        after = carry[:, :tk] + s[:, LANES:]
        w = jnp.exp(z + l1mb + after)
        if mask is not None:
            w = jnp.where(mask, w, 0.0)
        acc = acc + _dot(w.astype(bf16), v_blk)
        carry = carry + s[:, :LANES]
    return carry, acc


def _sb_finish(j0, carry, acc, step):
    def cond(state):
        j, carry, _ = state
        return (j >= 0) & (jnp.max(carry) > SB_SKIP_LOG)

    def body(state):
        j, carry, acc = state
        carry, acc = step(j, carry, acc)
        return j - 1, carry, acc

    return lax.while_loop(cond, body, (j0, carry, acc))[2]


def _stack_heads(q):
    lane = lax.broadcasted_iota(jnp.int32, q.shape, 1)
    zero = jnp.zeros_like(q)
    return jnp.concatenate([jnp.where(lane < SB_DIM, q, zero), jnp.where(lane >= SB_DIM, q, zero)], axis=0)


def _unstack_heads(acc2):
    tq = acc2.shape[0] // 2
    lane = lax.broadcasted_iota(jnp.int32, (tq, LANES), 1)
    return jnp.where(lane < SB_DIM, acc2[:tq], acc2[tq:])


def _causal_mask2(tq):
    r = lax.broadcasted_iota(jnp.int32, (2 * tq, tq), 0) & (tq - 1)
    c = lax.broadcasted_iota(jnp.int32, (2 * tq, tq), 1)
    return c < r


def _sb_prompt_kernel(q_ref, k_ref, v_ref, o_ref):
    tq = q_ref.shape[0]
    i = pl.program_id(2)
    q2 = _stack_heads(q_ref[...])
    sum_matrix = _sb_consts(tq)
    zero = jnp.zeros((2 * tq, LANES), f32)

    def block(j):
        rows = pl.ds(pl.multiple_of(j * tq, tq), tq)
        return k_ref[rows, :], v_ref[rows, :]

    blocks = [block(i) + (sum_matrix, _causal_mask2(tq))]
    for c in range(1, SB_WINDOW + 1):
        scale = jnp.where(i >= c, 1.0, 0.0)
        k_blk, v_blk = block(jnp.maximum(i - c, 0))
        blocks.append((k_blk, (v_blk.astype(f32) * scale).astype(bf16), _sb_consts(tq, scale), None))
    carry, acc = _sb_blocks(q2, blocks, zero, zero)

    def step(j, carry, acc):
        return _sb_blocks(q2, [block(j) + (sum_matrix, None)], carry, acc)

    acc = _sb_finish(i - SB_WINDOW - 1, carry, acc, step)
    o_ref[...] = _unstack_heads(acc).astype(o_ref.dtype)


def _sb_prompt(q, k, v, tq):
    b, t, _ = q.shape
    n_pair = SB_WIDTH // LANES
    qspec = pl.BlockSpec((None, tq, LANES), lambda bi, p, i: (bi, i, p))
    kvspec = pl.BlockSpec((None, t, LANES), lambda bi, p, i: (bi, 0, p))
    return pl.pallas_call(
        _sb_prompt_kernel,
        grid=(b, n_pair, t // tq),
        in_specs=[qspec, kvspec, kvspec],
        out_specs=qspec,
        out_shape=jax.ShapeDtypeStruct((b, t, SB_WIDTH), bf16),
        compiler_params=_params("arbitrary", "arbitrary", "arbitrary"),
        name="sb_prompt",
    )(q, k, v)


def _sb_sample_kernel(tk, q_ref, k_ref, v_ref, pk_ref, pv_ref, o_ref):
    tq = q_ref.shape[0]
    n_past = pk_ref.shape[0] // tk
    assert n_past >= SB_WINDOW
    q2 = _stack_heads(q_ref[...])
    zero = jnp.zeros((2 * tq, LANES), f32)
    sum_matrix = _sb_consts(tk)

    def block(j):
        start = j * tk if isinstance(j, int) else pl.multiple_of(j * tk, tk)
        rows = pl.ds(start, tk)
        return pk_ref[rows, :].astype(bf16), pv_ref[rows, :].astype(bf16), sum_matrix, None

    blocks = [(k_ref[...], v_ref[...], _sb_consts(tq), _causal_mask2(tq))]
    blocks += [block(n_past - c) for c in range(1, SB_WINDOW + 1)]
    carry, acc = _sb_blocks(q2, blocks, zero, zero)
    acc = _sb_finish(n_past - SB_WINDOW - 1, carry, acc,
                     lambda j, carry, acc: _sb_blocks(q2, [block(j)], carry, acc))
    o_ref[...] = _unstack_heads(acc).astype(o_ref.dtype)


def _sb_sample(q, k, v, past_k, past_v, tk):
    b, t, _ = q.shape
    p = past_k.shape[1]
    n_pair = SB_WIDTH // LANES
    qspec = pl.BlockSpec((None, t, LANES), lambda bi, pi: (bi, 0, pi))
    pspec = pl.BlockSpec((None, p, LANES), lambda bi, pi: (bi, 0, pi))
    return pl.pallas_call(
        functools.partial(_sb_sample_kernel, tk),
        grid=(b, n_pair),
        in_specs=[qspec, qspec, qspec, pspec, pspec],
        out_specs=qspec,
        out_shape=jax.ShapeDtypeStruct((b, t, SB_WIDTH), bf16),
        compiler_params=_params("arbitrary", "arbitrary"),
        name="sb_sample",
    )(q, k, v, past_k, past_v)


def _hgrn_head_chunk(q, k, v, b, st):
    n_sub = HG_CHUNK // HG_SUB
    v_bf = v.astype(bf16)
    o_inter = _dot_nt((q * jnp.exp(b)).astype(bf16), st.astype(bf16))
    t_idx = lax.broadcasted_iota(jnp.int32, (HG_SUB, HG_SUB, HG_DK), 0)
    s_idx = lax.broadcasted_iota(jnp.int32, (HG_SUB, HG_SUB, HG_DK), 1)
    causal = s_idx <= t_idx
    outs = []
    for i in range(n_sub):
        lo = i * HG_SUB
        rows = slice(lo, lo + HG_SUB)
        qi, ki, bi = q[rows], k[rows], b[rows]
        diff = bi[:, None, :] - bi[None, :, :]
        dec = jnp.exp(jnp.where(causal, diff, -1e30))
        a_diag = jnp.sum(qi[:, None, :] * ki[None, :, :] * dec, axis=-1)
        o_i = _dot(a_diag.astype(bf16), v_bf[rows])
        if i > 0:
            ref = b[lo - 1:lo]
            q_t = qi * jnp.exp(bi - ref)
            k_t = k[:lo] * jnp.exp(ref - b[:lo])
            a_off = _dot_nt(q_t.astype(bf16), k_t.astype(bf16))
            o_i = o_i + _dot(a_off.astype(bf16), v_bf[:lo])
        outs.append(o_inter[rows] + o_i)
    o = jnp.concatenate(outs, axis=0)
    b_last = b[HG_CHUNK - 1:HG_CHUNK]
    k_dec = (k * jnp.exp(b_last - b)).astype(bf16)
    st_new = jnp.exp(b_last) * st + lax.dot_general(v_bf, k_dec, TN_DIMS, preferred_element_type=f32)
    return o, st_new


def _hgrn_kernel(q_ref, lf_ref, k_ref, v_ref, gs_ref, gn_ref, s0_ref, o_ref, sout_ref, st_scr):
    tb = pl.program_id(1)
    n_tb = pl.num_programs(1)
    n_chunk = q_ref.shape[0] // HG_CHUNK

    @pl.when(tb == 0)
    def _():
        for h in range(HG_HEADS):
            st_scr[h] = s0_ref[h].T

    r = lax.broadcasted_iota(jnp.int32, (HG_CHUNK, HG_CHUNK), 0)
    c = lax.broadcasted_iota(jnp.int32, (HG_CHUNK, HG_CHUNK), 1)
    lower = jnp.where(c <= r, 1.0, 0.0).astype(bf16)

    def chunk(ci, carry):
        rows = pl.ds(pl.multiple_of(ci * HG_CHUNK, HG_CHUNK), HG_CHUNK)
        b_all = _dot_exact_rhs_lhs(lower, lf_ref[rows, :])
        for h in range(HG_HEADS):
            cols = slice(h * HG_DK, (h + 1) * HG_DK)
            o, st_new = _hgrn_head_chunk(q_ref[rows, cols], k_ref[rows, cols], v_ref[rows, cols],
                                         b_all[:, cols], st_scr[h])
            st_scr[h] = st_new
            o = o * lax.rsqrt(jnp.mean(o * o, axis=-1, keepdims=True) + EPS)
            o_ref[rows, cols] = (o * gn_ref[:, cols] * gs_ref[rows, cols]).astype(o_ref.dtype)
        return carry

    lax.fori_loop(0, n_chunk, chunk, 0)

    @pl.when(tb == n_tb - 1)
    def _():
        for h in range(HG_HEADS):
            sout_ref[h] = st_scr[h].T


def _dot_exact_rhs_lhs(m, x):
    hi, mid, lo = _split3(x)
    return _dot(m, hi) + _dot(m, mid) + _dot(m, lo)


def _hgrn(qb, lf, kb, ib, gs, gnorm, s0, tt):
    b, t, _ = qb.shape
    blk = pl.BlockSpec((None, tt, HG_W), lambda i, j: (i, j, 0))
    sspec = pl.BlockSpec((None, HG_HEADS, HG_DK, HG_DV), lambda i, j: (i, 0, 0, 0))
    return pl.pallas_call(
        _hgrn_kernel,
        grid=(b, t // tt),
        in_specs=[blk, blk, blk, blk, blk, pl.BlockSpec((1, HG_W), lambda i, j: (0, 0)), sspec],
        out_specs=[blk, sspec],
        out_shape=[jax.ShapeDtypeStruct((b, t, HG_W), bf16),
                   jax.ShapeDtypeStruct((b, HG_HEADS, HG_DK, HG_DV), f32)],
        scratch_shapes=[pltpu.VMEM((HG_HEADS, HG_DV, HG_DK), f32)],
        compiler_params=_params("arbitrary", "arbitrary"),
        name="hgrn",
    )(qb, lf, kb, ib, gs, gnorm, s0)


def _ffn_step(j, hn_scr, wg_ref, wu_ref, wd_ref, acc_scr):
    h = hn_scr[...]
    g = _dot(h, wg_ref[...])
    u = _dot(h, wu_ref[...])
    part = _dot((_silu(g) * u).astype(bf16), wd_ref[...])

    @pl.when(j == 0)
    def _():
        acc_scr[...] = part

    @pl.when(j > 0)
    def _():
        acc_scr[...] += part


def _ffn_specs():
    return [
        pl.BlockSpec((D_MODEL, FF_TILE), lambda i, t, j: (0, j)),
        pl.BlockSpec((D_MODEL, FF_TILE), lambda i, t, j: (0, j)),
        pl.BlockSpec((FF_TILE, D_MODEL), lambda i, t, j: (j, 0)),
    ]


def _even_post_kernel(x_ref, mod_ref, oa_ref, ob_ref, wo_ref, gf_ref, wg_ref, wu_ref, wd_ref,
                      out_ref, x1_scr, hn_scr, acc_scr):
    nb, tt, d = x_ref.shape
    m = nb * tt
    j = pl.program_id(2)
    n_j = pl.num_programs(2)

    @pl.when(j == 0)
    def _():
        o = (_dot(oa_ref[...].reshape(m, SB_WIDTH), wo_ref[:SB_WIDTH, :])
             + _dot(ob_ref[...].reshape(m, HG_W), wo_ref[SB_WIDTH:, :]))
        x1 = x_ref[...] + (1.0 + mod_ref[:, 2:3, :]) * o.reshape(nb, tt, d)
        x1_scr[...] = x1
        hn = _norm_mod(x1, gf_ref[...], mod_ref[:, 4:5, :], mod_ref[:, 3:4, :])
        hn_scr[...] = hn.reshape(m, d).astype(bf16)

    _ffn_step(j, hn_scr, wg_ref, wu_ref, wd_ref, acc_scr)

    @pl.when(j == n_j - 1)
    def _():
        out_ref[...] = x1_scr[...] + (1.0 + mod_ref[:, 5:6, :]) * acc_scr[...].reshape(nb, tt, d)


def _even_post(x, mod, oa, ob, wo, gf, wg, wu, wd, nb, tt):
    b, t, d = x.shape
    blk = lambda width: pl.BlockSpec((nb, tt, width), lambda i, tb, j: (i, tb, 0))
    const = lambda shape: pl.BlockSpec(shape, lambda i, tb, j: (0,) * len(shape))
    return pl.pallas_call(
        _even_post_kernel,
        grid=(b // nb, t // tt, D_FF // FF_TILE),
        in_specs=[blk(d), pl.BlockSpec((nb, N_MOD, d), lambda i, tb, j: (i, 0, 0)),
                  blk(SB_WIDTH), blk(HG_W), const((d, d)), const((1, d))] + _ffn_specs(),
        out_specs=blk(d),
        out_shape=jax.ShapeDtypeStruct((b, t, d), f32),
        scratch_shapes=[pltpu.VMEM((nb, tt, d), f32), pltpu.VMEM((nb * tt, d), bf16),
                        pltpu.VMEM((nb * tt, d), f32)],
        compiler_params=_params("arbitrary", "arbitrary", "arbitrary"),
        name="even_post_ffn",
    )(x, mod, oa, ob, wo, gf, wg, wu, wd)


def _linear_scan(a, u, tt):
    row = lax.broadcasted_iota(jnp.int32, a.shape, 0) & (tt - 1)
    d = 1
    while d < tt:
        keep = row >= d
        a_prev = jnp.where(keep, pltpu.roll(a, d, 0), 1.0)
        u_prev = jnp.where(keep, pltpu.roll(u, d, 0), 0.0)
        u = a * u_prev + u
        a = a * a_prev
        d *= 2
    return a, u


def _odd_kernel(stream_start, x_ref, mod_ref, gm_ref, win_ref, cw_ref, cb_ref, wax_ref, ba_ref, bx_ref,
                lam_ref, wo_ref, conv0_ref, h0_ref, gf_ref, fn_ref, wg_ref, wu_ref, wd_ref,
                y_ref, conv_ref, hout_ref,
                xs_scr, hprev_scr, x1_scr, hn_scr, acc_scr):
    nb, tt, d = x_ref.shape
    m = nb * tt
    tb = pl.program_id(1)
    j = pl.program_id(2)
    n_j = pl.num_programs(2)

    @pl.when(j == 0)
    def _():
        @pl.when(tb == 0)
        def _():
            xs_scr[:, HALO - (CONV_W - 1):HALO, :] = conv0_ref[...]
            hprev_scr[...] = h0_ref[...]

        x = x_ref[...]
        hn = _norm_mod(x, gm_ref[...], mod_ref[:, 1:2, :], mod_ref[:, 0:1, :])
        hn = hn.reshape(m, d).astype(bf16)
        gate = _dot(hn, win_ref[:, :d])
        xbr = _dot(hn, win_ref[:, d:]).reshape(nb, tt, d)
        xs_scr[:, HALO:HALO + tt, :] = xbr
        xc = cb_ref[...] + xbr * cw_ref[CONV_W - 1:CONV_W, :]
        for k in range(1, CONV_W):
            xc = xc + xs_scr[:, HALO - k:HALO - k + tt, :] * cw_ref[CONV_W - 1 - k:CONV_W - k, :]
        tail = xs_scr[:, HALO + tt - (CONV_W - 1):HALO + tt, :]
        conv_ref[...] = tail
        xs_scr[:, HALO - (CONV_W - 1):HALO, :] = tail

        xc = xc.reshape(m, d)
        xc_bf = xc.astype(bf16)
        ra, rx = [], []
        for blk in range(LRU_BLOCKS):
            both = _dot(xc_bf[:, blk * LRU_BLK:(blk + 1) * LRU_BLK], wax_ref[blk])
            ra.append(both[:, :LRU_BLK])
            rx.append(both[:, LRU_BLK:])
        r = _sigmoid(jnp.concatenate(ra, axis=1) + ba_ref[...])
        gi = _sigmoid(jnp.concatenate(rx, axis=1) + bx_ref[...])
        lam = lam_ref[...]
        log_sig_lam = -(jnp.maximum(-lam, 0.0) + jnp.log1p(jnp.exp(-jnp.abs(lam))))
        log_a = C_SCALE * r * log_sig_lam
        a = jnp.exp(log_a)
        mult = jnp.sqrt(-jnp.tanh(log_a) * (1.0 + a * a))
        if stream_start:
            row = (lax.broadcasted_iota(jnp.int32, (m, d), 0) & (tt - 1)) + tb * tt
            mult = jnp.where(row == 0, 1.0, mult)
        u = mult * gi * xc
        a_cum, h = _linear_scan(a, u, tt)
        h = h.reshape(nb, tt, d) + a_cum.reshape(nb, tt, d) * hprev_scr[...]
        h_last = h[:, tt - 1:tt, :]
        hprev_scr[...] = h_last
        hout_ref[...] = h_last

        y_in = (jax.nn.gelu(gate) * h.reshape(m, d)).astype(bf16)
        o = _dot(y_in, wo_ref[...])
        x1 = x + (1.0 + mod_ref[:, 2:3, :]) * o.reshape(nb, tt, d)
        x1_scr[...] = x1
        hn2 = _norm_mod(x1, gf_ref[...], mod_ref[:, 4:5, :], mod_ref[:, 3:4, :])
        hn_scr[...] = hn2.reshape(m, d).astype(bf16)

    _ffn_step(j, hn_scr, wg_ref, wu_ref, wd_ref, acc_scr)

    @pl.when(j == n_j - 1)
    def _():
        x2 = x1_scr[...] + (1.0 + mod_ref[:, 5:6, :]) * acc_scr[...].reshape(nb, tt, d)
        ms = jnp.mean(x2 * x2, axis=-1, keepdims=True)
        y_ref[...] = x2 * lax.rsqrt(ms + EPS) * fn_ref[...]


def _odd_layer(x, mod, gm, win, cw, cb, wax, ba, bx, lam, wo, conv0, h0, gf, fn, wg, wu, wd,
               nb, tt, stream_start):
    b, t, d = x.shape
    blk = pl.BlockSpec((nb, tt, d), lambda i, tb, j: (i, tb, 0))
    const = lambda shape: pl.BlockSpec(shape, lambda i, tb, j: (0,) * len(shape))
    per_seq = lambda rows: pl.BlockSpec((nb, rows, d), lambda i, tb, j: (i, 0, 0))
    return pl.pallas_call(
        functools.partial(_odd_kernel, stream_start),
        grid=(b // nb, t // tt, D_FF // FF_TILE),
        in_specs=[blk, per_seq(N_MOD), const((1, d)), const((d, 2 * d)), const((CONV_W, d)),
                  const((1, d)), const((LRU_BLOCKS, LRU_BLK, 2 * LRU_BLK)), const((1, d)),
                  const((1, d)), const((1, d)), const((d, d)), per_seq(CONV_W - 1), per_seq(1),
                  const((1, d)), const((1, d))] + _ffn_specs(),
        out_specs=[blk, per_seq(CONV_W - 1), per_seq(1)],
        out_shape=[jax.ShapeDtypeStruct((b, t, d), f32),
                   jax.ShapeDtypeStruct((b, CONV_W - 1, d), f32),
                   jax.ShapeDtypeStruct((b, 1, d), f32)],
        scratch_shapes=[pltpu.VMEM((nb, HALO + tt, d), f32), pltpu.VMEM((nb, 1, d), f32),
                        pltpu.VMEM((nb, tt, d), f32), pltpu.VMEM((nb * tt, d), bf16),
                        pltpu.VMEM((nb * tt, d), f32)],
        compiler_params=_params("arbitrary", "arbitrary", "arbitrary"),
        name="odd_layer_ffn",
    )(x, mod, gm, win, cw, cb, wax, ba, bx, lam, wo, conv0, h0, gf, fn, wg, wu, wd)


def _trunk(x, mods, past, weights, nb, tt, hg_tt):
    b, t, d = x.shape
    sample = past is not None
    row = lambda v: v.reshape(1, -1)

    qa, ka, va, ka_bf, va_bf, qb, lf, kb, ib, gs = _even_pre(
        x, mods[0], row(weights["norm_mix"][0]), weights["w_in_even"], weights["hg_lb_logits"],
        0, nb, tt)
    if sample:
        past_k, past_v, s0, conv0, h0 = past
        oa = _sb_sample(qa, ka_bf, va_bf, past_k, past_v, LANES)
    else:
        oa = _sb_prompt(qa, ka_bf, va_bf, LANES)
        s0 = jnp.zeros((b, HG_HEADS, HG_DK, HG_DV), f32)
        conv0 = jnp.zeros((b, CONV_W - 1, d), f32)
        h0 = jnp.zeros((b, 1, d), f32)
    ob, s_new = _hgrn(qb, lf, kb, ib, gs, row(weights["hg_gnorm"]), s0, hg_tt)
    x = _even_post(x, mods[0], oa, ob, weights["w_out_even"], row(weights["norm_ffn"][0]),
                   weights["ffn_wg"][0], weights["ffn_wu"][0], weights["ffn_wd"][0], nb, tt)

    y, conv_new, h_new = _odd_layer(
        x, mods[1], row(weights["norm_mix"][1]), weights["w_in_odd"], weights["conv_w"],
        row(weights["conv_b"]), weights["lru_wax"], row(weights["lru_ba"]), row(weights["lru_bx"]),
        row(weights["lru_lambda"]), weights["w_out_odd"], conv0, h0,
        row(weights["norm_ffn"][1]), row(weights["final_norm"]),
        weights["ffn_wg"][1], weights["ffn_wu"][1], weights["ffn_wd"][1],
        nb, tt, not sample)

    heads = lambda a: a.reshape(1, b, t, SB_HEADS, SB_DIM)
    return (y, heads(ka), heads(va), s_new[None], conv_new[None], h_new.reshape(1, b, d))


def kernel(x_prompt, x_sample, cache_sb_k, cache_sb_v, state_hgrn, state_conv, state_lru, c_prompt, c_sample, norm_mix, norm_ffn, w_ada, b_ada, w_in_even, w_out_even, hg_gnorm, hg_lb_logits, w_in_odd, conv_w, conv_b, lru_wa, lru_ba, lru_wx, lru_bx, lru_lambda, w_out_odd, ffn_wg, ffn_wu, ffn_wd, final_norm):
    bp = x_prompt.shape[0]
    bs, ts, d = x_sample.shape
    weights = {
        "norm_mix": norm_mix, "norm_ffn": norm_ffn,
        "w_in_even": w_in_even[0].astype(bf16), "w_out_even": w_out_even[0].astype(bf16),
        "hg_gnorm": hg_gnorm[0], "hg_lb_logits": hg_lb_logits,
        "w_in_odd": w_in_odd[0].astype(bf16), "conv_w": conv_w[0], "conv_b": conv_b[0],
        "lru_wax": jnp.concatenate([lru_wa[0], lru_wx[0]], axis=-1).astype(bf16),
        "lru_ba": lru_ba[0], "lru_bx": lru_bx[0], "lru_lambda": lru_lambda[0],
        "w_out_odd": w_out_odd[0].astype(bf16),
        "ffn_wg": ffn_wg.astype(bf16), "ffn_wu": ffn_wu.astype(bf16), "ffn_wd": ffn_wd.astype(bf16),
        "final_norm": final_norm,
    }
    mods = _ada(jnp.concatenate([c_prompt, c_sample], axis=0), w_ada, b_ada)
    mods = mods.reshape(DEPTH, bp + bs, N_MOD, d)

    out_p = _trunk(x_prompt, mods[:, :bp], None, weights, nb=1, tt=512, hg_tt=512)
    past = (cache_sb_k[0].reshape(bs, -1, SB_WIDTH), cache_sb_v[0].reshape(bs, -1, SB_WIDTH),
            state_hgrn[0], state_conv[0], state_lru[0].reshape(bs, 1, d))
    out_s = _trunk(x_sample, mods[:, bp:], past, weights, nb=bs, tt=ts, hg_tt=ts)

    y_p, k_p, v_p, s_p, conv_p, h_p = out_p
    y_s, k_s, v_s, s_s, conv_s, h_s = out_s
    return (y_p, y_s, k_p, v_p, s_p, conv_p, h_p, k_s, v_s, s_s, conv_s, h_s)
```

```python
import functools
import math

import jax
import jax.numpy as jnp
from jax import lax
from jax.experimental import pallas as pl
from jax.experimental.pallas import tpu as pltpu

f32 = jnp.float32
bf16 = jnp.bfloat16

D_MODEL = 1024
DEPTH = 2
EPS = 1e-6
N_MOD = 6

SB_HEADS = 8
SB_DIM = 64
SB_WIDTH = SB_HEADS * SB_DIM
HG_HEADS = 4
HG_DK = 128
HG_DV = 128
HG_W = HG_HEADS * HG_DK
SEG = 512
N_SEG = 7
HG_CHUNK = 64
HG_SUB = 16
LRU_BLOCKS = 8
LRU_BLK = D_MODEL // LRU_BLOCKS
CONV_W = 4
C_SCALE = 8.0
D_FF = 2816

SB_WINDOW = 2
SB_SKIP_LOG = -105.0

LANES = 128
SUBLANES = 8
HALO = 8
VMEM_LIMIT = 56 * 1024 * 1024

NT_DIMS = (((1,), (1,)), ((), ()))
TN_DIMS = (((0,), (0,)), ((), ()))


def _dot(a, b):
    return jnp.dot(a, b, preferred_element_type=f32)


def _dot_nt(a, b):
    return lax.dot_general(a, b, NT_DIMS, preferred_element_type=f32)


def _split3(x):
    hi = x.astype(bf16)
    r1 = x - hi.astype(f32)
    mid = r1.astype(bf16)
    lo = (r1 - mid.astype(f32)).astype(bf16)
    return hi, mid, lo


def _dot_exact_rhs(x, m):
    hi, mid, lo = _split3(x)
    return _dot(hi, m) + _dot(mid, m) + _dot(lo, m)


def _sigmoid(x):
    return jax.nn.sigmoid(x)


def _silu(x):
    return x * _sigmoid(x)


def _norm_mod(x, g, scale, shift):
    ms = jnp.mean(x * x, axis=-1, keepdims=True)
    return x * lax.rsqrt(ms + EPS) * g * (1.0 + scale) + shift


def _params(*sem):
    return pltpu.CompilerParams(dimension_semantics=sem, vmem_limit_bytes=VMEM_LIMIT)


def _ada_kernel(c_ref, w_ref, b_ref, o_ref):
    c = c_ref[...]
    sc = _silu(c)
    w = w_ref[...]
    s_hi = sc.astype(bf16)
    s_lo = (sc - s_hi.astype(f32)).astype(bf16)
    w_hi = w.astype(bf16)
    w_lo = (w - w_hi.astype(f32)).astype(bf16)
    o_ref[...] = _dot(s_hi, w_hi) + _dot(s_lo, w_hi) + _dot(s_hi, w_lo) + b_ref[...]


def _ada(c_all, w_ada, b_ada):
    rows = c_all.shape[0]
    n_out = N_MOD * D_MODEL
    tn = 1536
    return pl.pallas_call(
        _ada_kernel,
        grid=(DEPTH, n_out // tn),
        in_specs=[
            pl.BlockSpec((rows, D_MODEL), lambda l, j: (0, 0)),
            pl.BlockSpec((None, D_MODEL, tn), lambda l, j: (l, 0, j)),
            pl.BlockSpec((None, 1, tn), lambda l, j: (l, 0, j)),
        ],
        out_specs=pl.BlockSpec((None, rows, tn), lambda l, j: (l, 0, j)),
        out_shape=jax.ShapeDtypeStruct((DEPTH, rows, n_out), f32),
        compiler_params=_params("arbitrary", "arbitrary"),
        name="ada_mod",
    )(c_all, w_ada, b_ada.reshape(DEPTH, 1, n_out))


def _even_pre_kernel(layer, x_ref, mod_ref, g_ref, w_ref, lbl_ref,
                     qa_ref, ka_ref, va_ref, kab_ref, vab_ref,
                     qb_ref, lf_ref, kb_ref, ib_ref, gs_ref):
    nb, tt, d = x_ref.shape
    m = nb * tt
    hn = _norm_mod(x_ref[...], g_ref[...], mod_ref[:, 1:2, :], mod_ref[:, 0:1, :])
    hn = hn.reshape(m, d).astype(bf16)

    def seg(i):
        return _dot(hn, w_ref[:, i * SEG:(i + 1) * SEG]).reshape(nb, tt, SEG)

    qa_ref[...] = (seg(0) * (1.0 / math.sqrt(SB_DIM))).astype(bf16)
    ka = seg(1)
    ka_ref[...] = ka
    kab_ref[...] = ka.astype(bf16)
    va = seg(2)
    va_ref[...] = va
    vab_ref[...] = va.astype(bf16)
    qb_ref[...] = _silu(seg(3)) * (HG_DK ** -0.5)

    lg = lbl_ref[...]
    e = jnp.exp(lg - jnp.max(lg, axis=0, keepdims=True))
    lb = jnp.sum(e[:layer + 1], axis=0, keepdims=True) / jnp.sum(e, axis=0, keepdims=True)
    fb = seg(4)
    f = lb + (1.0 - lb) * _sigmoid(fb)
    lf_ref[...] = jnp.log(f)
    kb_ref[...] = (1.0 - lb) * _sigmoid(-fb)
    ib_ref[...] = seg(5)
    gs_ref[...] = _silu(seg(6))


def _even_pre(x, mod, g, w_bf, lb_logits, layer, nb, tt):
    b, t, d = x.shape
    grid = (b // nb, t // tt)
    blk = lambda width: pl.BlockSpec((nb, tt, width), lambda i, j: (i, j, 0))
    out_f = jax.ShapeDtypeStruct((b, t, SEG), f32)
    out_b = jax.ShapeDtypeStruct((b, t, SEG), bf16)
    return pl.pallas_call(
        functools.partial(_even_pre_kernel, layer),
        grid=grid,
        in_specs=[
            blk(d),
            pl.BlockSpec((nb, N_MOD, d), lambda i, j: (i, 0, 0)),
            pl.BlockSpec((1, d), lambda i, j: (0, 0)),
            pl.BlockSpec((d, N_SEG * SEG), lambda i, j: (0, 0)),
            pl.BlockSpec((DEPTH + 1, HG_W), lambda i, j: (0, 0)),
        ],
        out_specs=[blk(SEG)] * 10,
        out_shape=[out_b, out_f, out_f, out_b, out_b, out_f, out_f, out_f, out_f, out_f],
        compiler_params=_params("arbitrary", "arbitrary"),
        name="even_pre",
    )(x, mod, g, w_bf, lb_logits)


def _sb_consts(tk, scale=1.0):
    r = lax.broadcasted_iota(jnp.int32, (tk, LANES + tk), 0)
    c = lax.broadcasted_iota(jnp.int32, (tk, LANES + tk), 1)
    m = jnp.where((c < LANES) | (r > c - LANES), scale, 0.0).astype(bf16)
    return jnp.concatenate([m, m, m], axis=0)


def _sb_blocks(q2, blocks, carry, acc):
    zs = [_dot_nt(q2, k_blk) for k_blk, _, _, _ in blocks]
    l1mbs, sums = [], []
    for z, (_, _, sum_matrix, mask) in zip(zs, blocks):
        l1mb = -(jnp.maximum(z, 0.0) + jnp.log(1.0 + jnp.exp(-jnp.abs(z))))
        if mask is not None:
            l1mb = jnp.where(mask, l1mb, 0.0)
        l1mbs.append(l1mb)
        sums.append(_dot(jnp.concatenate(_split3(l1mb), axis=1), sum_matrix))
    for z, l1mb, s, (k_blk, v_blk, _, mask) in zip(zs, l1mbs, sums, blocks):
        tk = k_blk.shape[0]
        after = carry[:, :tk] + s[:, LANES:]
        w = jnp.exp(z + l1mb + after)
        if mask is not None:
            w = jnp.where(mask, w, 0.0)
        acc = acc + _dot(w.astype(bf16), v_blk)
        carry = carry + s[:, :LANES]
    return carry, acc


def _sb_finish(j0, carry, acc, step):
    def cond(state):
        j, carry, _ = state
        return (j >= 0) & (jnp.max(carry) > SB_SKIP_LOG)

    def body(state):
        j, carry, acc = state
        carry, acc = step(j, carry, acc)
        return j - 1, carry, acc

    return lax.while_loop(cond, body, (j0, carry, acc))[2]


def _stack_heads(q):
    lane = lax.broadcasted_iota(jnp.int32, q.shape, 1)
    zero = jnp.zeros_like(q)
    return jnp.concatenate([jnp.where(lane < SB_DIM, q, zero), jnp.where(lane >= SB_DIM, q, zero)], axis=0)


def _unstack_heads(acc2):
    tq = acc2.shape[0] // 2
    lane = lax.broadcasted_iota(jnp.int32, (tq, LANES), 1)
    return jnp.where(lane < SB_DIM, acc2[:tq], acc2[tq:])


def _causal_mask2(tq):
    r = lax.broadcasted_iota(jnp.int32, (2 * tq, tq), 0) & (tq - 1)
    c = lax.broadcasted_iota(jnp.int32, (2 * tq, tq), 1)
    return c < r


def _sb_prompt_kernel(q_ref, k_ref, v_ref, o_ref):
    tq = q_ref.shape[0]
    i = pl.program_id(2)
    q2 = _stack_heads(q_ref[...])
    sum_matrix = _sb_consts(tq)
    zero = jnp.zeros((2 * tq, LANES), f32)

    def block(j):
        rows = pl.ds(pl.multiple_of(j * tq, tq), tq)
        return k_ref[rows, :], v_ref[rows, :]

    blocks = [block(i) + (sum_matrix, _causal_mask2(tq))]
    for c in range(1, SB_WINDOW + 1):
        scale = jnp.where(i >= c, 1.0, 0.0)
        k_blk, v_blk = block(jnp.maximum(i - c, 0))
        blocks.append((k_blk, (v_blk.astype(f32) * scale).astype(bf16), _sb_consts(tq, scale), None))
    carry, acc = _sb_blocks(q2, blocks, zero, zero)

    def step(j, carry, acc):
        return _sb_blocks(q2, [block(j) + (sum_matrix, None)], carry, acc)

    acc = _sb_finish(i - SB_WINDOW - 1, carry, acc, step)
    o_ref[...] = _unstack_heads(acc).astype(o_ref.dtype)


def _sb_prompt(q, k, v, tq):
    b, t, _ = q.shape
    n_pair = SB_WIDTH // LANES
    qspec = pl.BlockSpec((None, tq, LANES), lambda bi, p, i: (bi, i, p))
    kvspec = pl.BlockSpec((None, t, LANES), lambda bi, p, i: (bi, 0, p))
    return pl.pallas_call(
        _sb_prompt_kernel,
        grid=(b, n_pair, t // tq),
        in_specs=[qspec, kvspec, kvspec],
        out_specs=qspec,
        out_shape=jax.ShapeDtypeStruct((b, t, SB_WIDTH), bf16),
        compiler_params=_params("arbitrary", "arbitrary", "arbitrary"),
        name="sb_prompt",
    )(q, k, v)


def _sb_sample_kernel(tk, q_ref, k_ref, v_ref, pk_ref, pv_ref, o_ref):
    tq = q_ref.shape[0]
    n_past = pk_ref.shape[0] // tk
    assert n_past >= SB_WINDOW
    q2 = _stack_heads(q_ref[...])
    zero = jnp.zeros((2 * tq, LANES), f32)
    sum_matrix = _sb_consts(tk)

    def block(j):
        start = j * tk if isinstance(j, int) else pl.multiple_of(j * tk, tk)
        rows = pl.ds(start, tk)
        return pk_ref[rows, :].astype(bf16), pv_ref[rows, :].astype(bf16), sum_matrix, None

    blocks = [(k_ref[...], v_ref[...], _sb_consts(tq), _causal_mask2(tq))]
    blocks += [block(n_past - c) for c in range(1, SB_WINDOW + 1)]
    carry, acc = _sb_blocks(q2, blocks, zero, zero)
    acc = _sb_finish(n_past - SB_WINDOW - 1, carry, acc,
                     lambda j, carry, acc: _sb_blocks(q2, [block(j)], carry, acc))
    o_ref[...] = _unstack_heads(acc).astype(o_ref.dtype)


def _sb_sample(q, k, v, past_k, past_v, tk):
    b, t, _ = q.shape
    p = past_k.shape[1]
    n_pair = SB_WIDTH // LANES
    qspec = pl.BlockSpec((None, t, LANES), lambda bi, pi: (bi, 0, pi))
    pspec = pl.BlockSpec((None, p, LANES), lambda bi, pi: (bi, 0, pi))
    return pl.pallas_call(
        functools.partial(_sb_sample_kernel, tk),
        grid=(b, n_pair),
        in_specs=[qspec, qspec, qspec, pspec, pspec],
        out_specs=qspec,
        out_shape=jax.ShapeDtypeStruct((b, t, SB_WIDTH), bf16),
        compiler_params=_params("arbitrary", "arbitrary"),
        name="sb_sample",
    )(q, k, v, past_k, past_v)


def _hgrn_head_chunk(q, k, v, b, st):
    n_sub = HG_CHUNK // HG_SUB
    v_bf = v.astype(bf16)
    o_inter = _dot_nt((q * jnp.exp(b)).astype(bf16), st.astype(bf16))
    t_idx = lax.broadcasted_iota(jnp.int32, (HG_SUB, HG_SUB, HG_DK), 0)
    s_idx = lax.broadcasted_iota(jnp.int32, (HG_SUB, HG_SUB, HG_DK), 1)
    causal = s_idx <= t_idx
    outs = []
    for i in range(n_sub):
        lo = i * HG_SUB
        rows = slice(lo, lo + HG_SUB)
        qi, ki, bi = q[rows], k[rows], b[rows]
        diff = bi[:, None, :] - bi[None, :, :]
        dec = jnp.exp(jnp.where(causal, diff, -1e30))
        a_diag = jnp.sum(qi[:, None, :] * ki[None, :, :] * dec, axis=-1)
        o_i = _dot(a_diag.astype(bf16), v_bf[rows])
        if i > 0:
            ref = b[lo - 1:lo]
            q_t = qi * jnp.exp(bi - ref)
            k_t = k[:lo] * jnp.exp(ref - b[:lo])
            a_off = _dot_nt(q_t.astype(bf16), k_t.astype(bf16))
            o_i = o_i + _dot(a_off.astype(bf16), v_bf[:lo])
        outs.append(o_inter[rows] + o_i)
    o = jnp.concatenate(outs, axis=0)
    b_last = b[HG_CHUNK - 1:HG_CHUNK]
    k_dec = (k * jnp.exp(b_last - b)).astype(bf16)
    st_new = jnp.exp(b_last) * st + lax.dot_general(v_bf, k_dec, TN_DIMS, preferred_element_type=f32)
    return o, st_new


def _hgrn_kernel(q_ref, lf_ref, k_ref, v_ref, gs_ref, gn_ref, s0_ref, o_ref, sout_ref, st_scr):
    tb = pl.program_id(1)
    n_tb = pl.num_programs(1)
    n_chunk = q_ref.shape[0] // HG_CHUNK

    @pl.when(tb == 0)
    def _():
        for h in range(HG_HEADS):
            st_scr[h] = s0_ref[h].T

    r = lax.broadcasted_iota(jnp.int32, (HG_CHUNK, HG_CHUNK), 0)
    c = lax.broadcasted_iota(jnp.int32, (HG_CHUNK, HG_CHUNK), 1)
    lower = jnp.where(c <= r, 1.0, 0.0).astype(bf16)

    def chunk(ci, carry):
        rows = pl.ds(pl.multiple_of(ci * HG_CHUNK, HG_CHUNK), HG_CHUNK)
        b_all = _dot_exact_rhs_lhs(lower, lf_ref[rows, :])
        for h in range(HG_HEADS):
            cols = slice(h * HG_DK, (h + 1) * HG_DK)
            o, st_new = _hgrn_head_chunk(q_ref[rows, cols], k_ref[rows, cols], v_ref[rows, cols],
                                         b_all[:, cols], st_scr[h])
            st_scr[h] = st_new
            o = o * lax.rsqrt(jnp.mean(o * o, axis=-1, keepdims=True) + EPS)
            o_ref[rows, cols] = (o * gn_ref[:, cols] * gs_ref[rows, cols]).astype(o_ref.dtype)
        return carry

    lax.fori_loop(0, n_chunk, chunk, 0)

    @pl.when(tb == n_tb - 1)
    def _():
        for h in range(HG_HEADS):
            sout_ref[h] = st_scr[h].T


def _dot_exact_rhs_lhs(m, x):
    hi, mid, lo = _split3(x)
    return _dot(m, hi) + _dot(m, mid) + _dot(m, lo)


def _hgrn(qb, lf, kb, ib, gs, gnorm, s0, tt):
    b, t, _ = qb.shape
    blk = pl.BlockSpec((None, tt, HG_W), lambda i, j: (i, j, 0))
    sspec = pl.BlockSpec((None, HG_HEADS, HG_DK, HG_DV), lambda i, j: (i, 0, 0, 0))
    return pl.pallas_call(
        _hgrn_kernel,
        grid=(b, t // tt),
        in_specs=[blk, blk, blk, blk, blk, pl.BlockSpec((1, HG_W), lambda i, j: (0, 0)), sspec],
        out_specs=[blk, sspec],
        out_shape=[jax.ShapeDtypeStruct((b, t, HG_W), bf16),
                   jax.ShapeDtypeStruct((b, HG_HEADS, HG_DK, HG_DV), f32)],
        scratch_shapes=[pltpu.VMEM((HG_HEADS, HG_DV, HG_DK), f32)],
        compiler_params=_params("arbitrary", "arbitrary"),
        name="hgrn",
    )(qb, lf, kb, ib, gs, gnorm, s0)


def _ffn(hn_bf, wg_ref, wu_ref, wd_ref):
    g = _dot(hn_bf, wg_ref[...])
    u = _dot(hn_bf, wu_ref[...])
    return _dot((_silu(g) * u).astype(bf16), wd_ref[...])


def _resident(shape):
    return pl.BlockSpec(shape, lambda *_: (0,) * len(shape), pipeline_mode=pl.Buffered(1))


def _ffn_specs():
    return [_resident((D_MODEL, D_FF)), _resident((D_MODEL, D_FF)), _resident((D_FF, D_MODEL))]


def _even_post_kernel(x_ref, mod_ref, oa_ref, ob_ref, wo_ref, gf_ref, wg_ref, wu_ref, wd_ref, out_ref):
    nb, tt, d = x_ref.shape
    m = nb * tt
    o = (_dot(oa_ref[...].reshape(m, SB_WIDTH), wo_ref[:SB_WIDTH, :])
         + _dot(ob_ref[...].reshape(m, HG_W), wo_ref[SB_WIDTH:, :]))
    x1 = x_ref[...] + (1.0 + mod_ref[:, 2:3, :]) * o.reshape(nb, tt, d)
    hn = _norm_mod(x1, gf_ref[...], mod_ref[:, 4:5, :], mod_ref[:, 3:4, :])
    y = _ffn(hn.reshape(m, d).astype(bf16), wg_ref, wu_ref, wd_ref)
    out_ref[...] = x1 + (1.0 + mod_ref[:, 5:6, :]) * y.reshape(nb, tt, d)


def _even_post(x, mod, oa, ob, wo, gf, wg, wu, wd, nb, tt):
    b, t, d = x.shape
    blk = lambda width: pl.BlockSpec((nb, tt, width), lambda i, tb: (i, tb, 0))
    return pl.pallas_call(
        _even_post_kernel,
        grid=(b // nb, t // tt),
        in_specs=[blk(d), pl.BlockSpec((nb, N_MOD, d), lambda i, tb: (i, 0, 0)),
                  blk(SB_WIDTH), blk(HG_W), _resident((d, d)), _resident((1, d))] + _ffn_specs(),
        out_specs=blk(d),
        out_shape=jax.ShapeDtypeStruct((b, t, d), f32),
        compiler_params=_params("arbitrary", "arbitrary"),
        name="even_post_ffn",
    )(x, mod, oa, ob, wo, gf, wg, wu, wd)


def _linear_scan(a, u, h_init, tt):
    m, d = a.shape
    n_group = m // SUBLANES
    a3 = a.reshape(n_group, SUBLANES, d)
    u3 = u.reshape(n_group, SUBLANES, d)
    sub = lax.broadcasted_iota(jnp.int32, (1, SUBLANES, d), 1)
    step = 1
    while step < SUBLANES:
        keep = sub >= step
        a_prev = jnp.where(keep, pltpu.roll(a3, step, 1), 1.0)
        u_prev = jnp.where(keep, pltpu.roll(u3, step, 1), 0.0)
        u3 = a3 * u_prev + u3
        a3 = a3 * a_prev
        step *= 2
    groups_per_seq = tt // SUBLANES
    carry = None
    groups = []
    for gi in range(n_group):
        if gi % groups_per_seq == 0:
            carry = h_init[gi // groups_per_seq]
        h_group = u3[gi] + a3[gi] * carry
        groups.append(h_group)
        carry = h_group[SUBLANES - 1:SUBLANES, :]
    return jnp.concatenate(groups, axis=0)


def _odd_kernel(stream_start, x_ref, mod_ref, gm_ref, win_ref, cw_ref, cb_ref, wax_ref, ba_ref, bx_ref,
                lam_ref, wo_ref, conv0_ref, h0_ref, gf_ref, fn_ref, wg_ref, wu_ref, wd_ref,
                y_ref, conv_ref, hout_ref, xs_scr, hprev_scr):
    nb, tt, d = x_ref.shape
    m = nb * tt
    tb = pl.program_id(1)

    @pl.when(tb == 0)
    def _():
        xs_scr[:, HALO - (CONV_W - 1):HALO, :] = conv0_ref[...]
        hprev_scr[...] = h0_ref[...]

    x = x_ref[...]
    hn = _norm_mod(x, gm_ref[...], mod_ref[:, 1:2, :], mod_ref[:, 0:1, :])
    hn = hn.reshape(m, d).astype(bf16)
    gate = _dot(hn, win_ref[:, :d])
    xbr = _dot(hn, win_ref[:, d:]).reshape(nb, tt, d)
    xs_scr[:, HALO:HALO + tt, :] = xbr
    xc = cb_ref[...] + xbr * cw_ref[CONV_W - 1:CONV_W, :]
    for k in range(1, CONV_W):
        xc = xc + xs_scr[:, HALO - k:HALO - k + tt, :] * cw_ref[CONV_W - 1 - k:CONV_W - k, :]
    tail = xs_scr[:, HALO + tt - (CONV_W - 1):HALO + tt, :]
    conv_ref[...] = tail
    xs_scr[:, HALO - (CONV_W - 1):HALO, :] = tail

    xc = xc.reshape(m, d)
    xc_bf = xc.astype(bf16)
    ra, rx = [], []
    for blk in range(LRU_BLOCKS):
        both = _dot(xc_bf[:, blk * LRU_BLK:(blk + 1) * LRU_BLK], wax_ref[blk])
        ra.append(both[:, :LRU_BLK])
        rx.append(both[:, LRU_BLK:])
    r = _sigmoid(jnp.concatenate(ra, axis=1) + ba_ref[...])
    gi = _sigmoid(jnp.concatenate(rx, axis=1) + bx_ref[...])
    lam = lam_ref[...]
    log_sig_lam = -(jnp.maximum(-lam, 0.0) + jnp.log1p(jnp.exp(-jnp.abs(lam))))
    log_a = C_SCALE * r * log_sig_lam
    a = jnp.exp(log_a)
    mult = jnp.sqrt(-jnp.tanh(log_a) * (1.0 + a * a))
    if stream_start:
        row = (lax.broadcasted_iota(jnp.int32, (m, d), 0) & (tt - 1)) + tb * tt
        mult = jnp.where(row == 0, 1.0, mult)
    h = _linear_scan(a, mult * gi * xc, hprev_scr[...], tt)
    h_last = h.reshape(nb, tt, d)[:, tt - 1:tt, :]
    hprev_scr[...] = h_last
    hout_ref[...] = h_last

    y_in = (jax.nn.gelu(gate) * h).astype(bf16)
    o = _dot(y_in, wo_ref[...])
    x1 = x + (1.0 + mod_ref[:, 2:3, :]) * o.reshape(nb, tt, d)
    hn2 = _norm_mod(x1, gf_ref[...], mod_ref[:, 4:5, :], mod_ref[:, 3:4, :])
    y = _ffn(hn2.reshape(m, d).astype(bf16), wg_ref, wu_ref, wd_ref)
    x2 = x1 + (1.0 + mod_ref[:, 5:6, :]) * y.reshape(nb, tt, d)
    ms = jnp.mean(x2 * x2, axis=-1, keepdims=True)
    y_ref[...] = x2 * lax.rsqrt(ms + EPS) * fn_ref[...]


def _odd_layer(x, mod, gm, win, cw, cb, wax, ba, bx, lam, wo, conv0, h0, gf, fn, wg, wu, wd,
               nb, tt, stream_start):
    b, t, d = x.shape
    blk = pl.BlockSpec((nb, tt, d), lambda i, tb: (i, tb, 0))
    per_seq = lambda rows: pl.BlockSpec((nb, rows, d), lambda i, tb: (i, 0, 0))
    row = _resident((1, d))
    return pl.pallas_call(
        functools.partial(_odd_kernel, stream_start),
        grid=(b // nb, t // tt),
        in_specs=[blk, per_seq(N_MOD), row, _resident((d, 2 * d)), _resident((CONV_W, d)),
                  row, _resident((LRU_BLOCKS, LRU_BLK, 2 * LRU_BLK)), row,
                  row, row, _resident((d, d)), per_seq(CONV_W - 1), per_seq(1),
                  row, row] + _ffn_specs(),
        out_specs=[blk, per_seq(CONV_W - 1), per_seq(1)],
        out_shape=[jax.ShapeDtypeStruct((b, t, d), f32),
                   jax.ShapeDtypeStruct((b, CONV_W - 1, d), f32),
                   jax.ShapeDtypeStruct((b, 1, d), f32)],
        scratch_shapes=[pltpu.VMEM((nb, HALO + tt, d), f32), pltpu.VMEM((nb, 1, d), f32)],
        compiler_params=_params("arbitrary", "arbitrary"),
        name="odd_layer_ffn",
    )(x, mod, gm, win, cw, cb, wax, ba, bx, lam, wo, conv0, h0, gf, fn, wg, wu, wd)


def _trunk(x, mods, past, weights, nb, tt, hg_tt):
    b, t, d = x.shape
    sample = past is not None
    row = lambda v: v.reshape(1, -1)

    qa, ka, va, ka_bf, va_bf, qb, lf, kb, ib, gs = _even_pre(
        x, mods[0], row(weights["norm_mix"][0]), weights["w_in_even"], weights["hg_lb_logits"],
        0, nb, tt)
    if sample:
        past_k, past_v, s0, conv0, h0 = past
        oa = _sb_sample(qa, ka_bf, va_bf, past_k, past_v, LANES)
    else:
        oa = _sb_prompt(qa, ka_bf, va_bf, LANES)
        s0 = jnp.zeros((b, HG_HEADS, HG_DK, HG_DV), f32)
        conv0 = jnp.zeros((b, CONV_W - 1, d), f32)
        h0 = jnp.zeros((b, 1, d), f32)
    ob, s_new = _hgrn(qb, lf, kb, ib, gs, row(weights["hg_gnorm"]), s0, hg_tt)
    x = _even_post(x, mods[0], oa, ob, weights["w_out_even"], row(weights["norm_ffn"][0]),
                   weights["ffn_wg"][0], weights["ffn_wu"][0], weights["ffn_wd"][0], nb, tt)

    y, conv_new, h_new = _odd_layer(
        x, mods[1], row(weights["norm_mix"][1]), weights["w_in_odd"], weights["conv_w"],
        row(weights["conv_b"]), weights["lru_wax"], row(weights["lru_ba"]), row(weights["lru_bx"]),
        row(weights["lru_lambda"]), weights["w_out_odd"], conv0, h0,
        row(weights["norm_ffn"][1]), row(weights["final_norm"]),
        weights["ffn_wg"][1], weights["ffn_wu"][1], weights["ffn_wd"][1],
        nb, tt, not sample)

    heads = lambda a: a.reshape(1, b, t, SB_HEADS, SB_DIM)
    return (y, heads(ka), heads(va), s_new[None], conv_new[None], h_new.reshape(1, b, d))


def kernel(x_prompt, x_sample, cache_sb_k, cache_sb_v, state_hgrn, state_conv, state_lru, c_prompt, c_sample, norm_mix, norm_ffn, w_ada, b_ada, w_in_even, w_out_even, hg_gnorm, hg_lb_logits, w_in_odd, conv_w, conv_b, lru_wa, lru_ba, lru_wx, lru_bx, lru_lambda, w_out_odd, ffn_wg, ffn_wu, ffn_wd, final_norm):
    bp = x_prompt.shape[0]
    bs, ts, d = x_sample.shape
    weights = {
        "norm_mix": norm_mix, "norm_ffn": norm_ffn,
        "w_in_even": w_in_even[0].astype(bf16), "w_out_even": w_out_even[0].astype(bf16),
        "hg_gnorm": hg_gnorm[0], "hg_lb_logits": hg_lb_logits,
        "w_in_odd": w_in_odd[0].astype(bf16), "conv_w": conv_w[0], "conv_b": conv_b[0],
        "lru_wax": jnp.concatenate([lru_wa[0], lru_wx[0]], axis=-1).astype(bf16),
        "lru_ba": lru_ba[0], "lru_bx": lru_bx[0], "lru_lambda": lru_lambda[0],
        "w_out_odd": w_out_odd[0].astype(bf16),
        "ffn_wg": ffn_wg.astype(bf16), "ffn_wu": ffn_wu.astype(bf16), "ffn_wd": ffn_wd.astype(bf16),
        "final_norm": final_norm,
    }
    mods = _ada(jnp.concatenate([c_prompt, c_sample], axis=0), w_ada, b_ada)
    mods = mods.reshape(DEPTH, bp + bs, N_MOD, d)

    out_p = _trunk(x_prompt, mods[:, :bp], None, weights, nb=1, tt=512, hg_tt=512)
    past = (cache_sb_k[0].reshape(bs, -1, SB_WIDTH), cache_sb_v[0].reshape(bs, -1, SB_WIDTH),
            state_hgrn[0], state_conv[0], state_lru[0].reshape(bs, 1, d))
    out_s = _trunk(x_sample, mods[:, bp:], past, weights, nb=bs, tt=ts, hg_tt=ts)

    y_p, k_p, v_p, s_p, conv_p, h_p = out_p
    y_s, k_s, v_s, s_s, conv_s, h_s = out_s
    return (y_p, y_s, k_p, v_p, s_p, conv_p, h_p, k_s, v_s, s_s, conv_s, h_s)
```

```python
import functools
import math

import jax
import jax.numpy as jnp
from jax import lax
from jax.experimental import pallas as pl
from jax.experimental.pallas import tpu as pltpu

f32 = jnp.float32
bf16 = jnp.bfloat16

D_MODEL = 1024
DEPTH = 2
EPS = 1e-6
N_MOD = 6

SB_HEADS = 8
SB_DIM = 64
SB_WIDTH = SB_HEADS * SB_DIM
HG_HEADS = 4
HG_DK = 128
HG_DV = 128
HG_W = HG_HEADS * HG_DK
SEG = 512
N_SEG = 7
HG_CHUNK = 64
HG_SUB = 16
LRU_BLOCKS = 8
LRU_BLK = D_MODEL // LRU_BLOCKS
CONV_W = 4
C_SCALE = 8.0
D_FF = 2816

SB_WINDOW = 2
SB_SKIP_LOG = -105.0

LANES = 128
SUBLANES = 8
HALO = 8
VMEM_LIMIT = 56 * 1024 * 1024

NT_DIMS = (((1,), (1,)), ((), ()))
TN_DIMS = (((0,), (0,)), ((), ()))


def _dot(a, b):
    return jnp.dot(a, b, preferred_element_type=f32)


def _dot_nt(a, b):
    return lax.dot_general(a, b, NT_DIMS, preferred_element_type=f32)


def _split3(x):
    hi = x.astype(bf16)
    r1 = x - hi.astype(f32)
    mid = r1.astype(bf16)
    lo = (r1 - mid.astype(f32)).astype(bf16)
    return hi, mid, lo


def _dot_exact_rhs(x, m):
    hi, mid, lo = _split3(x)
    return _dot(hi, m) + _dot(mid, m) + _dot(lo, m)


def _sigmoid(x):
    return jax.nn.sigmoid(x)


def _silu(x):
    return x * _sigmoid(x)


def _norm_mod(x, g, scale, shift):
    ms = jnp.mean(x * x, axis=-1, keepdims=True)
    return x * lax.rsqrt(ms + EPS) * (g * (1.0 + scale)) + shift


def _params(*sem):
    return pltpu.CompilerParams(dimension_semantics=sem, vmem_limit_bytes=VMEM_LIMIT)


def _ada_kernel(c_ref, w_ref, b_ref, o_ref):
    c = c_ref[...]
    sc = _silu(c)
    w = w_ref[...]
    s_hi = sc.astype(bf16)
    s_lo = (sc - s_hi.astype(f32)).astype(bf16)
    w_hi = w.astype(bf16)
    w_lo = (w - w_hi.astype(f32)).astype(bf16)
    o_ref[...] = _dot(s_hi, w_hi) + _dot(s_lo, w_hi) + _dot(s_hi, w_lo) + b_ref[...]


def _ada(c_all, w_ada, b_ada):
    rows = c_all.shape[0]
    n_out = N_MOD * D_MODEL
    tn = 1536
    return pl.pallas_call(
        _ada_kernel,
        grid=(DEPTH, n_out // tn),
        in_specs=[
            pl.BlockSpec((rows, D_MODEL), lambda l, j: (0, 0)),
            pl.BlockSpec((None, D_MODEL, tn), lambda l, j: (l, 0, j)),
            pl.BlockSpec((None, 1, tn), lambda l, j: (l, 0, j)),
        ],
        out_specs=pl.BlockSpec((None, rows, tn), lambda l, j: (l, 0, j)),
        out_shape=jax.ShapeDtypeStruct((DEPTH, rows, n_out), f32),
        compiler_params=_params("arbitrary", "arbitrary"),
        name="ada_mod",
    )(c_all, w_ada, b_ada.reshape(DEPTH, 1, n_out))


def _even_pre_kernel(layer, x_ref, mod_ref, g_ref, w_ref, lbl_ref,
                     qa_ref, ka_ref, va_ref, kab_ref, vab_ref,
                     qb_ref, lf_ref, kb_ref, ib_ref, gs_ref):
    nb, tt, d = x_ref.shape
    m = nb * tt
    hn = _norm_mod(x_ref[...], g_ref[...], mod_ref[:, 1:2, :], mod_ref[:, 0:1, :])
    hn = hn.reshape(m, d).astype(bf16)

    def seg(i):
        return _dot(hn, w_ref[:, i * SEG:(i + 1) * SEG]).reshape(nb, tt, SEG)

    qa_ref[...] = (seg(0) * (1.0 / math.sqrt(SB_DIM))).astype(bf16)
    ka = seg(1)
    ka_ref[...] = ka
    kab_ref[...] = ka.astype(bf16)
    va = seg(2)
    va_ref[...] = va
    vab_ref[...] = va.astype(bf16)
    qb_ref[...] = _silu(seg(3)) * (HG_DK ** -0.5)

    lg = lbl_ref[...]
    e = jnp.exp(lg - jnp.max(lg, axis=0, keepdims=True))
    lb = jnp.sum(e[:layer + 1], axis=0, keepdims=True) / jnp.sum(e, axis=0, keepdims=True)
    fb = seg(4)
    f = lb + (1.0 - lb) * _sigmoid(fb)
    lf_ref[...] = jnp.log(f)
    kb_ref[...] = (1.0 - lb) * _sigmoid(-fb)
    ib_ref[...] = seg(5)
    gs_ref[...] = _silu(seg(6))


def _even_pre(x, mod, g, w_bf, lb_logits, layer, nb, tt):
    b, t, d = x.shape
    grid = (b // nb, t // tt)
    blk = lambda width: pl.BlockSpec((nb, tt, width), lambda i, j: (i, j, 0))
    out_f = jax.ShapeDtypeStruct((b, t, SEG), f32)
    out_b = jax.ShapeDtypeStruct((b, t, SEG), bf16)
    return pl.pallas_call(
        functools.partial(_even_pre_kernel, layer),
        grid=grid,
        in_specs=[
            blk(d),
            pl.BlockSpec((nb, N_MOD, d), lambda i, j: (i, 0, 0)),
            pl.BlockSpec((1, d), lambda i, j: (0, 0)),
            pl.BlockSpec((d, N_SEG * SEG), lambda i, j: (0, 0)),
            pl.BlockSpec((DEPTH + 1, HG_W), lambda i, j: (0, 0)),
        ],
        out_specs=[blk(SEG)] * 10,
        out_shape=[out_b, out_f, out_f, out_b, out_b, out_f, out_f, out_f, out_f, out_f],
        compiler_params=_params("arbitrary", "arbitrary"),
        name="even_pre",
    )(x, mod, g, w_bf, lb_logits)


def _sb_consts(tk, scale=1.0):
    r = lax.broadcasted_iota(jnp.int32, (tk, LANES + tk), 0)
    c = lax.broadcasted_iota(jnp.int32, (tk, LANES + tk), 1)
    m = jnp.where((c < LANES) | (r > c - LANES), scale, 0.0).astype(bf16)
    return jnp.concatenate([m, m, m], axis=0)


def _sb_blocks(q2, blocks, carry, acc):
    zs = [_dot_nt(q2, k_blk) for k_blk, _, _, _ in blocks]
    l1mbs, sums = [], []
    for z, (_, _, sum_matrix, mask) in zip(zs, blocks):
        l1mb = -(jnp.maximum(z, 0.0) + jnp.log(1.0 + jnp.exp(-jnp.abs(z))))
        if mask is not None:
            l1mb = jnp.where(mask, l1mb, 0.0)
        l1mbs.append(l1mb)
        sums.append(_dot(jnp.concatenate(_split3(l1mb), axis=1), sum_matrix))
    for z, l1mb, s, (k_blk, v_blk, _, mask) in zip(zs, l1mbs, sums, blocks):
        tk = k_blk.shape[0]
        after = carry[:, :tk] + s[:, LANES:]
        w = jnp.exp(z + l1mb + after)
        if mask is not None:
            w = jnp.where(mask, w, 0.0)
        acc = acc + _dot(w.astype(bf16), v_blk)
        carry = carry + s[:, :LANES]
    return carry, acc


def _sb_finish(j0, carry, acc, step):
    def cond(state):
        j, carry, _ = state
        return (j >= 0) & (jnp.max(carry) > SB_SKIP_LOG)

    def body(state):
        j, carry, acc = state
        carry, acc = step(j, carry, acc)
        return j - 1, carry, acc

    return lax.while_loop(cond, body, (j0, carry, acc))[2]


def _stack_heads(q):
    lane = lax.broadcasted_iota(jnp.int32, q.shape, 1)
    zero = jnp.zeros_like(q)
    return jnp.concatenate([jnp.where(lane < SB_DIM, q, zero), jnp.where(lane >= SB_DIM, q, zero)], axis=0)


def _unstack_heads(acc2):
    tq = acc2.shape[0] // 2
    lane = lax.broadcasted_iota(jnp.int32, (tq, LANES), 1)
    return jnp.where(lane < SB_DIM, acc2[:tq], acc2[tq:])


def _causal_mask2(tq):
    r = lax.broadcasted_iota(jnp.int32, (2 * tq, tq), 0) & (tq - 1)
    c = lax.broadcasted_iota(jnp.int32, (2 * tq, tq), 1)
    return c < r


def _sb_prompt_kernel(q_ref, k_ref, v_ref, o_ref):
    tq = q_ref.shape[0]
    i = pl.program_id(2)
    q2 = _stack_heads(q_ref[...])
    sum_matrix = _sb_consts(tq)
    zero = jnp.zeros((2 * tq, LANES), f32)

    def block(j):
        rows = pl.ds(pl.multiple_of(j * tq, tq), tq)
        return k_ref[rows, :], v_ref[rows, :]

    blocks = [block(i) + (sum_matrix, _causal_mask2(tq))]
    for c in range(1, SB_WINDOW + 1):
        scale = jnp.where(i >= c, 1.0, 0.0)
        k_blk, v_blk = block(jnp.maximum(i - c, 0))
        blocks.append((k_blk, (v_blk.astype(f32) * scale).astype(bf16), _sb_consts(tq, scale), None))
    carry, acc = _sb_blocks(q2, blocks, zero, zero)

    def step(j, carry, acc):
        return _sb_blocks(q2, [block(j) + (sum_matrix, None)], carry, acc)

    acc = _sb_finish(i - SB_WINDOW - 1, carry, acc, step)
    o_ref[...] = _unstack_heads(acc).astype(o_ref.dtype)


def _sb_prompt(q, k, v, tq):
    b, t, _ = q.shape
    n_pair = SB_WIDTH // LANES
    qspec = pl.BlockSpec((None, tq, LANES), lambda bi, p, i: (bi, i, p))
    kvspec = pl.BlockSpec((None, t, LANES), lambda bi, p, i: (bi, 0, p))
    return pl.pallas_call(
        _sb_prompt_kernel,
        grid=(b, n_pair, t // tq),
        in_specs=[qspec, kvspec, kvspec],
        out_specs=qspec,
        out_shape=jax.ShapeDtypeStruct((b, t, SB_WIDTH), bf16),
        compiler_params=_params("arbitrary", "arbitrary", "arbitrary"),
        name="sb_prompt",
    )(q, k, v)


def _head_pair_rows(ref, offset, pair, tk):
    heads = [ref[pl.ds(offset + 2 * pair + h, tk, stride=SB_HEADS), :] for h in range(2)]
    return jnp.concatenate(heads, axis=1).astype(bf16)


def _sb_sample_kernel(tk, n_past, q_ref, k_ref, v_ref, wk_ref, wv_ref, pk_hbm, pv_hbm, o_ref,
                      kbuf, vbuf, sem):
    tq = q_ref.shape[0]
    b = pl.program_id(0)
    blk_rows = tk * SB_HEADS
    zero = jnp.zeros((2 * tq, LANES), f32)
    sum_matrix = _sb_consts(tk)
    diag_matrix = _sb_consts(tq)
    diag_mask = _causal_mask2(tq)

    def fetch(j):
        rows = pl.ds(pl.multiple_of(j * blk_rows, blk_rows), blk_rows)
        copies = [pltpu.make_async_copy(pk_hbm.at[b, rows, :], kbuf, sem.at[0]),
                  pltpu.make_async_copy(pv_hbm.at[b, rows, :], vbuf, sem.at[1])]
        for cp in copies:
            cp.start()
        for cp in copies:
            cp.wait()

    states = []
    for pair in range(SB_WIDTH // LANES):
        cols = slice(pair * LANES, (pair + 1) * LANES)
        q2 = _stack_heads(q_ref[:, cols])
        blocks = [(k_ref[:, cols], v_ref[:, cols], diag_matrix, diag_mask)]
        for c in range(1, SB_WINDOW + 1):
            off = (SB_WINDOW - c) * blk_rows
            blocks.append((_head_pair_rows(wk_ref, off, pair, tk), _head_pair_rows(wv_ref, off, pair, tk),
                           sum_matrix, None))
        states.append((q2,) + _sb_blocks(q2, blocks, zero, zero))

    for pair, (q2, carry, acc) in enumerate(states):
        def step(j, carry, acc, pair=pair, q2=q2):
            fetch(j)
            blk = (_head_pair_rows(kbuf, 0, pair, tk), _head_pair_rows(vbuf, 0, pair, tk), sum_matrix, None)
            return _sb_blocks(q2, [blk], carry, acc)

        acc = _sb_finish(n_past - SB_WINDOW - 1, carry, acc, step)
        o_ref[:, pair * LANES:(pair + 1) * LANES] = _unstack_heads(acc).astype(o_ref.dtype)


def _sb_sample(q, k, v, past_k, past_v, tk):
    b, t, _ = q.shape
    n_past = past_k.shape[1] // (tk * SB_HEADS)
    assert n_past % SB_WINDOW == 0 and n_past >= SB_WINDOW
    win_rows = SB_WINDOW * tk * SB_HEADS
    qspec = pl.BlockSpec((None, t, SB_WIDTH), lambda bi: (bi, 0, 0))
    wspec = pl.BlockSpec((None, win_rows, SB_DIM), lambda bi: (bi, n_past // SB_WINDOW - 1, 0))
    hbm = pl.BlockSpec(memory_space=pl.ANY)
    return pl.pallas_call(
        functools.partial(_sb_sample_kernel, tk, n_past),
        grid=(b,),
        in_specs=[qspec, qspec, qspec, wspec, wspec, hbm, hbm],
        out_specs=qspec,
        out_shape=jax.ShapeDtypeStruct((b, t, SB_WIDTH), bf16),
        scratch_shapes=[pltpu.VMEM((tk * SB_HEADS, SB_DIM), f32), pltpu.VMEM((tk * SB_HEADS, SB_DIM), f32),
                        pltpu.SemaphoreType.DMA((2,))],
        compiler_params=_params("arbitrary"),
        name="sb_sample",
    )(q, k, v, past_k, past_v, past_k, past_v)


def _hgrn_head_chunk(q, k, v, b, st):
    n_sub = HG_CHUNK // HG_SUB
    v_bf = v.astype(bf16)
    o_inter = _dot_nt((q * jnp.exp(b)).astype(bf16), st.astype(bf16))
    t_idx = lax.broadcasted_iota(jnp.int32, (HG_SUB, HG_SUB, HG_DK), 0)
    s_idx = lax.broadcasted_iota(jnp.int32, (HG_SUB, HG_SUB, HG_DK), 1)
    causal = s_idx <= t_idx
    outs = []
    for i in range(n_sub):
        lo = i * HG_SUB
        rows = slice(lo, lo + HG_SUB)
        qi, ki, bi = q[rows], k[rows], b[rows]
        diff = bi[:, None, :] - bi[None, :, :]
        dec = jnp.exp(jnp.where(causal, diff, -1e30))
        a_diag = jnp.sum(qi[:, None, :] * ki[None, :, :] * dec, axis=-1)
        o_i = _dot(a_diag.astype(bf16), v_bf[rows])
        if i > 0:
            ref = b[lo - 1:lo]
            q_t = qi * jnp.exp(bi - ref)
            k_t = k[:lo] * jnp.exp(ref - b[:lo])
            a_off = _dot_nt(q_t.astype(bf16), k_t.astype(bf16))
            o_i = o_i + _dot(a_off.astype(bf16), v_bf[:lo])
        outs.append(o_inter[rows] + o_i)
    o = jnp.concatenate(outs, axis=0)
    b_last = b[HG_CHUNK - 1:HG_CHUNK]
    k_dec = (k * jnp.exp(b_last - b)).astype(bf16)
    st_new = jnp.exp(b_last) * st + lax.dot_general(v_bf, k_dec, TN_DIMS, preferred_element_type=f32)
    return o, st_new


def _hgrn_kernel(q_ref, lf_ref, k_ref, v_ref, gs_ref, gn_ref, s0_ref, o_ref, sout_ref, st_scr):
    tb = pl.program_id(1)
    n_tb = pl.num_programs(1)
    n_chunk = q_ref.shape[0] // HG_CHUNK

    @pl.when(tb == 0)
    def _():
        for h in range(HG_HEADS):
            st_scr[h] = s0_ref[h].T

    r = lax.broadcasted_iota(jnp.int32, (HG_CHUNK, HG_CHUNK), 0)
    c = lax.broadcasted_iota(jnp.int32, (HG_CHUNK, HG_CHUNK), 1)
    lower = jnp.where(c <= r, 1.0, 0.0).astype(bf16)

    def chunk(ci, carry):
        rows = pl.ds(pl.multiple_of(ci * HG_CHUNK, HG_CHUNK), HG_CHUNK)
        b_all = _dot_exact_rhs_lhs(lower, lf_ref[rows, :])
        for h in range(HG_HEADS):
            cols = slice(h * HG_DK, (h + 1) * HG_DK)
            o, st_new = _hgrn_head_chunk(q_ref[rows, cols], k_ref[rows, cols], v_ref[rows, cols],
                                         b_all[:, cols], st_scr[h])
            st_scr[h] = st_new
            o = o * lax.rsqrt(jnp.mean(o * o, axis=-1, keepdims=True) + EPS)
            o_ref[rows, cols] = (o * gn_ref[:, cols] * gs_ref[rows, cols]).astype(o_ref.dtype)
        return carry

    lax.fori_loop(0, n_chunk, chunk, 0)

    @pl.when(tb == n_tb - 1)
    def _():
        for h in range(HG_HEADS):
            sout_ref[h] = st_scr[h].T


def _dot_exact_rhs_lhs(m, x):
    hi, mid, lo = _split3(x)
    return _dot(m, hi) + _dot(m, mid) + _dot(m, lo)


def _hgrn(qb, lf, kb, ib, gs, gnorm, s0, tt):
    b, t, _ = qb.shape
    blk = pl.BlockSpec((None, tt, HG_W), lambda i, j: (i, j, 0))
    sspec = pl.BlockSpec((None, HG_HEADS, HG_DK, HG_DV), lambda i, j: (i, 0, 0, 0))
    return pl.pallas_call(
        _hgrn_kernel,
        grid=(b, t // tt),
        in_specs=[blk, blk, blk, blk, blk, pl.BlockSpec((1, HG_W), lambda i, j: (0, 0)), sspec],
        out_specs=[blk, sspec],
        out_shape=[jax.ShapeDtypeStruct((b, t, HG_W), bf16),
                   jax.ShapeDtypeStruct((b, HG_HEADS, HG_DK, HG_DV), f32)],
        scratch_shapes=[pltpu.VMEM((HG_HEADS, HG_DV, HG_DK), f32)],
        compiler_params=_params("arbitrary", "arbitrary"),
        name="hgrn",
    )(qb, lf, kb, ib, gs, gnorm, s0)


def _ffn(hn_bf, wg_ref, wu_ref, wd_ref):
    g = _dot(hn_bf, wg_ref[...])
    u = _dot(hn_bf, wu_ref[...])
    return _dot((_silu(g) * u).astype(bf16), wd_ref[...])


def _resident(shape):
    return pl.BlockSpec(shape, lambda *_: (0,) * len(shape), pipeline_mode=pl.Buffered(1))


def _ffn_specs():
    return [_resident((D_MODEL, D_FF)), _resident((D_MODEL, D_FF)), _resident((D_FF, D_MODEL))]


def _even_post_kernel(x_ref, mod_ref, oa_ref, ob_ref, wo_ref, gf_ref, wg_ref, wu_ref, wd_ref, out_ref):
    nb, tt, d = x_ref.shape
    m = nb * tt
    o = (_dot(oa_ref[...].reshape(m, SB_WIDTH), wo_ref[:SB_WIDTH, :])
         + _dot(ob_ref[...].reshape(m, HG_W), wo_ref[SB_WIDTH:, :]))
    x1 = x_ref[...] + (1.0 + mod_ref[:, 2:3, :]) * o.reshape(nb, tt, d)
    hn = _norm_mod(x1, gf_ref[...], mod_ref[:, 4:5, :], mod_ref[:, 3:4, :])
    y = _ffn(hn.reshape(m, d).astype(bf16), wg_ref, wu_ref, wd_ref)
    out_ref[...] = x1 + (1.0 + mod_ref[:, 5:6, :]) * y.reshape(nb, tt, d)


def _even_post(x, mod, oa, ob, wo, gf, wg, wu, wd, nb, tt):
    b, t, d = x.shape
    blk = lambda width: pl.BlockSpec((nb, tt, width), lambda i, tb: (i, tb, 0))
    return pl.pallas_call(
        _even_post_kernel,
        grid=(b // nb, t // tt),
        in_specs=[blk(d), pl.BlockSpec((nb, N_MOD, d), lambda i, tb: (i, 0, 0)),
                  blk(SB_WIDTH), blk(HG_W), _resident((d, d)), _resident((1, d))] + _ffn_specs(),
        out_specs=blk(d),
        out_shape=jax.ShapeDtypeStruct((b, t, d), f32),
        compiler_params=_params("arbitrary", "arbitrary"),
        name="even_post_ffn",
    )(x, mod, oa, ob, wo, gf, wg, wu, wd)


def _linear_scan(a, u, h_init, tt):
    m, d = a.shape
    n_group = m // SUBLANES
    a3 = a.reshape(n_group, SUBLANES, d)
    u3 = u.reshape(n_group, SUBLANES, d)
    sub = lax.broadcasted_iota(jnp.int32, (1, SUBLANES, d), 1)
    step = 1
    while step < SUBLANES:
        keep = sub >= step
        a_prev = jnp.where(keep, pltpu.roll(a3, step, 1), 1.0)
        u_prev = jnp.where(keep, pltpu.roll(u3, step, 1), 0.0)
        u3 = a3 * u_prev + u3
        a3 = a3 * a_prev
        step *= 2
    groups_per_seq = tt // SUBLANES
    carry = None
    groups = []
    for gi in range(n_group):
        if gi % groups_per_seq == 0:
            carry = h_init[gi // groups_per_seq]
        h_group = u3[gi] + a3[gi] * carry
        groups.append(h_group)
        carry = h_group[SUBLANES - 1:SUBLANES, :]
    return jnp.concatenate(groups, axis=0)


def _odd_kernel(stream_start, x_ref, mod_ref, gm_ref, win_ref, cw_ref, cb_ref, wax_ref, ba_ref, bx_ref,
                lam_ref, wo_ref, conv0_ref, h0_ref, gf_ref, fn_ref, wg_ref, wu_ref, wd_ref,
                y_ref, conv_ref, hout_ref, xs_scr, hprev_scr):
    nb, tt, d = x_ref.shape
    m = nb * tt
    tb = pl.program_id(1)

    @pl.when(tb == 0)
    def _():
        xs_scr[:, HALO - (CONV_W - 1):HALO, :] = conv0_ref[...]
        hprev_scr[...] = h0_ref[...]

    x = x_ref[...]
    hn = _norm_mod(x, gm_ref[...], mod_ref[:, 1:2, :], mod_ref[:, 0:1, :])
    hn = hn.reshape(m, d).astype(bf16)
    gate = _dot(hn, win_ref[:, :d])
    xbr = _dot(hn, win_ref[:, d:]).reshape(nb, tt, d)
    xs_scr[:, HALO:HALO + tt, :] = xbr
    xc = cb_ref[...] + xbr * cw_ref[CONV_W - 1:CONV_W, :]
    for k in range(1, CONV_W):
        xc = xc + xs_scr[:, HALO - k:HALO - k + tt, :] * cw_ref[CONV_W - 1 - k:CONV_W - k, :]
    tail = xs_scr[:, HALO + tt - (CONV_W - 1):HALO + tt, :]
    conv_ref[...] = tail
    xs_scr[:, HALO - (CONV_W - 1):HALO, :] = tail

    xc = xc.reshape(m, d)
    xc_bf = xc.astype(bf16)
    ra, rx = [], []
    for blk in range(LRU_BLOCKS):
        both = _dot(xc_bf[:, blk * LRU_BLK:(blk + 1) * LRU_BLK], wax_ref[blk])
        ra.append(both[:, :LRU_BLK])
        rx.append(both[:, LRU_BLK:])
    r = _sigmoid(jnp.concatenate(ra, axis=1) + ba_ref[...])
    gi = _sigmoid(jnp.concatenate(rx, axis=1) + bx_ref[...])
    lam = lam_ref[...]
    log_sig_lam = -(jnp.maximum(-lam, 0.0) + jnp.log1p(jnp.exp(-jnp.abs(lam))))
    log_a = C_SCALE * r * log_sig_lam
    a = jnp.exp(log_a)
    mult = jnp.sqrt(-jnp.tanh(log_a) * (1.0 + a * a))
    if stream_start:
        row = (lax.broadcasted_iota(jnp.int32, (m, d), 0) & (tt - 1)) + tb * tt
        mult = jnp.where(row == 0, 1.0, mult)
    h = _linear_scan(a, mult * gi * xc, hprev_scr[...], tt)
    h_last = h.reshape(nb, tt, d)[:, tt - 1:tt, :]
    hprev_scr[...] = h_last
    hout_ref[...] = h_last

    y_in = (jax.nn.gelu(gate) * h).astype(bf16)
    o = _dot(y_in, wo_ref[...])
    x1 = x + (1.0 + mod_ref[:, 2:3, :]) * o.reshape(nb, tt, d)
    hn2 = _norm_mod(x1, gf_ref[...], mod_ref[:, 4:5, :], mod_ref[:, 3:4, :])
    y = _ffn(hn2.reshape(m, d).astype(bf16), wg_ref, wu_ref, wd_ref)
    x2 = x1 + (1.0 + mod_ref[:, 5:6, :]) * y.reshape(nb, tt, d)
    ms = jnp.mean(x2 * x2, axis=-1, keepdims=True)
    y_ref[...] = x2 * lax.rsqrt(ms + EPS) * fn_ref[...]


def _odd_layer(x, mod, gm, win, cw, cb, wax, ba, bx, lam, wo, conv0, h0, gf, fn, wg, wu, wd,
               nb, tt, stream_start):
    b, t, d = x.shape
    blk = pl.BlockSpec((nb, tt, d), lambda i, tb: (i, tb, 0))
    per_seq = lambda rows: pl.BlockSpec((nb, rows, d), lambda i, tb: (i, 0, 0))
    row = _resident((1, d))
    return pl.pallas_call(
        functools.partial(_odd_kernel, stream_start),
        grid=(b // nb, t // tt),
        in_specs=[blk, per_seq(N_MOD), row, _resident((d, 2 * d)), _resident((CONV_W, d)),
                  row, _resident((LRU_BLOCKS, LRU_BLK, 2 * LRU_BLK)), row,
                  row, row, _resident((d, d)), per_seq(CONV_W - 1), per_seq(1),
                  row, row] + _ffn_specs(),
        out_specs=[blk, per_seq(CONV_W - 1), per_seq(1)],
        out_shape=[jax.ShapeDtypeStruct((b, t, d), f32),
                   jax.ShapeDtypeStruct((b, CONV_W - 1, d), f32),
                   jax.ShapeDtypeStruct((b, 1, d), f32)],
        scratch_shapes=[pltpu.VMEM((nb, HALO + tt, d), f32), pltpu.VMEM((nb, 1, d), f32)],
        compiler_params=_params("arbitrary", "arbitrary"),
        name="odd_layer_ffn",
    )(x, mod, gm, win, cw, cb, wax, ba, bx, lam, wo, conv0, h0, gf, fn, wg, wu, wd)


def _trunk(x, mods, past, weights, nb, tt, hg_tt):
    b, t, d = x.shape
    sample = past is not None
    row = lambda v: v.reshape(1, -1)

    qa, ka, va, ka_bf, va_bf, qb, lf, kb, ib, gs = _even_pre(
        x, mods[0], row(weights["norm_mix"][0]), weights["w_in_even"], weights["hg_lb_logits"],
        0, nb, tt)
    if sample:
        past_k, past_v, s0, conv0, h0 = past
        oa = _sb_sample(qa, ka_bf, va_bf, past_k, past_v, LANES)
    else:
        oa = _sb_prompt(qa, ka_bf, va_bf, LANES)
        s0 = jnp.zeros((b, HG_HEADS, HG_DK, HG_DV), f32)
        conv0 = jnp.zeros((b, CONV_W - 1, d), f32)
        h0 = jnp.zeros((b, 1, d), f32)
    ob, s_new = _hgrn(qb, lf, kb, ib, gs, row(weights["hg_gnorm"]), s0, hg_tt)
    x = _even_post(x, mods[0], oa, ob, weights["w_out_even"], row(weights["norm_ffn"][0]),
                   weights["ffn_wg"][0], weights["ffn_wu"][0], weights["ffn_wd"][0], nb, tt)

    y, conv_new, h_new = _odd_layer(
        x, mods[1], row(weights["norm_mix"][1]), weights["w_in_odd"], weights["conv_w"],
        row(weights["conv_b"]), weights["lru_wax"], row(weights["lru_ba"]), row(weights["lru_bx"]),
        row(weights["lru_lambda"]), weights["w_out_odd"], conv0, h0,
        row(weights["norm_ffn"][1]), row(weights["final_norm"]),
        weights["ffn_wg"][1], weights["ffn_wu"][1], weights["ffn_wd"][1],
        nb, tt, not sample)

    heads = lambda a: a.reshape(1, b, t, SB_HEADS, SB_DIM)
    return (y, heads(ka), heads(va), s_new[None], conv_new[None], h_new.reshape(1, b, d))


def kernel(x_prompt, x_sample, cache_sb_k, cache_sb_v, state_hgrn, state_conv, state_lru, c_prompt, c_sample, norm_mix, norm_ffn, w_ada, b_ada, w_in_even, w_out_even, hg_gnorm, hg_lb_logits, w_in_odd, conv_w, conv_b, lru_wa, lru_ba, lru_wx, lru_bx, lru_lambda, w_out_odd, ffn_wg, ffn_wu, ffn_wd, final_norm):
    bp = x_prompt.shape[0]
    bs, ts, d = x_sample.shape
    weights = {
        "norm_mix": norm_mix, "norm_ffn": norm_ffn,
        "w_in_even": w_in_even[0].astype(bf16), "w_out_even": w_out_even[0].astype(bf16),
        "hg_gnorm": hg_gnorm[0], "hg_lb_logits": hg_lb_logits,
        "w_in_odd": w_in_odd[0].astype(bf16), "conv_w": conv_w[0], "conv_b": conv_b[0],
        "lru_wax": jnp.concatenate([lru_wa[0], lru_wx[0]], axis=-1).astype(bf16),
        "lru_ba": lru_ba[0], "lru_bx": lru_bx[0], "lru_lambda": lru_lambda[0],
        "w_out_odd": w_out_odd[0].astype(bf16),
        "ffn_wg": ffn_wg.astype(bf16), "ffn_wu": ffn_wu.astype(bf16), "ffn_wd": ffn_wd.astype(bf16),
        "final_norm": final_norm,
    }
    mods = _ada(jnp.concatenate([c_prompt, c_sample], axis=0), w_ada, b_ada)
    mods = mods.reshape(DEPTH, bp + bs, N_MOD, d)

    out_p = _trunk(x_prompt, mods[:, :bp], None, weights, nb=1, tt=512, hg_tt=512)
    past = (cache_sb_k[0].reshape(bs, -1, SB_DIM), cache_sb_v[0].reshape(bs, -1, SB_DIM),
            state_hgrn[0], state_conv[0], state_lru[0].reshape(bs, 1, d))
    out_s = _trunk(x_sample, mods[:, bp:], past, weights, nb=bs, tt=ts, hg_tt=ts)

    y_p, k_p, v_p, s_p, conv_p, h_p = out_p
    y_s, k_s, v_s, s_s, conv_s, h_s = out_s
    return (y_p, y_s, k_p, v_p, s_p, conv_p, h_p, k_s, v_s, s_s, conv_s, h_s)
```

```python
import functools
import math

import jax
import jax.numpy as jnp
from jax import lax
from jax.experimental import pallas as pl
from jax.experimental.pallas import tpu as pltpu

f32 = jnp.float32
bf16 = jnp.bfloat16

D_MODEL = 1024
DEPTH = 2
EPS = 1e-6
N_MOD = 6

SB_HEADS = 8
SB_DIM = 64
SB_WIDTH = SB_HEADS * SB_DIM
HG_HEADS = 4
HG_DK = 128
HG_DV = 128
HG_W = HG_HEADS * HG_DK
SEG = 512
N_SEG = 7
HG_CHUNK = 64
HG_SUB = 16
LRU_BLOCKS = 8
LRU_BLK = D_MODEL // LRU_BLOCKS
CONV_W = 4
C_SCALE = 8.0
D_FF = 2816

SB_WINDOW = 2
SB_SKIP_LOG = -105.0

LANES = 128
SUBLANES = 8
HALO = 8
VMEM_LIMIT = 56 * 1024 * 1024

NT_DIMS = (((1,), (1,)), ((), ()))
TN_DIMS = (((0,), (0,)), ((), ()))


def _dot(a, b):
    return jnp.dot(a, b, preferred_element_type=f32)


def _dot_nt(a, b):
    return lax.dot_general(a, b, NT_DIMS, preferred_element_type=f32)


def _split3(x):
    hi = x.astype(bf16)
    r1 = x - hi.astype(f32)
    mid = r1.astype(bf16)
    lo = (r1 - mid.astype(f32)).astype(bf16)
    return hi, mid, lo


def _dot_exact_rhs(x, m):
    hi, mid, lo = _split3(x)
    return _dot(hi, m) + _dot(mid, m) + _dot(lo, m)


def _sigmoid(x):
    return jax.nn.sigmoid(x)


def _silu(x):
    return x * _sigmoid(x)


def _norm_mod(x, g, scale, shift):
    ms = jnp.mean(x * x, axis=-1, keepdims=True)
    return x * lax.rsqrt(ms + EPS) * (g * (1.0 + scale)) + shift


def _params(*sem):
    return pltpu.CompilerParams(dimension_semantics=sem, vmem_limit_bytes=VMEM_LIMIT)


def _ada_kernel(c_ref, w_ref, b_ref, o_ref):
    c = c_ref[...]
    sc = _silu(c)
    w = w_ref[...]
    s_hi = sc.astype(bf16)
    s_lo = (sc - s_hi.astype(f32)).astype(bf16)
    w_hi = w.astype(bf16)
    w_lo = (w - w_hi.astype(f32)).astype(bf16)
    o_ref[...] = _dot(s_hi, w_hi) + _dot(s_lo, w_hi) + _dot(s_hi, w_lo) + b_ref[...]


def _ada(c_all, w_ada, b_ada):
    rows = c_all.shape[0]
    n_out = N_MOD * D_MODEL
    tn = 1536
    return pl.pallas_call(
        _ada_kernel,
        grid=(DEPTH, n_out // tn),
        in_specs=[
            pl.BlockSpec((rows, D_MODEL), lambda l, j: (0, 0)),
            pl.BlockSpec((None, D_MODEL, tn), lambda l, j: (l, 0, j)),
            pl.BlockSpec((None, 1, tn), lambda l, j: (l, 0, j)),
        ],
        out_specs=pl.BlockSpec((None, rows, tn), lambda l, j: (l, 0, j)),
        out_shape=jax.ShapeDtypeStruct((DEPTH, rows, n_out), f32),
        compiler_params=_params("arbitrary", "arbitrary"),
        name="ada_mod",
    )(c_all, w_ada, b_ada.reshape(DEPTH, 1, n_out))


def _even_pre_kernel(layer, kv_transposed, x_ref, mod_ref, g_ref, w_ref, wkvt_ref, lbl_ref,
                     qa_ref, ka_ref, va_ref, kab_ref, vab_ref,
                     qb_ref, lf_ref, kb_ref, ib_ref, gs_ref):
    nb, tt, d = x_ref.shape
    m = nb * tt
    hn = _norm_mod(x_ref[...], g_ref[...], mod_ref[:, 1:2, :], mod_ref[:, 0:1, :])
    hn = hn.reshape(m, d).astype(bf16)

    def seg(i):
        return _dot(hn, w_ref[:, i * SEG:(i + 1) * SEG]).reshape(nb, tt, SEG)

    qa_ref[...] = (seg(0) * (1.0 / math.sqrt(SB_DIM))).astype(bf16)
    if kv_transposed:
        for i, (full_ref, blocks_ref) in enumerate(((ka_ref, kab_ref), (va_ref, vab_ref))):
            kt = _dot_nt(wkvt_ref[i], hn)
            full_ref[...] = kt
            for c in range(m // LANES):
                blocks_ref[c] = kt[:, c * LANES:(c + 1) * LANES].astype(bf16)
    else:
        ka = seg(1)
        ka_ref[...] = ka
        kab_ref[...] = ka.astype(bf16)
        va = seg(2)
        va_ref[...] = va
        vab_ref[...] = va.astype(bf16)
    qb_ref[...] = _silu(seg(3)) * (HG_DK ** -0.5)

    lg = lbl_ref[...]
    e = jnp.exp(lg - jnp.max(lg, axis=0, keepdims=True))
    lb = jnp.sum(e[:layer + 1], axis=0, keepdims=True) / jnp.sum(e, axis=0, keepdims=True)
    fb = seg(4)
    f = lb + (1.0 - lb) * _sigmoid(fb)
    lf_ref[...] = jnp.log(f)
    kb_ref[...] = (1.0 - lb) * _sigmoid(-fb)
    ib_ref[...] = seg(5)
    gs_ref[...] = _silu(seg(6))


def _even_pre(x, mod, g, w_bf, wkvt_bf, lb_logits, layer, nb, tt, kv_transposed):
    b, t, d = x.shape
    grid = (b // nb, t // tt)
    blk = lambda width: pl.BlockSpec((nb, tt, width), lambda i, j: (i, j, 0))
    out_f = jax.ShapeDtypeStruct((b, t, SEG), f32)
    out_b = jax.ShapeDtypeStruct((b, t, SEG), bf16)
    if kv_transposed:
        assert nb == 1 and tt % LANES == 0
        kv_f, kv_b = jax.ShapeDtypeStruct((b, SEG, t), f32), jax.ShapeDtypeStruct((b, t // LANES, SEG, LANES), bf16)
        kv_f_spec = pl.BlockSpec((None, SEG, tt), lambda i, j: (i, 0, j))
        kv_b_spec = pl.BlockSpec((None, tt // LANES, SEG, LANES), lambda i, j: (i, j, 0, 0))
    else:
        kv_f, kv_b, kv_f_spec, kv_b_spec = out_f, out_b, blk(SEG), blk(SEG)
    return pl.pallas_call(
        functools.partial(_even_pre_kernel, layer, kv_transposed),
        grid=grid,
        in_specs=[
            blk(d),
            pl.BlockSpec((nb, N_MOD, d), lambda i, j: (i, 0, 0)),
            _resident((1, d)),
            _resident((d, N_SEG * SEG)),
            _resident((2, SEG, d)),
            _resident((DEPTH + 1, HG_W)),
        ],
        out_specs=[blk(SEG), kv_f_spec, kv_f_spec, kv_b_spec, kv_b_spec] + [blk(SEG)] * 5,
        out_shape=[out_b, kv_f, kv_f, kv_b, kv_b, out_f, out_f, out_f, out_f, out_f],
        compiler_params=_params("arbitrary", "arbitrary"),
        name="even_pre",
    )(x, mod, g, w_bf, wkvt_bf, lb_logits)


def _sb_consts(tk, scale=1.0):
    r = lax.broadcasted_iota(jnp.int32, (tk, LANES + tk), 0)
    c = lax.broadcasted_iota(jnp.int32, (tk, LANES + tk), 1)
    m = jnp.where((c < LANES) | (r > c - LANES), scale, 0.0).astype(bf16)
    return jnp.concatenate([m, m, m], axis=0)


def _sb_blocks(q2, blocks, carry, acc):
    zs = [_dot(q2, k_blk) if transposed else _dot_nt(q2, k_blk)
          for k_blk, _, _, _, transposed in blocks]
    l1mbs, sums = [], []
    for z, (_, _, sum_matrix, mask, _) in zip(zs, blocks):
        l1mb = -(jnp.maximum(z, 0.0) + jnp.log(1.0 + jnp.exp(-jnp.abs(z))))
        if mask is not None:
            l1mb = jnp.where(mask, l1mb, 0.0)
        l1mbs.append(l1mb)
        sums.append(_dot(jnp.concatenate(_split3(l1mb), axis=1), sum_matrix))
    for z, l1mb, s, (_, v_blk, _, mask, transposed) in zip(zs, l1mbs, sums, blocks):
        tk = z.shape[1]
        after = carry[:, :tk] + s[:, LANES:]
        w = jnp.exp(z + l1mb + after)
        if mask is not None:
            w = jnp.where(mask, w, 0.0)
        w = w.astype(bf16)
        acc = acc + (_dot_nt(w, v_blk) if transposed else _dot(w, v_blk))
        carry = carry + s[:, :LANES]
    return carry, acc


def _sb_finish(j0, carry, acc, step):
    def cond(state):
        j, carry, _ = state
        return (j >= 0) & (jnp.max(carry) > SB_SKIP_LOG)

    def body(state):
        j, carry, acc = state
        carry, acc = step(j, carry, acc)
        return j - 1, carry, acc

    return lax.while_loop(cond, body, (j0, carry, acc))[2]


def _stack_heads(q):
    lane = lax.broadcasted_iota(jnp.int32, q.shape, 1)
    zero = jnp.zeros_like(q)
    return jnp.concatenate([jnp.where(lane < SB_DIM, q, zero), jnp.where(lane >= SB_DIM, q, zero)], axis=0)


def _unstack_heads(acc2):
    tq = acc2.shape[0] // 2
    lane = lax.broadcasted_iota(jnp.int32, (tq, LANES), 1)
    return jnp.where(lane < SB_DIM, acc2[:tq], acc2[tq:])


def _causal_mask2(tq):
    r = lax.broadcasted_iota(jnp.int32, (2 * tq, tq), 0) & (tq - 1)
    c = lax.broadcasted_iota(jnp.int32, (2 * tq, tq), 1)
    return c < r


def _sb_prompt_kernel(q_ref, k_ref, v_ref, o_ref):
    tq = q_ref.shape[0]
    i = pl.program_id(2)
    q2 = _stack_heads(q_ref[...])
    sum_matrix = _sb_consts(tq)
    zero = jnp.zeros((2 * tq, LANES), f32)

    blocks = [(k_ref[i], v_ref[i], sum_matrix, _causal_mask2(tq), True)]
    for c in range(1, SB_WINDOW + 1):
        scale = jnp.where(i >= c, 1.0, 0.0)
        j = jnp.maximum(i - c, 0)
        blocks.append((k_ref[j], (v_ref[j].astype(f32) * scale).astype(bf16), _sb_consts(tq, scale), None, True))
    carry, acc = _sb_blocks(q2, blocks, zero, zero)

    def step(j, carry, acc):
        return _sb_blocks(q2, [(k_ref[j], v_ref[j], sum_matrix, None, True)], carry, acc)

    acc = _sb_finish(i - SB_WINDOW - 1, carry, acc, step)
    o_ref[...] = _unstack_heads(acc).astype(o_ref.dtype)


def _sb_prompt(q, k_blocks, v_blocks):
    b, t, _ = q.shape
    tq = LANES
    n_pair = SB_WIDTH // LANES
    qspec = pl.BlockSpec((None, tq, LANES), lambda bi, p, i: (bi, i, p))
    kvspec = pl.BlockSpec((None, t // tq, LANES, tq), lambda bi, p, i: (bi, 0, p, 0))
    return pl.pallas_call(
        _sb_prompt_kernel,
        grid=(b, n_pair, t // tq),
        in_specs=[qspec, kvspec, kvspec],
        out_specs=qspec,
        out_shape=jax.ShapeDtypeStruct((b, t, SB_WIDTH), bf16),
        compiler_params=_params("arbitrary", "arbitrary", "arbitrary"),
        name="sb_prompt",
    )(q, k_blocks, v_blocks)


def _sb_sample_kernel(tk, n_past, q_ref, k_ref, v_ref, wk_ref, wv_ref, pk_hbm, pv_hbm, o_ref,
                      kbuf, vbuf, sem):
    tq = q_ref.shape[0]
    b = pl.program_id(0)
    zero = jnp.zeros((2 * tq, LANES), f32)
    sum_matrix = _sb_consts(tk)
    diag_matrix = _sb_consts(tq)
    diag_mask = _causal_mask2(tq)

    def fetch(j):
        keys = pl.ds(pl.multiple_of(j * tk, tk), tk)
        copies = [pltpu.make_async_copy(pk_hbm.at[b, :, keys], kbuf, sem.at[0]),
                  pltpu.make_async_copy(pv_hbm.at[b, :, keys], vbuf, sem.at[1])]
        for cp in copies:
            cp.start()
        for cp in copies:
            cp.wait()

    states = []
    for pair in range(SB_WIDTH // LANES):
        cols = slice(pair * LANES, (pair + 1) * LANES)
        q2 = _stack_heads(q_ref[:, cols])
        blocks = [(k_ref[:, cols], v_ref[:, cols], diag_matrix, diag_mask, False)]
        for c in range(1, SB_WINDOW + 1):
            keys = slice((SB_WINDOW - c) * tk, (SB_WINDOW - c + 1) * tk)
            blocks.append((wk_ref[cols, keys].astype(bf16), wv_ref[cols, keys].astype(bf16),
                           sum_matrix, None, True))
        states.append((q2,) + _sb_blocks(q2, blocks, zero, zero))

    for pair, (q2, carry, acc) in enumerate(states):
        cols = slice(pair * LANES, (pair + 1) * LANES)

        def step(j, carry, acc, cols=cols, q2=q2):
            fetch(j)
            blk = (kbuf[cols, :].astype(bf16), vbuf[cols, :].astype(bf16), sum_matrix, None, True)
            return _sb_blocks(q2, [blk], carry, acc)

        acc = _sb_finish(n_past - SB_WINDOW - 1, carry, acc, step)
        o_ref[:, cols] = _unstack_heads(acc).astype(o_ref.dtype)


def _sb_sample(q, k, v, past_kt, past_vt, tk):
    b, t, _ = q.shape
    n_past = past_kt.shape[2] // tk
    assert n_past % SB_WINDOW == 0 and n_past >= SB_WINDOW
    qspec = pl.BlockSpec((None, t, SB_WIDTH), lambda bi: (bi, 0, 0))
    wspec = pl.BlockSpec((None, SB_WIDTH, SB_WINDOW * tk), lambda bi: (bi, 0, n_past // SB_WINDOW - 1))
    hbm = pl.BlockSpec(memory_space=pl.ANY)
    return pl.pallas_call(
        functools.partial(_sb_sample_kernel, tk, n_past),
        grid=(b,),
        in_specs=[qspec, qspec, qspec, wspec, wspec, hbm, hbm],
        out_specs=qspec,
        out_shape=jax.ShapeDtypeStruct((b, t, SB_WIDTH), bf16),
        scratch_shapes=[pltpu.VMEM((SB_WIDTH, tk), f32), pltpu.VMEM((SB_WIDTH, tk), f32),
                        pltpu.SemaphoreType.DMA((2,))],
        compiler_params=_params("arbitrary"),
        name="sb_sample",
    )(q, k, v, past_kt, past_vt, past_kt, past_vt)


def _hgrn_head_chunk(q, k, v, b, st):
    n_sub = HG_CHUNK // HG_SUB
    v_bf = v.astype(bf16)
    o_inter = _dot_nt((q * jnp.exp(b)).astype(bf16), st.astype(bf16))
    t_idx = lax.broadcasted_iota(jnp.int32, (HG_SUB, HG_SUB, HG_DK), 0)
    s_idx = lax.broadcasted_iota(jnp.int32, (HG_SUB, HG_SUB, HG_DK), 1)
    causal = s_idx <= t_idx
    outs = []
    for i in range(n_sub):
        lo = i * HG_SUB
        rows = slice(lo, lo + HG_SUB)
        qi, ki, bi = q[rows], k[rows], b[rows]
        diff = bi[:, None, :] - bi[None, :, :]
        dec = jnp.exp(jnp.where(causal, diff, -1e30))
        a_diag = jnp.sum(qi[:, None, :] * ki[None, :, :] * dec, axis=-1)
        o_i = _dot(a_diag.astype(bf16), v_bf[rows])
        if i > 0:
            ref = b[lo - 1:lo]
            q_t = qi * jnp.exp(bi - ref)
            k_t = k[:lo] * jnp.exp(ref - b[:lo])
            a_off = _dot_nt(q_t.astype(bf16), k_t.astype(bf16))
            o_i = o_i + _dot(a_off.astype(bf16), v_bf[:lo])
        outs.append(o_inter[rows] + o_i)
    o = jnp.concatenate(outs, axis=0)
    b_last = b[HG_CHUNK - 1:HG_CHUNK]
    k_dec = (k * jnp.exp(b_last - b)).astype(bf16)
    st_new = jnp.exp(b_last) * st + lax.dot_general(v_bf, k_dec, TN_DIMS, preferred_element_type=f32)
    return o, st_new


def _hgrn_kernel(q_ref, lf_ref, k_ref, v_ref, gs_ref, gn_ref, s0_ref, o_ref, sout_ref, st_scr):
    tb = pl.program_id(1)
    n_tb = pl.num_programs(1)
    n_chunk = q_ref.shape[0] // HG_CHUNK

    @pl.when(tb == 0)
    def _():
        for h in range(HG_HEADS):
            st_scr[h] = s0_ref[h].T

    r = lax.broadcasted_iota(jnp.int32, (HG_CHUNK, HG_CHUNK), 0)
    c = lax.broadcasted_iota(jnp.int32, (HG_CHUNK, HG_CHUNK), 1)
    lower = jnp.where(c <= r, 1.0, 0.0).astype(bf16)

    def chunk(ci, carry):
        rows = pl.ds(pl.multiple_of(ci * HG_CHUNK, HG_CHUNK), HG_CHUNK)
        b_all = _dot_exact_rhs_lhs(lower, lf_ref[rows, :])
        for h in range(HG_HEADS):
            cols = slice(h * HG_DK, (h + 1) * HG_DK)
            o, st_new = _hgrn_head_chunk(q_ref[rows, cols], k_ref[rows, cols], v_ref[rows, cols],
                                         b_all[:, cols], st_scr[h])
            st_scr[h] = st_new
            o = o * lax.rsqrt(jnp.mean(o * o, axis=-1, keepdims=True) + EPS)
            o_ref[rows, cols] = (o * gn_ref[:, cols] * gs_ref[rows, cols]).astype(o_ref.dtype)
        return carry

    lax.fori_loop(0, n_chunk, chunk, 0)

    @pl.when(tb == n_tb - 1)
    def _():
        for h in range(HG_HEADS):
            sout_ref[h] = st_scr[h].T


def _dot_exact_rhs_lhs(m, x):
    hi, mid, lo = _split3(x)
    return _dot(m, hi) + _dot(m, mid) + _dot(m, lo)


def _hgrn(qb, lf, kb, ib, gs, gnorm, s0, tt):
    b, t, _ = qb.shape
    blk = pl.BlockSpec((None, tt, HG_W), lambda i, j: (i, j, 0))
    sspec = pl.BlockSpec((None, HG_HEADS, HG_DK, HG_DV), lambda i, j: (i, 0, 0, 0))
    return pl.pallas_call(
        _hgrn_kernel,
        grid=(b, t // tt),
        in_specs=[blk, blk, blk, blk, blk, pl.BlockSpec((1, HG_W), lambda i, j: (0, 0)), sspec],
        out_specs=[blk, sspec],
        out_shape=[jax.ShapeDtypeStruct((b, t, HG_W), bf16),
                   jax.ShapeDtypeStruct((b, HG_HEADS, HG_DK, HG_DV), f32)],
        scratch_shapes=[pltpu.VMEM((HG_HEADS, HG_DV, HG_DK), f32)],
        compiler_params=_params("arbitrary", "arbitrary"),
        name="hgrn",
    )(qb, lf, kb, ib, gs, gnorm, s0)


def _ffn(hn_bf, wg_ref, wu_ref, wd_ref):
    g = _dot(hn_bf, wg_ref[...])
    u = _dot(hn_bf, wu_ref[...])
    return _dot((_silu(g) * u).astype(bf16), wd_ref[...])


def _resident(shape):
    return pl.BlockSpec(shape, lambda *_: (0,) * len(shape), pipeline_mode=pl.Buffered(1))


def _ffn_specs():
    return [_resident((D_MODEL, D_FF)), _resident((D_MODEL, D_FF)), _resident((D_FF, D_MODEL))]


def _even_post_kernel(x_ref, mod_ref, oa_ref, ob_ref, wo_ref, gf_ref, wg_ref, wu_ref, wd_ref, out_ref):
    nb, tt, d = x_ref.shape
    m = nb * tt
    o = (_dot(oa_ref[...].reshape(m, SB_WIDTH), wo_ref[:SB_WIDTH, :])
         + _dot(ob_ref[...].reshape(m, HG_W), wo_ref[SB_WIDTH:, :]))
    x1 = x_ref[...] + (1.0 + mod_ref[:, 2:3, :]) * o.reshape(nb, tt, d)
    hn = _norm_mod(x1, gf_ref[...], mod_ref[:, 4:5, :], mod_ref[:, 3:4, :])
    y = _ffn(hn.reshape(m, d).astype(bf16), wg_ref, wu_ref, wd_ref)
    out_ref[...] = x1 + (1.0 + mod_ref[:, 5:6, :]) * y.reshape(nb, tt, d)


def _even_post(x, mod, oa, ob, wo, gf, wg, wu, wd, nb, tt):
    b, t, d = x.shape
    blk = lambda width: pl.BlockSpec((nb, tt, width), lambda i, tb: (i, tb, 0))
    return pl.pallas_call(
        _even_post_kernel,
        grid=(b // nb, t // tt),
        in_specs=[blk(d), pl.BlockSpec((nb, N_MOD, d), lambda i, tb: (i, 0, 0)),
                  blk(SB_WIDTH), blk(HG_W), _resident((d, d)), _resident((1, d))] + _ffn_specs(),
        out_specs=blk(d),
        out_shape=jax.ShapeDtypeStruct((b, t, d), f32),
        compiler_params=_params("arbitrary", "arbitrary"),
        name="even_post_ffn",
    )(x, mod, oa, ob, wo, gf, wg, wu, wd)


def _linear_scan(a, u, h_init, tt):
    m, d = a.shape
    n_group = m // SUBLANES
    a3 = a.reshape(n_group, SUBLANES, d)
    u3 = u.reshape(n_group, SUBLANES, d)
    sub = lax.broadcasted_iota(jnp.int32, (1, SUBLANES, d), 1)
    step = 1
    while step < SUBLANES:
        keep = sub >= step
        a_prev = jnp.where(keep, pltpu.roll(a3, step, 1), 1.0)
        u_prev = jnp.where(keep, pltpu.roll(u3, step, 1), 0.0)
        u3 = a3 * u_prev + u3
        a3 = a3 * a_prev
        step *= 2
    groups_per_seq = tt // SUBLANES
    carry = None
    groups = []
    for gi in range(n_group):
        if gi % groups_per_seq == 0:
            carry = h_init[gi // groups_per_seq]
        h_group = u3[gi] + a3[gi] * carry
        groups.append(h_group)
        carry = h_group[SUBLANES - 1:SUBLANES, :]
    return jnp.concatenate(groups, axis=0)


def _odd_kernel(stream_start, x_ref, mod_ref, gm_ref, win_ref, cw_ref, cb_ref, wax_ref, ba_ref, bx_ref,
                lam_ref, wo_ref, conv0_ref, h0_ref, gf_ref, fn_ref, wg_ref, wu_ref, wd_ref,
                y_ref, conv_ref, hout_ref, xs_scr, hprev_scr):
    nb, tt, d = x_ref.shape
    m = nb * tt
    tb = pl.program_id(1)

    @pl.when(tb == 0)
    def _():
        xs_scr[:, HALO - (CONV_W - 1):HALO, :] = conv0_ref[...]
        hprev_scr[...] = h0_ref[...]

    x = x_ref[...]
    hn = _norm_mod(x, gm_ref[...], mod_ref[:, 1:2, :], mod_ref[:, 0:1, :])
    hn = hn.reshape(m, d).astype(bf16)
    gate = _dot(hn, win_ref[:, :d])
    xbr = _dot(hn, win_ref[:, d:]).reshape(nb, tt, d)
    xs_scr[:, HALO:HALO + tt, :] = xbr
    xc = cb_ref[...] + xbr * cw_ref[CONV_W - 1:CONV_W, :]
    for k in range(1, CONV_W):
        xc = xc + xs_scr[:, HALO - k:HALO - k + tt, :] * cw_ref[CONV_W - 1 - k:CONV_W - k, :]
    tail = xs_scr[:, HALO + tt - (CONV_W - 1):HALO + tt, :]
    conv_ref[...] = tail
    xs_scr[:, HALO - (CONV_W - 1):HALO, :] = tail

    xc = xc.reshape(m, d)
    xc_bf = xc.astype(bf16)
    ra, rx = [], []
    for blk in range(LRU_BLOCKS):
        both = _dot(xc_bf[:, blk * LRU_BLK:(blk + 1) * LRU_BLK], wax_ref[blk])
        ra.append(both[:, :LRU_BLK])
        rx.append(both[:, LRU_BLK:])
    r = _sigmoid(jnp.concatenate(ra, axis=1) + ba_ref[...])
    gi = _sigmoid(jnp.concatenate(rx, axis=1) + bx_ref[...])
    lam = lam_ref[...]
    log_sig_lam = -(jnp.maximum(-lam, 0.0) + jnp.log1p(jnp.exp(-jnp.abs(lam))))
    log_a = C_SCALE * r * log_sig_lam
    a = jnp.exp(log_a)
    mult = jnp.sqrt(-jnp.tanh(log_a) * (1.0 + a * a))
    if stream_start:
        row = (lax.broadcasted_iota(jnp.int32, (m, d), 0) & (tt - 1)) + tb * tt
        mult = jnp.where(row == 0, 1.0, mult)
    h = _linear_scan(a, mult * gi * xc, hprev_scr[...], tt)
    h_last = h.reshape(nb, tt, d)[:, tt - 1:tt, :]
    hprev_scr[...] = h_last
    hout_ref[...] = h_last

    y_in = (jax.nn.gelu(gate) * h).astype(bf16)
    o = _dot(y_in, wo_ref[...])
    x1 = x + (1.0 + mod_ref[:, 2:3, :]) * o.reshape(nb, tt, d)
    hn2 = _norm_mod(x1, gf_ref[...], mod_ref[:, 4:5, :], mod_ref[:, 3:4, :])
    y = _ffn(hn2.reshape(m, d).astype(bf16), wg_ref, wu_ref, wd_ref)
    x2 = x1 + (1.0 + mod_ref[:, 5:6, :]) * y.reshape(nb, tt, d)
    ms = jnp.mean(x2 * x2, axis=-1, keepdims=True)
    y_ref[...] = x2 * lax.rsqrt(ms + EPS) * fn_ref[...]


def _odd_layer(x, mod, gm, win, cw, cb, wax, ba, bx, lam, wo, conv0, h0, gf, fn, wg, wu, wd,
               nb, tt, stream_start):
    b, t, d = x.shape
    blk = pl.BlockSpec((nb, tt, d), lambda i, tb: (i, tb, 0))
    per_seq = lambda rows: pl.BlockSpec((nb, rows, d), lambda i, tb: (i, 0, 0))
    row = _resident((1, d))
    return pl.pallas_call(
        functools.partial(_odd_kernel, stream_start),
        grid=(b // nb, t // tt),
        in_specs=[blk, per_seq(N_MOD), row, _resident((d, 2 * d)), _resident((CONV_W, d)),
                  row, _resident((LRU_BLOCKS, LRU_BLK, 2 * LRU_BLK)), row,
                  row, row, _resident((d, d)), per_seq(CONV_W - 1), per_seq(1),
                  row, row] + _ffn_specs(),
        out_specs=[blk, per_seq(CONV_W - 1), per_seq(1)],
        out_shape=[jax.ShapeDtypeStruct((b, t, d), f32),
                   jax.ShapeDtypeStruct((b, CONV_W - 1, d), f32),
                   jax.ShapeDtypeStruct((b, 1, d), f32)],
        scratch_shapes=[pltpu.VMEM((nb, HALO + tt, d), f32), pltpu.VMEM((nb, 1, d), f32)],
        compiler_params=_params("arbitrary", "arbitrary"),
        name="odd_layer_ffn",
    )(x, mod, gm, win, cw, cb, wax, ba, bx, lam, wo, conv0, h0, gf, fn, wg, wu, wd)


def _trunk(x, mods, past, weights, nb, tt, hg_tt):
    b, t, d = x.shape
    sample = past is not None
    row = lambda v: v.reshape(1, -1)

    qa, ka, va, ka_bf, va_bf, qb, lf, kb, ib, gs = _even_pre(
        x, mods[0], row(weights["norm_mix"][0]), weights["w_in_even"], weights["w_kv_t"],
        weights["hg_lb_logits"], 0, nb, tt, not sample)
    if sample:
        past_k, past_v, s0, conv0, h0 = past
        oa = _sb_sample(qa, ka_bf, va_bf, past_k, past_v, LANES)
        heads = lambda a: a.reshape(1, b, t, SB_HEADS, SB_DIM)
    else:
        oa = _sb_prompt(qa, ka_bf, va_bf)
        heads = lambda a: a.reshape(b, SB_HEADS, SB_DIM, t).transpose(0, 3, 1, 2)[None]
        s0 = jnp.zeros((b, HG_HEADS, HG_DK, HG_DV), f32)
        conv0 = jnp.zeros((b, CONV_W - 1, d), f32)
        h0 = jnp.zeros((b, 1, d), f32)
    ob, s_new = _hgrn(qb, lf, kb, ib, gs, row(weights["hg_gnorm"]), s0, hg_tt)
    x = _even_post(x, mods[0], oa, ob, weights["w_out_even"], row(weights["norm_ffn"][0]),
                   weights["ffn_wg"][0], weights["ffn_wu"][0], weights["ffn_wd"][0], nb, tt)

    y, conv_new, h_new = _odd_layer(
        x, mods[1], row(weights["norm_mix"][1]), weights["w_in_odd"], weights["conv_w"],
        row(weights["conv_b"]), weights["lru_wax"], row(weights["lru_ba"]), row(weights["lru_bx"]),
        row(weights["lru_lambda"]), weights["w_out_odd"], conv0, h0,
        row(weights["norm_ffn"][1]), row(weights["final_norm"]),
        weights["ffn_wg"][1], weights["ffn_wu"][1], weights["ffn_wd"][1],
        nb, tt, not sample)

    return (y, heads(ka), heads(va), s_new[None], conv_new[None], h_new.reshape(1, b, d))


def kernel(x_prompt, x_sample, cache_sb_k, cache_sb_v, state_hgrn, state_conv, state_lru, c_prompt, c_sample, norm_mix, norm_ffn, w_ada, b_ada, w_in_even, w_out_even, hg_gnorm, hg_lb_logits, w_in_odd, conv_w, conv_b, lru_wa, lru_ba, lru_wx, lru_bx, lru_lambda, w_out_odd, ffn_wg, ffn_wu, ffn_wd, final_norm):
    bp = x_prompt.shape[0]
    bs, ts, d = x_sample.shape
    weights = {
        "norm_mix": norm_mix, "norm_ffn": norm_ffn,
        "w_in_even": w_in_even[0].astype(bf16), "w_out_even": w_out_even[0].astype(bf16),
        "w_kv_t": jnp.stack([w_in_even[0][:, SEG:2 * SEG].T, w_in_even[0][:, 2 * SEG:3 * SEG].T]).astype(bf16),
        "hg_gnorm": hg_gnorm[0], "hg_lb_logits": hg_lb_logits,
        "w_in_odd": w_in_odd[0].astype(bf16), "conv_w": conv_w[0], "conv_b": conv_b[0],
        "lru_wax": jnp.concatenate([lru_wa[0], lru_wx[0]], axis=-1).astype(bf16),
        "lru_ba": lru_ba[0], "lru_bx": lru_bx[0], "lru_lambda": lru_lambda[0],
        "w_out_odd": w_out_odd[0].astype(bf16),
        "ffn_wg": ffn_wg.astype(bf16), "ffn_wu": ffn_wu.astype(bf16), "ffn_wd": ffn_wd.astype(bf16),
        "final_norm": final_norm,
    }
    mods = _ada(jnp.concatenate([c_prompt, c_sample], axis=0), w_ada, b_ada)
    mods = mods.reshape(DEPTH, bp + bs, N_MOD, d)

    out_p = _trunk(x_prompt, mods[:, :bp], None, weights, nb=1, tt=512, hg_tt=512)
    cache_t = lambda c: c[0].transpose(0, 2, 3, 1).reshape(bs, SB_WIDTH, -1)
    past = (cache_t(cache_sb_k), cache_t(cache_sb_v),
            state_hgrn[0], state_conv[0], state_lru[0].reshape(bs, 1, d))
    out_s = _trunk(x_sample, mods[:, bp:], past, weights, nb=bs, tt=ts, hg_tt=ts)

    y_p, k_p, v_p, s_p, conv_p, h_p = out_p
    y_s, k_s, v_s, s_s, conv_s, h_s = out_s
    return (y_p, y_s, k_p, v_p, s_p, conv_p, h_p, k_s, v_s, s_s, conv_s, h_s)
```

```python
import functools
import math

import jax
import jax.numpy as jnp
from jax import lax
from jax.experimental import pallas as pl
from jax.experimental.pallas import tpu as pltpu

f32 = jnp.float32
bf16 = jnp.bfloat16

D_MODEL = 1024
DEPTH = 2
EPS = 1e-6
N_MOD = 6

SB_HEADS = 8
SB_DIM = 64
SB_WIDTH = SB_HEADS * SB_DIM
HG_HEADS = 4
HG_DK = 128
HG_DV = 128
HG_W = HG_HEADS * HG_DK
SEG = 512
N_SEG = 7
HG_CHUNK = 64
HG_SUB = 16
HG_MATMUL_MIN_LOG = -40.0
LRU_BLOCKS = 8
LRU_BLK = D_MODEL // LRU_BLOCKS
CONV_W = 4
C_SCALE = 8.0
D_FF = 2816

SB_WINDOW = 2
SB_QBLOCKS = 2
SB_SKIP_LOG = -105.0

LANES = 128
SUBLANES = 8
HALO = 8
VMEM_LIMIT = 56 * 1024 * 1024

NT_DIMS = (((1,), (1,)), ((), ()))
TN_DIMS = (((0,), (0,)), ((), ()))


def _dot(a, b):
    return jnp.dot(a, b, preferred_element_type=f32)


def _dot_nt(a, b):
    return lax.dot_general(a, b, NT_DIMS, preferred_element_type=f32)


def _split3(x):
    hi = x.astype(bf16)
    r1 = x - hi.astype(f32)
    mid = r1.astype(bf16)
    lo = (r1 - mid.astype(f32)).astype(bf16)
    return hi, mid, lo


def _dot_exact_rhs(x, m):
    hi, mid, lo = _split3(x)
    return _dot(hi, m) + _dot(mid, m) + _dot(lo, m)


def _sigmoid(x):
    return jax.nn.sigmoid(x)


def _silu(x):
    return x * _sigmoid(x)


def _norm_mod(x, g, scale, shift):
    ms = jnp.mean(x * x, axis=-1, keepdims=True)
    return x * lax.rsqrt(ms + EPS) * (g * (1.0 + scale)) + shift


def _params(*sem):
    return pltpu.CompilerParams(dimension_semantics=sem, vmem_limit_bytes=VMEM_LIMIT)


def _ada_kernel(c_ref, w_ref, b_ref, o_ref):
    c = c_ref[...]
    sc = _silu(c)
    w = w_ref[...]
    s_hi = sc.astype(bf16)
    s_lo = (sc - s_hi.astype(f32)).astype(bf16)
    w_hi = w.astype(bf16)
    w_lo = (w - w_hi.astype(f32)).astype(bf16)
    o_ref[...] = _dot(s_hi, w_hi) + _dot(s_lo, w_hi) + _dot(s_hi, w_lo) + b_ref[...]


def _ada(c_all, w_ada, b_ada):
    rows = c_all.shape[0]
    n_out = N_MOD * D_MODEL
    tn = 1536
    return pl.pallas_call(
        _ada_kernel,
        grid=(DEPTH, n_out // tn),
        in_specs=[
            pl.BlockSpec((rows, D_MODEL), lambda l, j: (0, 0)),
            pl.BlockSpec((None, D_MODEL, tn), lambda l, j: (l, 0, j)),
            pl.BlockSpec((None, 1, tn), lambda l, j: (l, 0, j)),
        ],
        out_specs=pl.BlockSpec((None, rows, tn), lambda l, j: (l, 0, j)),
        out_shape=jax.ShapeDtypeStruct((DEPTH, rows, n_out), f32),
        compiler_params=_params("arbitrary", "arbitrary"),
        name="ada_mod",
    )(c_all, w_ada, b_ada.reshape(DEPTH, 1, n_out))


def _even_pre_kernel(layer, kv_transposed, x_ref, mod_ref, g_ref, w_ref, wkvt_ref, lbl_ref,
                     qa_ref, ka_ref, va_ref, kab_ref, vab_ref,
                     qb_ref, lf_ref, kb_ref, ib_ref, gs_ref):
    nb, tt, d = x_ref.shape
    m = nb * tt
    hn = _norm_mod(x_ref[...], g_ref[...], mod_ref[:, 1:2, :], mod_ref[:, 0:1, :])
    hn = hn.reshape(m, d).astype(bf16)

    def seg(i):
        return _dot(hn, w_ref[:, i * SEG:(i + 1) * SEG]).reshape(nb, tt, SEG)

    qa_ref[...] = (seg(0) * (1.0 / math.sqrt(SB_DIM))).astype(bf16)
    if kv_transposed:
        for i, (full_ref, blocks_ref) in enumerate(((ka_ref, kab_ref), (va_ref, vab_ref))):
            kt = _dot_nt(wkvt_ref[i], hn)
            full_ref[...] = kt
            for c in range(m // LANES):
                blocks_ref[c] = kt[:, c * LANES:(c + 1) * LANES].astype(bf16)
    else:
        ka = seg(1)
        ka_ref[...] = ka
        kab_ref[...] = ka.astype(bf16)
        va = seg(2)
        va_ref[...] = va
        vab_ref[...] = va.astype(bf16)
    qb_ref[...] = _silu(seg(3)) * (HG_DK ** -0.5)

    lg = lbl_ref[...]
    e = jnp.exp(lg - jnp.max(lg, axis=0, keepdims=True))
    lb = jnp.sum(e[:layer + 1], axis=0, keepdims=True) / jnp.sum(e, axis=0, keepdims=True)
    fb = seg(4)
    f = lb + (1.0 - lb) * _sigmoid(fb)
    lf_ref[...] = jnp.log(f)
    kb_ref[...] = (1.0 - lb) * _sigmoid(-fb)
    ib_ref[...] = seg(5)
    gs_ref[...] = _silu(seg(6))


def _even_pre(x, mod, g, w_bf, wkvt_bf, lb_logits, layer, nb, tt, kv_transposed):
    b, t, d = x.shape
    grid = (b // nb, t // tt)
    blk = lambda width: pl.BlockSpec((nb, tt, width), lambda i, j: (i, j, 0))
    out_f = jax.ShapeDtypeStruct((b, t, SEG), f32)
    out_b = jax.ShapeDtypeStruct((b, t, SEG), bf16)
    if kv_transposed:
        assert nb == 1 and tt % LANES == 0
        kv_f, kv_b = jax.ShapeDtypeStruct((b, SEG, t), f32), jax.ShapeDtypeStruct((b, t // LANES, SEG, LANES), bf16)
        kv_f_spec = pl.BlockSpec((None, SEG, tt), lambda i, j: (i, 0, j))
        kv_b_spec = pl.BlockSpec((None, tt // LANES, SEG, LANES), lambda i, j: (i, j, 0, 0))
    else:
        kv_f, kv_b, kv_f_spec, kv_b_spec = out_f, out_b, blk(SEG), blk(SEG)
    return pl.pallas_call(
        functools.partial(_even_pre_kernel, layer, kv_transposed),
        grid=grid,
        in_specs=[
            blk(d),
            pl.BlockSpec((nb, N_MOD, d), lambda i, j: (i, 0, 0)),
            _resident((1, d)),
            _resident((d, N_SEG * SEG)),
            _resident((2, SEG, d)),
            _resident((DEPTH + 1, HG_W)),
        ],
        out_specs=[blk(SEG), kv_f_spec, kv_f_spec, kv_b_spec, kv_b_spec] + [blk(SEG)] * 5,
        out_shape=[out_b, kv_f, kv_f, kv_b, kv_b, out_f, out_f, out_f, out_f, out_f],
        compiler_params=_params("arbitrary", "arbitrary"),
        name="even_pre",
    )(x, mod, g, w_bf, wkvt_bf, lb_logits)


def _sb_consts(tk, scale=1.0):
    r = lax.broadcasted_iota(jnp.int32, (tk, LANES + tk), 0)
    c = lax.broadcasted_iota(jnp.int32, (tk, LANES + tk), 1)
    m = jnp.where((c < LANES) | (r > c - LANES), scale, 0.0).astype(bf16)
    return jnp.concatenate([m, m, m], axis=0)


def _sb_groups(groups):
    zs = [[_dot(q2, k_blk) if transposed else _dot_nt(q2, k_blk)
           for k_blk, _, _, _, transposed in blocks] for q2, blocks, _, _ in groups]
    l1mbs, sums = [], []
    for g, (_, blocks, _, _) in enumerate(groups):
        l1mbs.append([])
        sums.append([])
        for z, (_, _, sum_matrix, mask, _) in zip(zs[g], blocks):
            l1mb = -(jnp.maximum(z, 0.0) + jnp.log(1.0 + jnp.exp(-jnp.abs(z))))
            if mask is not None:
                l1mb = jnp.where(mask, l1mb, 0.0)
            l1mbs[g].append(l1mb)
            sums[g].append(_dot(jnp.concatenate(_split3(l1mb), axis=1), sum_matrix))
    results = []
    for g, (_, blocks, carry, acc) in enumerate(groups):
        for z, l1mb, s, (_, v_blk, _, mask, transposed) in zip(zs[g], l1mbs[g], sums[g], blocks):
            tk = z.shape[1]
            after = carry[:, :tk] + s[:, LANES:]
            w = jnp.exp(z + l1mb + after)
            if mask is not None:
                w = jnp.where(mask, w, 0.0)
            w = w.astype(bf16)
            acc = acc + (_dot_nt(w, v_blk) if transposed else _dot(w, v_blk))
            carry = carry + s[:, :LANES]
        results.append((carry, acc))
    return results


def _sb_blocks(q2, blocks, carry, acc):
    return _sb_groups([(q2, blocks, carry, acc)])[0]


def _sb_finish(j0, carry, acc, step):
    def cond(state):
        j, carry, _ = state
        return (j >= 0) & (jnp.max(carry) > SB_SKIP_LOG)

    def body(state):
        j, carry, acc = state
        carry, acc = step(j, carry, acc)
        return j - 1, carry, acc

    return lax.while_loop(cond, body, (j0, carry, acc))[2]


def _stack_heads(q):
    lane = lax.broadcasted_iota(jnp.int32, q.shape, 1)
    zero = jnp.zeros_like(q)
    return jnp.concatenate([jnp.where(lane < SB_DIM, q, zero), jnp.where(lane >= SB_DIM, q, zero)], axis=0)


def _unstack_heads(acc2):
    tq = acc2.shape[0] // 2
    lane = lax.broadcasted_iota(jnp.int32, (tq, LANES), 1)
    return jnp.where(lane < SB_DIM, acc2[:tq], acc2[tq:])


def _causal_mask2(tq):
    r = lax.broadcasted_iota(jnp.int32, (2 * tq, tq), 0) & (tq - 1)
    c = lax.broadcasted_iota(jnp.int32, (2 * tq, tq), 1)
    return c < r


def _sb_prompt_kernel(q_ref, k_ref, v_ref, o_ref):
    tq = k_ref.shape[2]
    n_q = q_ref.shape[0] // tq
    i0 = pl.program_id(2) * n_q
    sum_matrix = _sb_consts(tq)
    diag_mask = _causal_mask2(tq)
    zero = jnp.zeros((2 * tq, LANES), f32)

    groups = []
    for g in range(n_q):
        i = i0 + g
        q2 = _stack_heads(q_ref[g * tq:(g + 1) * tq, :])
        blocks = [(k_ref[i], v_ref[i], sum_matrix, diag_mask, True)]
        for c in range(1, SB_WINDOW + 1):
            scale = jnp.where(i >= c, 1.0, 0.0)
            j = jnp.maximum(i - c, 0)
            blocks.append((k_ref[j], (v_ref[j].astype(f32) * scale).astype(bf16), _sb_consts(tq, scale),
                           None, True))
        groups.append((q2, blocks, zero, zero))

    for g, (carry, acc) in enumerate(_sb_groups(groups)):
        q2 = groups[g][0]

        def step(j, carry, acc, q2=q2):
            return _sb_blocks(q2, [(k_ref[j], v_ref[j], sum_matrix, None, True)], carry, acc)

        acc = _sb_finish(i0 + g - SB_WINDOW - 1, carry, acc, step)
        o_ref[g * tq:(g + 1) * tq, :] = _unstack_heads(acc).astype(o_ref.dtype)


def _sb_prompt(q, k_blocks, v_blocks):
    b, t, _ = q.shape
    tk = LANES
    tq = SB_QBLOCKS * tk
    n_pair = SB_WIDTH // LANES
    qspec = pl.BlockSpec((None, tq, LANES), lambda bi, p, i: (bi, i, p))
    kvspec = pl.BlockSpec((None, t // tk, LANES, tk), lambda bi, p, i: (bi, 0, p, 0))
    return pl.pallas_call(
        _sb_prompt_kernel,
        grid=(b, n_pair, t // tq),
        in_specs=[qspec, kvspec, kvspec],
        out_specs=qspec,
        out_shape=jax.ShapeDtypeStruct((b, t, SB_WIDTH), bf16),
        compiler_params=_params("arbitrary", "arbitrary", "arbitrary"),
        name="sb_prompt",
    )(q, k_blocks, v_blocks)


def _sb_sample_kernel(tk, n_past, q_ref, k_ref, v_ref, wk_ref, wv_ref, pk_hbm, pv_hbm, o_ref,
                      kbuf, vbuf, sem):
    tq = q_ref.shape[0]
    b = pl.program_id(0)
    zero = jnp.zeros((2 * tq, LANES), f32)
    sum_matrix = _sb_consts(tk)
    diag_matrix = _sb_consts(tq)
    diag_mask = _causal_mask2(tq)

    def fetch(j):
        keys = pl.ds(pl.multiple_of(j * tk, tk), tk)
        copies = [pltpu.make_async_copy(pk_hbm.at[b, :, keys], kbuf, sem.at[0]),
                  pltpu.make_async_copy(pv_hbm.at[b, :, keys], vbuf, sem.at[1])]
        for cp in copies:
            cp.start()
        for cp in copies:
            cp.wait()

    groups = []
    for pair in range(SB_WIDTH // LANES):
        cols = slice(pair * LANES, (pair + 1) * LANES)
        q2 = _stack_heads(q_ref[:, cols])
        blocks = [(k_ref[:, cols], v_ref[:, cols], diag_matrix, diag_mask, False)]
        for c in range(1, SB_WINDOW + 1):
            keys = slice((SB_WINDOW - c) * tk, (SB_WINDOW - c + 1) * tk)
            blocks.append((wk_ref[cols, keys].astype(bf16), wv_ref[cols, keys].astype(bf16),
                           sum_matrix, None, True))
        groups.append((q2, blocks, zero, zero))

    for pair, (carry, acc) in enumerate(_sb_groups(groups)):
        cols = slice(pair * LANES, (pair + 1) * LANES)
        q2 = groups[pair][0]

        def step(j, carry, acc, cols=cols, q2=q2):
            fetch(j)
            blk = (kbuf[cols, :].astype(bf16), vbuf[cols, :].astype(bf16), sum_matrix, None, True)
            return _sb_blocks(q2, [blk], carry, acc)

        acc = _sb_finish(n_past - SB_WINDOW - 1, carry, acc, step)
        o_ref[:, cols] = _unstack_heads(acc).astype(o_ref.dtype)


def _sb_sample(q, k, v, past_kt, past_vt, tk):
    b, t, _ = q.shape
    n_past = past_kt.shape[2] // tk
    assert n_past % SB_WINDOW == 0 and n_past >= SB_WINDOW
    qspec = pl.BlockSpec((None, t, SB_WIDTH), lambda bi: (bi, 0, 0))
    wspec = pl.BlockSpec((None, SB_WIDTH, SB_WINDOW * tk), lambda bi: (bi, 0, n_past // SB_WINDOW - 1))
    hbm = pl.BlockSpec(memory_space=pl.ANY)
    return pl.pallas_call(
        functools.partial(_sb_sample_kernel, tk, n_past),
        grid=(b,),
        in_specs=[qspec, qspec, qspec, wspec, wspec, hbm, hbm],
        out_specs=qspec,
        out_shape=jax.ShapeDtypeStruct((b, t, SB_WIDTH), bf16),
        scratch_shapes=[pltpu.VMEM((SB_WIDTH, tk), f32), pltpu.VMEM((SB_WIDTH, tk), f32),
                        pltpu.SemaphoreType.DMA((2,))],
        compiler_params=_params("arbitrary"),
        name="sb_sample",
    )(q, k, v, past_kt, past_vt, past_kt, past_vt)


def _hgrn_head_chunk(q, k, v, b, st):
    n_sub = HG_CHUNK // HG_SUB
    v_bf = v.astype(bf16)
    o_inter = _dot_nt((q * jnp.exp(b)).astype(bf16), st.astype(bf16))
    t_idx = lax.broadcasted_iota(jnp.int32, (HG_SUB, HG_SUB, HG_DK), 0)
    s_idx = lax.broadcasted_iota(jnp.int32, (HG_SUB, HG_SUB, HG_DK), 1)
    causal = s_idx <= t_idx
    outs = []
    for i in range(n_sub):
        lo = i * HG_SUB
        rows = slice(lo, lo + HG_SUB)
        qi, ki, bi = q[rows], k[rows], b[rows]
        diff = bi[:, None, :] - bi[None, :, :]
        dec = jnp.exp(jnp.where(causal, diff, -1e30))
        a_diag = jnp.sum(qi[:, None, :] * ki[None, :, :] * dec, axis=-1)
        o_i = _dot(a_diag.astype(bf16), v_bf[rows])
        if i > 0:
            ref = b[lo - 1:lo]
            q_t = qi * jnp.exp(bi - ref)
            k_t = k[:lo] * jnp.exp(ref - b[:lo])
            a_off = _dot_nt(q_t.astype(bf16), k_t.astype(bf16))
            o_i = o_i + _dot(a_off.astype(bf16), v_bf[:lo])
        outs.append(o_inter[rows] + o_i)
    o = jnp.concatenate(outs, axis=0)
    b_last = b[HG_CHUNK - 1:HG_CHUNK]
    k_dec = (k * jnp.exp(b_last - b)).astype(bf16)
    st_new = jnp.exp(b_last) * st + lax.dot_general(v_bf, k_dec, TN_DIMS, preferred_element_type=f32)
    return o, st_new


def _hgrn_chunk_matmul(qs, ks, vs, bs, sts):
    n_sub = HG_CHUNK // HG_SUB
    heads = range(len(qs))
    q_in, q_t, k_t, k_dec, v_bf, b_last = [], [], [], [], [], []
    for h in heads:
        q, k, b = qs[h], ks[h], bs[h]
        q_in.append((q * jnp.exp(b)).astype(bf16))
        q_t.append([])
        k_t.append([])
        for i in range(n_sub):
            lo, hi = i * HG_SUB, (i + 1) * HG_SUB
            if i == 0:
                q_t[h].append(q_in[h][:hi])
                k_t[h].append((k[:hi] * jnp.exp(-b[:hi])).astype(bf16))
            else:
                ref = b[lo - 1:lo]
                q_t[h].append((q[lo:hi] * jnp.exp(b[lo:hi] - ref)).astype(bf16))
                k_t[h].append((k[:hi] * jnp.exp(ref - b[:hi])).astype(bf16))
        b_last.append(b[HG_CHUNK - 1:HG_CHUNK])
        k_dec.append((k * jnp.exp(b_last[h] - b)).astype(bf16))
        v_bf.append(vs[h].astype(bf16))

    o_inter = [_dot_nt(q_in[h], sts[h].astype(bf16)) for h in heads]
    attn = [[_dot_nt(q_t[h][i], k_t[h][i]) for i in range(n_sub)] for h in heads]
    st_add = [lax.dot_general(v_bf[h], k_dec[h], TN_DIMS, preferred_element_type=f32) for h in heads]

    outs, new_sts = [], []
    for h in heads:
        pieces = []
        for i in range(n_sub):
            hi = (i + 1) * HG_SUB
            r = lax.broadcasted_iota(jnp.int32, (HG_SUB, hi), 0)
            c = lax.broadcasted_iota(jnp.int32, (HG_SUB, hi), 1)
            a = jnp.where(c - i * HG_SUB <= r, attn[h][i], 0.0)
            pieces.append(_dot(a.astype(bf16), v_bf[h][:hi]))
        outs.append(jnp.concatenate(pieces, axis=0) + o_inter[h])
        new_sts.append(jnp.exp(b_last[h]) * sts[h] + st_add[h])
    return outs, new_sts


def _hgrn_kernel(q_ref, lf_ref, k_ref, v_ref, gs_ref, gn_ref, s0_ref, o_ref, sout_ref, st_scr):
    tb = pl.program_id(1)
    n_tb = pl.num_programs(1)
    n_chunk = q_ref.shape[0] // HG_CHUNK
    n_sub = HG_CHUNK // HG_SUB

    @pl.when(tb == 0)
    def _():
        for h in range(HG_HEADS):
            st_scr[h] = s0_ref[h].T

    r = lax.broadcasted_iota(jnp.int32, (HG_CHUNK, HG_CHUNK), 0)
    c = lax.broadcasted_iota(jnp.int32, (HG_CHUNK, HG_CHUNK), 1)
    lower = jnp.where(c <= r, 1.0, 0.0).astype(bf16)
    head_cols = [slice(h * HG_DK, (h + 1) * HG_DK) for h in range(HG_HEADS)]

    def chunk(ci, carry):
        rows = pl.ds(pl.multiple_of(ci * HG_CHUNK, HG_CHUNK), HG_CHUNK)
        b_all = _dot_exact_rhs_lhs(lower, lf_ref[rows, :])

        def finish(h, o, st_new):
            cols = head_cols[h]
            st_scr[h] = st_new
            o = o * lax.rsqrt(jnp.mean(o * o, axis=-1, keepdims=True) + EPS)
            o_ref[rows, cols] = (o * gn_ref[:, cols] * gs_ref[rows, cols]).astype(o_ref.dtype)

        ends = [b_all[(i + 1) * HG_SUB - 1:(i + 1) * HG_SUB, :] for i in range(n_sub)]
        span = ends[0]
        for i in range(1, n_sub):
            span = jnp.minimum(span, ends[i] - ends[i - 1])
        bounded = jnp.min(span) >= HG_MATMUL_MIN_LOG

        @pl.when(bounded)
        def _():
            outs, new_sts = _hgrn_chunk_matmul(
                [q_ref[rows, cols] for cols in head_cols], [k_ref[rows, cols] for cols in head_cols],
                [v_ref[rows, cols] for cols in head_cols], [b_all[:, cols] for cols in head_cols],
                [st_scr[h] for h in range(HG_HEADS)])
            for h in range(HG_HEADS):
                finish(h, outs[h], new_sts[h])

        @pl.when(jnp.logical_not(bounded))
        def _():
            for h, cols in enumerate(head_cols):
                o, st_new = _hgrn_head_chunk(q_ref[rows, cols], k_ref[rows, cols], v_ref[rows, cols],
                                             b_all[:, cols], st_scr[h])
                finish(h, o, st_new)

        return carry

    lax.fori_loop(0, n_chunk, chunk, 0)

    @pl.when(tb == n_tb - 1)
    def _():
        for h in range(HG_HEADS):
            sout_ref[h] = st_scr[h].T


def _dot_exact_rhs_lhs(m, x):
    hi, mid, lo = _split3(x)
    return _dot(m, hi) + _dot(m, mid) + _dot(m, lo)


def _hgrn(qb, lf, kb, ib, gs, gnorm, s0, tt):
    b, t, _ = qb.shape
    blk = pl.BlockSpec((None, tt, HG_W), lambda i, j: (i, j, 0))
    sspec = pl.BlockSpec((None, HG_HEADS, HG_DK, HG_DV), lambda i, j: (i, 0, 0, 0))
    return pl.pallas_call(
        _hgrn_kernel,
        grid=(b, t // tt),
        in_specs=[blk, blk, blk, blk, blk, pl.BlockSpec((1, HG_W), lambda i, j: (0, 0)), sspec],
        out_specs=[blk, sspec],
        out_shape=[jax.ShapeDtypeStruct((b, t, HG_W), bf16),
                   jax.ShapeDtypeStruct((b, HG_HEADS, HG_DK, HG_DV), f32)],
        scratch_shapes=[pltpu.VMEM((HG_HEADS, HG_DV, HG_DK), f32)],
        compiler_params=_params("arbitrary", "arbitrary"),
        name="hgrn",
    )(qb, lf, kb, ib, gs, gnorm, s0)


def _ffn(hn_bf, wg_ref, wu_ref, wd_ref):
    g = _dot(hn_bf, wg_ref[...])
    u = _dot(hn_bf, wu_ref[...])
    return _dot((_silu(g) * u).astype(bf16), wd_ref[...])


def _resident(shape):
    return pl.BlockSpec(shape, lambda *_: (0,) * len(shape), pipeline_mode=pl.Buffered(1))


def _ffn_specs():
    return [_resident((D_MODEL, D_FF)), _resident((D_MODEL, D_FF)), _resident((D_FF, D_MODEL))]


def _even_post_kernel(x_ref, mod_ref, oa_ref, ob_ref, wo_ref, gf_ref, wg_ref, wu_ref, wd_ref, out_ref):
    nb, tt, d = x_ref.shape
    m = nb * tt
    o = (_dot(oa_ref[...].reshape(m, SB_WIDTH), wo_ref[:SB_WIDTH, :])
         + _dot(ob_ref[...].reshape(m, HG_W), wo_ref[SB_WIDTH:, :]))
    x1 = x_ref[...] + (1.0 + mod_ref[:, 2:3, :]) * o.reshape(nb, tt, d)
    hn = _norm_mod(x1, gf_ref[...], mod_ref[:, 4:5, :], mod_ref[:, 3:4, :])
    y = _ffn(hn.reshape(m, d).astype(bf16), wg_ref, wu_ref, wd_ref)
    out_ref[...] = x1 + (1.0 + mod_ref[:, 5:6, :]) * y.reshape(nb, tt, d)


def _even_post(x, mod, oa, ob, wo, gf, wg, wu, wd, nb, tt):
    b, t, d = x.shape
    blk = lambda width: pl.BlockSpec((nb, tt, width), lambda i, tb: (i, tb, 0))
    return pl.pallas_call(
        _even_post_kernel,
        grid=(b // nb, t // tt),
        in_specs=[blk(d), pl.BlockSpec((nb, N_MOD, d), lambda i, tb: (i, 0, 0)),
                  blk(SB_WIDTH), blk(HG_W), _resident((d, d)), _resident((1, d))] + _ffn_specs(),
        out_specs=blk(d),
        out_shape=jax.ShapeDtypeStruct((b, t, d), f32),
        compiler_params=_params("arbitrary", "arbitrary"),
        name="even_post_ffn",
    )(x, mod, oa, ob, wo, gf, wg, wu, wd)


def _linear_scan(a, u, h_init, tt):
    m, d = a.shape
    n_group = m // SUBLANES
    a3 = a.reshape(n_group, SUBLANES, d)
    u3 = u.reshape(n_group, SUBLANES, d)
    sub = lax.broadcasted_iota(jnp.int32, (1, SUBLANES, d), 1)
    step = 1
    while step < SUBLANES:
        keep = sub >= step
        a_prev = jnp.where(keep, pltpu.roll(a3, step, 1), 1.0)
        u_prev = jnp.where(keep, pltpu.roll(u3, step, 1), 0.0)
        u3 = a3 * u_prev + u3
        a3 = a3 * a_prev
        step *= 2
    groups_per_seq = tt // SUBLANES
    carry = None
    groups = []
    for gi in range(n_group):
        if gi % groups_per_seq == 0:
            carry = h_init[gi // groups_per_seq]
        h_group = u3[gi] + a3[gi] * carry
        groups.append(h_group)
        carry = h_group[SUBLANES - 1:SUBLANES, :]
    return jnp.concatenate(groups, axis=0)


def _odd_kernel(stream_start, x_ref, mod_ref, gm_ref, win_ref, cw_ref, cb_ref, wax_ref, ba_ref, bx_ref,
                lam_ref, wo_ref, conv0_ref, h0_ref, gf_ref, fn_ref, wg_ref, wu_ref, wd_ref,
                y_ref, conv_ref, hout_ref, xs_scr, hprev_scr):
    nb, tt, d = x_ref.shape
    m = nb * tt
    tb = pl.program_id(1)

    @pl.when(tb == 0)
    def _():
        xs_scr[:, HALO - (CONV_W - 1):HALO, :] = conv0_ref[...]
        hprev_scr[...] = h0_ref[...]

    x = x_ref[...]
    hn = _norm_mod(x, gm_ref[...], mod_ref[:, 1:2, :], mod_ref[:, 0:1, :])
    hn = hn.reshape(m, d).astype(bf16)
    gate = _dot(hn, win_ref[:, :d])
    xbr = _dot(hn, win_ref[:, d:]).reshape(nb, tt, d)
    xs_scr[:, HALO:HALO + tt, :] = xbr
    xc = cb_ref[...] + xbr * cw_ref[CONV_W - 1:CONV_W, :]
    for k in range(1, CONV_W):
        xc = xc + xs_scr[:, HALO - k:HALO - k + tt, :] * cw_ref[CONV_W - 1 - k:CONV_W - k, :]
    tail = xs_scr[:, HALO + tt - (CONV_W - 1):HALO + tt, :]
    conv_ref[...] = tail
    xs_scr[:, HALO - (CONV_W - 1):HALO, :] = tail

    xc = xc.reshape(m, d)
    xc_bf = xc.astype(bf16)
    ra, rx = [], []
    for blk in range(LRU_BLOCKS):
        both = _dot(xc_bf[:, blk * LRU_BLK:(blk + 1) * LRU_BLK], wax_ref[blk])
        ra.append(both[:, :LRU_BLK])
        rx.append(both[:, LRU_BLK:])
    r = _sigmoid(jnp.concatenate(ra, axis=1) + ba_ref[...])
    gi = _sigmoid(jnp.concatenate(rx, axis=1) + bx_ref[...])
    lam = lam_ref[...]
    log_sig_lam = -(jnp.maximum(-lam, 0.0) + jnp.log1p(jnp.exp(-jnp.abs(lam))))
    log_a = C_SCALE * r * log_sig_lam
    a = jnp.exp(log_a)
    mult = jnp.sqrt(-jnp.tanh(log_a) * (1.0 + a * a))
    if stream_start:
        row = (lax.broadcasted_iota(jnp.int32, (m, d), 0) & (tt - 1)) + tb * tt
        mult = jnp.where(row == 0, 1.0, mult)
    h = _linear_scan(a, mult * gi * xc, hprev_scr[...], tt)
    h_last = h.reshape(nb, tt, d)[:, tt - 1:tt, :]
    hprev_scr[...] = h_last
    hout_ref[...] = h_last

    y_in = (jax.nn.gelu(gate) * h).astype(bf16)
    o = _dot(y_in, wo_ref[...])
    x1 = x + (1.0 + mod_ref[:, 2:3, :]) * o.reshape(nb, tt, d)
    hn2 = _norm_mod(x1, gf_ref[...], mod_ref[:, 4:5, :], mod_ref[:, 3:4, :])
    y = _ffn(hn2.reshape(m, d).astype(bf16), wg_ref, wu_ref, wd_ref)
    x2 = x1 + (1.0 + mod_ref[:, 5:6, :]) * y.reshape(nb, tt, d)
    ms = jnp.mean(x2 * x2, axis=-1, keepdims=True)
    y_ref[...] = x2 * lax.rsqrt(ms + EPS) * fn_ref[...]


def _odd_layer(x, mod, gm, win, cw, cb, wax, ba, bx, lam, wo, conv0, h0, gf, fn, wg, wu, wd,
               nb, tt, stream_start):
    b, t, d = x.shape
    blk = pl.BlockSpec((nb, tt, d), lambda i, tb: (i, tb, 0))
    per_seq = lambda rows: pl.BlockSpec((nb, rows, d), lambda i, tb: (i, 0, 0))
    row = _resident((1, d))
    return pl.pallas_call(
        functools.partial(_odd_kernel, stream_start),
        grid=(b // nb, t // tt),
        in_specs=[blk, per_seq(N_MOD), row, _resident((d, 2 * d)), _resident((CONV_W, d)),
                  row, _resident((LRU_BLOCKS, LRU_BLK, 2 * LRU_BLK)), row,
                  row, row, _resident((d, d)), per_seq(CONV_W - 1), per_seq(1),
                  row, row] + _ffn_specs(),
        out_specs=[blk, per_seq(CONV_W - 1), per_seq(1)],
        out_shape=[jax.ShapeDtypeStruct((b, t, d), f32),
                   jax.ShapeDtypeStruct((b, CONV_W - 1, d), f32),
                   jax.ShapeDtypeStruct((b, 1, d), f32)],
        scratch_shapes=[pltpu.VMEM((nb, HALO + tt, d), f32), pltpu.VMEM((nb, 1, d), f32)],
        compiler_params=_params("arbitrary", "arbitrary"),
        name="odd_layer_ffn",
    )(x, mod, gm, win, cw, cb, wax, ba, bx, lam, wo, conv0, h0, gf, fn, wg, wu, wd)


def _trunk(x, mods, past, weights, nb, tt, hg_tt):
    b, t, d = x.shape
    sample = past is not None
    row = lambda v: v.reshape(1, -1)

    qa, ka, va, ka_bf, va_bf, qb, lf, kb, ib, gs = _even_pre(
        x, mods[0], row(weights["norm_mix"][0]), weights["w_in_even"], weights["w_kv_t"],
        weights["hg_lb_logits"], 0, nb, tt, not sample)
    if sample:
        past_k, past_v, s0, conv0, h0 = past
        oa = _sb_sample(qa, ka_bf, va_bf, past_k, past_v, LANES)
        heads = lambda a: a.reshape(1, b, t, SB_HEADS, SB_DIM)
    else:
        oa = _sb_prompt(qa, ka_bf, va_bf)
        heads = lambda a: a.reshape(b, SB_HEADS, SB_DIM, t).transpose(0, 3, 1, 2)[None]
        s0 = jnp.zeros((b, HG_HEADS, HG_DK, HG_DV), f32)
        conv0 = jnp.zeros((b, CONV_W - 1, d), f32)
        h0 = jnp.zeros((b, 1, d), f32)
    ob, s_new = _hgrn(qb, lf, kb, ib, gs, row(weights["hg_gnorm"]), s0, hg_tt)
    x = _even_post(x, mods[0], oa, ob, weights["w_out_even"], row(weights["norm_ffn"][0]),
                   weights["ffn_wg"][0], weights["ffn_wu"][0], weights["ffn_wd"][0], nb, tt)

    y, conv_new, h_new = _odd_layer(
        x, mods[1], row(weights["norm_mix"][1]), weights["w_in_odd"], weights["conv_w"],
        row(weights["conv_b"]), weights["lru_wax"], row(weights["lru_ba"]), row(weights["lru_bx"]),
        row(weights["lru_lambda"]), weights["w_out_odd"], conv0, h0,
        row(weights["norm_ffn"][1]), row(weights["final_norm"]),
        weights["ffn_wg"][1], weights["ffn_wu"][1], weights["ffn_wd"][1],
        nb, tt, not sample)

    return (y, heads(ka), heads(va), s_new[None], conv_new[None], h_new.reshape(1, b, d))


def kernel(x_prompt, x_sample, cache_sb_k, cache_sb_v, state_hgrn, state_conv, state_lru, c_prompt, c_sample, norm_mix, norm_ffn, w_ada, b_ada, w_in_even, w_out_even, hg_gnorm, hg_lb_logits, w_in_odd, conv_w, conv_b, lru_wa, lru_ba, lru_wx, lru_bx, lru_lambda, w_out_odd, ffn_wg, ffn_wu, ffn_wd, final_norm):
    bp = x_prompt.shape[0]
    bs, ts, d = x_sample.shape
    weights = {
        "norm_mix": norm_mix, "norm_ffn": norm_ffn,
        "w_in_even": w_in_even[0].astype(bf16), "w_out_even": w_out_even[0].astype(bf16),
        "w_kv_t": jnp.stack([w_in_even[0][:, SEG:2 * SEG].T, w_in_even[0][:, 2 * SEG:3 * SEG].T]).astype(bf16),
        "hg_gnorm": hg_gnorm[0], "hg_lb_logits": hg_lb_logits,
        "w_in_odd": w_in_odd[0].astype(bf16), "conv_w": conv_w[0], "conv_b": conv_b[0],
        "lru_wax": jnp.concatenate([lru_wa[0], lru_wx[0]], axis=-1).astype(bf16),
        "lru_ba": lru_ba[0], "lru_bx": lru_bx[0], "lru_lambda": lru_lambda[0],
        "w_out_odd": w_out_odd[0].astype(bf16),
        "ffn_wg": ffn_wg.astype(bf16), "ffn_wu": ffn_wu.astype(bf16), "ffn_wd": ffn_wd.astype(bf16),
        "final_norm": final_norm,
    }
    mods = _ada(jnp.concatenate([c_prompt, c_sample], axis=0), w_ada, b_ada)
    mods = mods.reshape(DEPTH, bp + bs, N_MOD, d)

    out_p = _trunk(x_prompt, mods[:, :bp], None, weights, nb=1, tt=512, hg_tt=512)
    cache_t = lambda c: c[0].transpose(0, 2, 3, 1).reshape(bs, SB_WIDTH, -1)
    past = (cache_t(cache_sb_k), cache_t(cache_sb_v),
            state_hgrn[0], state_conv[0], state_lru[0].reshape(bs, 1, d))
    out_s = _trunk(x_sample, mods[:, bp:], past, weights, nb=bs, tt=ts, hg_tt=ts)

    y_p, k_p, v_p, s_p, conv_p, h_p = out_p
    y_s, k_s, v_s, s_s, conv_s, h_s = out_s
    return (y_p, y_s, k_p, v_p, s_p, conv_p, h_p, k_s, v_s, s_s, conv_s, h_s)
```

```python
import functools
import math

import jax
import jax.numpy as jnp
from jax import lax
from jax.experimental import pallas as pl
from jax.experimental.pallas import tpu as pltpu

f32 = jnp.float32
bf16 = jnp.bfloat16

D_MODEL = 1024
DEPTH = 2
EPS = 1e-6
N_MOD = 6

SB_HEADS = 8
SB_DIM = 64
SB_WIDTH = SB_HEADS * SB_DIM
HG_HEADS = 4
HG_DK = 128
HG_DV = 128
HG_W = HG_HEADS * HG_DK
SEG = 512
N_SEG = 7
HG_CHUNK = 64
HG_SUB = 16
HG_MATMUL_MIN_LOG = -40.0
LRU_BLOCKS = 8
LRU_BLK = D_MODEL // LRU_BLOCKS
CONV_W = 4
C_SCALE = 8.0
D_FF = 2816

SB_WINDOW = 2
SB_QBLOCKS = 2
SB_SKIP_LOG = -105.0

MXU_COLS = 256

LANES = 128
SUBLANES = 8
HALO = 8
VMEM_LIMIT = 60 * 1024 * 1024

NT_DIMS = (((1,), (1,)), ((), ()))
TN_DIMS = (((0,), (0,)), ((), ()))


def _dot(a, b):
    return jnp.dot(a, b, preferred_element_type=f32)


def _dot_nt(a, b):
    return lax.dot_general(a, b, NT_DIMS, preferred_element_type=f32)


def _split3(x):
    hi = x.astype(bf16)
    r1 = x - hi.astype(f32)
    mid = r1.astype(bf16)
    lo = (r1 - mid.astype(f32)).astype(bf16)
    return hi, mid, lo


def _dot_exact_rhs(x, m):
    hi, mid, lo = _split3(x)
    return _dot(hi, m) + _dot(mid, m) + _dot(lo, m)


def _sigmoid(x):
    return jax.nn.sigmoid(x)


def _silu(x):
    return x * _sigmoid(x)


def _norm_mod(x, g, scale, shift):
    ms = jnp.mean(x * x, axis=-1, keepdims=True)
    return x * lax.rsqrt(ms + EPS) * (g * (1.0 + scale)) + shift


def _params(*sem):
    return pltpu.CompilerParams(dimension_semantics=sem, vmem_limit_bytes=VMEM_LIMIT)


def _ada_kernel(c_ref, w_ref, b_ref, o_ref):
    c = c_ref[...]
    sc = _silu(c)
    w = w_ref[...]
    s_hi = sc.astype(bf16)
    s_lo = (sc - s_hi.astype(f32)).astype(bf16)
    w_hi = w.astype(bf16)
    w_lo = (w - w_hi.astype(f32)).astype(bf16)
    o_ref[...] = _dot(s_hi, w_hi) + _dot(s_lo, w_hi) + _dot(s_hi, w_lo) + b_ref[...]


def _ada(c_all, w_ada, b_ada):
    rows = c_all.shape[0]
    n_out = N_MOD * D_MODEL
    tn = 1536
    return pl.pallas_call(
        _ada_kernel,
        grid=(DEPTH, n_out // tn),
        in_specs=[
            pl.BlockSpec((rows, D_MODEL), lambda l, j: (0, 0)),
            pl.BlockSpec((None, D_MODEL, tn), lambda l, j: (l, 0, j)),
            pl.BlockSpec((None, 1, tn), lambda l, j: (l, 0, j)),
        ],
        out_specs=pl.BlockSpec((None, rows, tn), lambda l, j: (l, 0, j)),
        out_shape=jax.ShapeDtypeStruct((DEPTH, rows, n_out), f32),
        compiler_params=_params("arbitrary", "arbitrary"),
        name="ada_mod",
    )(c_all, w_ada, b_ada.reshape(DEPTH, 1, n_out))


def _even_pre_kernel(layer, kv_transposed, x_ref, mod_ref, g_ref, w_ref, wkvt_ref, lbl_ref,
                     qa_ref, ka_ref, va_ref, kab_ref, vab_ref,
                     qb_ref, lf_ref, kb_ref, ib_ref, gs_ref):
    nb, tt, d = x_ref.shape
    m = nb * tt
    hn = _norm_mod(x_ref[...], g_ref[...], mod_ref[:, 1:2, :], mod_ref[:, 0:1, :])
    hn = hn.reshape(m, d).astype(bf16)

    def seg(i):
        return _dot(hn, w_ref[:, i * SEG:(i + 1) * SEG]).reshape(nb, tt, SEG)

    qa_ref[...] = (seg(0) * (1.0 / math.sqrt(SB_DIM))).astype(bf16)
    if kv_transposed:
        for i, (full_ref, blocks_ref) in enumerate(((ka_ref, kab_ref), (va_ref, vab_ref))):
            kt = _dot_nt(wkvt_ref[i], hn)
            full_ref[...] = kt
            for c in range(m // LANES):
                blocks_ref[c] = kt[:, c * LANES:(c + 1) * LANES].astype(bf16)
    else:
        ka = seg(1)
        ka_ref[...] = ka
        kab_ref[...] = ka.astype(bf16)
        va = seg(2)
        va_ref[...] = va
        vab_ref[...] = va.astype(bf16)
    qb_ref[...] = _silu(seg(3)) * (HG_DK ** -0.5)

    lg = lbl_ref[...]
    e = jnp.exp(lg - jnp.max(lg, axis=0, keepdims=True))
    lb = jnp.sum(e[:layer + 1], axis=0, keepdims=True) / jnp.sum(e, axis=0, keepdims=True)
    fb = seg(4)
    f = lb + (1.0 - lb) * _sigmoid(fb)
    lf_ref[...] = jnp.log(f)
    kb_ref[...] = (1.0 - lb) * _sigmoid(-fb)
    ib_ref[...] = seg(5)
    gs_ref[...] = _silu(seg(6))


def _even_pre(x, mod, g, w_bf, wkvt_bf, lb_logits, layer, nb, tt, kv_transposed):
    b, t, d = x.shape
    grid = (b // nb, t // tt)
    blk = lambda width: pl.BlockSpec((nb, tt, width), lambda i, j: (i, j, 0))
    out_f = jax.ShapeDtypeStruct((b, t, SEG), f32)
    out_b = jax.ShapeDtypeStruct((b, t, SEG), bf16)
    if kv_transposed:
        assert nb == 1 and tt % LANES == 0
        kv_f, kv_b = jax.ShapeDtypeStruct((b, SEG, t), f32), jax.ShapeDtypeStruct((b, t // LANES, SEG, LANES), bf16)
        kv_f_spec = pl.BlockSpec((None, SEG, tt), lambda i, j: (i, 0, j))
        kv_b_spec = pl.BlockSpec((None, tt // LANES, SEG, LANES), lambda i, j: (i, j, 0, 0))
    else:
        kv_f, kv_b, kv_f_spec, kv_b_spec = out_f, out_b, blk(SEG), blk(SEG)
    return pl.pallas_call(
        functools.partial(_even_pre_kernel, layer, kv_transposed),
        grid=grid,
        in_specs=[
            blk(d),
            pl.BlockSpec((nb, N_MOD, d), lambda i, j: (i, 0, 0)),
            _resident((1, d)),
            _resident((d, N_SEG * SEG)),
            _resident((2, SEG, d)),
            _resident((DEPTH + 1, HG_W)),
        ],
        out_specs=[blk(SEG), kv_f_spec, kv_f_spec, kv_b_spec, kv_b_spec] + [blk(SEG)] * 5,
        out_shape=[out_b, kv_f, kv_f, kv_b, kv_b, out_f, out_f, out_f, out_f, out_f],
        compiler_params=_params("arbitrary", "arbitrary"),
        name="even_pre",
    )(x, mod, g, w_bf, wkvt_bf, lb_logits)


def _sb_consts(tk, scale=1.0):
    r = lax.broadcasted_iota(jnp.int32, (tk, LANES + tk), 0)
    c = lax.broadcasted_iota(jnp.int32, (tk, LANES + tk), 1)
    m = jnp.where((c < LANES) | (r > c - LANES), scale, 0.0).astype(bf16)
    return jnp.concatenate([m, m, m], axis=0)


def _sb_groups(groups):
    zs = [[_dot(q2, k_blk) if transposed else _dot_nt(q2, k_blk)
           for k_blk, _, _, _, transposed in blocks] for q2, blocks, _, _ in groups]
    l1mbs, sums = [], []
    for g, (_, blocks, _, _) in enumerate(groups):
        l1mbs.append([])
        sums.append([])
        for z, (_, _, sum_matrix, mask, _) in zip(zs[g], blocks):
            l1mb = -(jnp.maximum(z, 0.0) + jnp.log(1.0 + jnp.exp(-jnp.abs(z))))
            if mask is not None:
                l1mb = jnp.where(mask, l1mb, 0.0)
            l1mbs[g].append(l1mb)
            sums[g].append(_dot(jnp.concatenate(_split3(l1mb), axis=1), sum_matrix))
    results = []
    for g, (_, blocks, carry, acc) in enumerate(groups):
        for z, l1mb, s, (_, v_blk, _, mask, transposed) in zip(zs[g], l1mbs[g], sums[g], blocks):
            tk = z.shape[1]
            after = carry[:, :tk] + s[:, LANES:]
            w = jnp.exp(z + l1mb + after)
            if mask is not None:
                w = jnp.where(mask, w, 0.0)
            w = w.astype(bf16)
            acc = acc + (_dot_nt(w, v_blk) if transposed else _dot(w, v_blk))
            carry = carry + s[:, :LANES]
        results.append((carry, acc))
    return results


def _sb_blocks(q2, blocks, carry, acc):
    return _sb_groups([(q2, blocks, carry, acc)])[0]


def _sb_finish(j0, carry, acc, step):
    def cond(state):
        j, carry, _ = state
        return (j >= 0) & (jnp.max(carry) > SB_SKIP_LOG)

    def body(state):
        j, carry, acc = state
        carry, acc = step(j, carry, acc)
        return j - 1, carry, acc

    return lax.while_loop(cond, body, (j0, carry, acc))[2]


def _stack_heads(q):
    lane = lax.broadcasted_iota(jnp.int32, q.shape, 1)
    zero = jnp.zeros_like(q)
    return jnp.concatenate([jnp.where(lane < SB_DIM, q, zero), jnp.where(lane >= SB_DIM, q, zero)], axis=0)


def _unstack_heads(acc2):
    tq = acc2.shape[0] // 2
    lane = lax.broadcasted_iota(jnp.int32, (tq, LANES), 1)
    return jnp.where(lane < SB_DIM, acc2[:tq], acc2[tq:])


def _causal_mask2(tq):
    r = lax.broadcasted_iota(jnp.int32, (2 * tq, tq), 0) & (tq - 1)
    c = lax.broadcasted_iota(jnp.int32, (2 * tq, tq), 1)
    return c < r


def _sb_prompt_kernel(q_ref, k_ref, v_ref, o_ref):
    tq = k_ref.shape[2]
    n_q = q_ref.shape[0] // tq
    i0 = pl.program_id(2) * n_q
    sum_matrix = _sb_consts(tq)
    diag_mask = _causal_mask2(tq)
    zero = jnp.zeros((2 * tq, LANES), f32)

    groups = []
    for g in range(n_q):
        i = i0 + g
        q2 = _stack_heads(q_ref[g * tq:(g + 1) * tq, :])
        blocks = [(k_ref[i], v_ref[i], sum_matrix, diag_mask, True)]
        for c in range(1, SB_WINDOW + 1):
            scale = jnp.where(i >= c, 1.0, 0.0)
            j = jnp.maximum(i - c, 0)
            blocks.append((k_ref[j], (v_ref[j].astype(f32) * scale).astype(bf16), _sb_consts(tq, scale),
                           None, True))
        groups.append((q2, blocks, zero, zero))

    for g, (carry, acc) in enumerate(_sb_groups(groups)):
        q2 = groups[g][0]

        def step(j, carry, acc, q2=q2):
            return _sb_blocks(q2, [(k_ref[j], v_ref[j], sum_matrix, None, True)], carry, acc)

        acc = _sb_finish(i0 + g - SB_WINDOW - 1, carry, acc, step)
        o_ref[g * tq:(g + 1) * tq, :] = _unstack_heads(acc).astype(o_ref.dtype)


def _sb_prompt(q, k_blocks, v_blocks):
    b, t, _ = q.shape
    tk = LANES
    tq = SB_QBLOCKS * tk
    n_pair = SB_WIDTH // LANES
    qspec = pl.BlockSpec((None, tq, LANES), lambda bi, p, i: (bi, i, p))
    kvspec = pl.BlockSpec((None, t // tk, LANES, tk), lambda bi, p, i: (bi, 0, p, 0))
    return pl.pallas_call(
        _sb_prompt_kernel,
        grid=(b, n_pair, t // tq),
        in_specs=[qspec, kvspec, kvspec],
        out_specs=qspec,
        out_shape=jax.ShapeDtypeStruct((b, t, SB_WIDTH), bf16),
        compiler_params=_params("arbitrary", "arbitrary", "arbitrary"),
        name="sb_prompt",
    )(q, k_blocks, v_blocks)


def _sb_sample_kernel(tk, n_past, q_ref, k_ref, v_ref, wk_ref, wv_ref, pk_hbm, pv_hbm, o_ref,
                      kbuf, vbuf, sem):
    tq = q_ref.shape[0]
    b = pl.program_id(0)
    zero = jnp.zeros((2 * tq, LANES), f32)
    sum_matrix = _sb_consts(tk)
    diag_matrix = _sb_consts(tq)
    diag_mask = _causal_mask2(tq)

    def fetch(j):
        keys = pl.ds(pl.multiple_of(j * tk, tk), tk)
        copies = [pltpu.make_async_copy(pk_hbm.at[b, :, keys], kbuf, sem.at[0]),
                  pltpu.make_async_copy(pv_hbm.at[b, :, keys], vbuf, sem.at[1])]
        for cp in copies:
            cp.start()
        for cp in copies:
            cp.wait()

    groups = []
    for pair in range(SB_WIDTH // LANES):
        cols = slice(pair * LANES, (pair + 1) * LANES)
        q2 = _stack_heads(q_ref[:, cols])
        blocks = [(k_ref[:, cols], v_ref[:, cols], diag_matrix, diag_mask, False)]
        for c in range(1, SB_WINDOW + 1):
            keys = slice((SB_WINDOW - c) * tk, (SB_WINDOW - c + 1) * tk)
            blocks.append((wk_ref[cols, keys].astype(bf16), wv_ref[cols, keys].astype(bf16),
                           sum_matrix, None, True))
        groups.append((q2, blocks, zero, zero))

    for pair, (carry, acc) in enumerate(_sb_groups(groups)):
        cols = slice(pair * LANES, (pair + 1) * LANES)
        q2 = groups[pair][0]

        def step(j, carry, acc, cols=cols, q2=q2):
            fetch(j)
            blk = (kbuf[cols, :].astype(bf16), vbuf[cols, :].astype(bf16), sum_matrix, None, True)
            return _sb_blocks(q2, [blk], carry, acc)

        acc = _sb_finish(n_past - SB_WINDOW - 1, carry, acc, step)
        o_ref[:, cols] = _unstack_heads(acc).astype(o_ref.dtype)


def _sb_sample(q, k, v, past_kt, past_vt, tk):
    b, t, _ = q.shape
    n_past = past_kt.shape[2] // tk
    assert n_past % SB_WINDOW == 0 and n_past >= SB_WINDOW
    qspec = pl.BlockSpec((None, t, SB_WIDTH), lambda bi: (bi, 0, 0))
    wspec = pl.BlockSpec((None, SB_WIDTH, SB_WINDOW * tk), lambda bi: (bi, 0, n_past // SB_WINDOW - 1))
    hbm = pl.BlockSpec(memory_space=pl.ANY)
    return pl.pallas_call(
        functools.partial(_sb_sample_kernel, tk, n_past),
        grid=(b,),
        in_specs=[qspec, qspec, qspec, wspec, wspec, hbm, hbm],
        out_specs=qspec,
        out_shape=jax.ShapeDtypeStruct((b, t, SB_WIDTH), bf16),
        scratch_shapes=[pltpu.VMEM((SB_WIDTH, tk), f32), pltpu.VMEM((SB_WIDTH, tk), f32),
                        pltpu.SemaphoreType.DMA((2,))],
        compiler_params=_params("arbitrary"),
        name="sb_sample",
    )(q, k, v, past_kt, past_vt, past_kt, past_vt)


def _hgrn_head_chunk(q, k, v, b, st):
    n_sub = HG_CHUNK // HG_SUB
    v_bf = v.astype(bf16)
    o_inter = _dot_nt((q * jnp.exp(b)).astype(bf16), st.astype(bf16))
    t_idx = lax.broadcasted_iota(jnp.int32, (HG_SUB, HG_SUB, HG_DK), 0)
    s_idx = lax.broadcasted_iota(jnp.int32, (HG_SUB, HG_SUB, HG_DK), 1)
    causal = s_idx <= t_idx
    outs = []
    for i in range(n_sub):
        lo = i * HG_SUB
        rows = slice(lo, lo + HG_SUB)
        qi, ki, bi = q[rows], k[rows], b[rows]
        diff = bi[:, None, :] - bi[None, :, :]
        dec = jnp.exp(jnp.where(causal, diff, -1e30))
        a_diag = jnp.sum(qi[:, None, :] * ki[None, :, :] * dec, axis=-1)
        o_i = _dot(a_diag.astype(bf16), v_bf[rows])
        if i > 0:
            ref = b[lo - 1:lo]
            q_t = qi * jnp.exp(bi - ref)
            k_t = k[:lo] * jnp.exp(ref - b[:lo])
            a_off = _dot_nt(q_t.astype(bf16), k_t.astype(bf16))
            o_i = o_i + _dot(a_off.astype(bf16), v_bf[:lo])
        outs.append(o_inter[rows] + o_i)
    o = jnp.concatenate(outs, axis=0)
    b_last = b[HG_CHUNK - 1:HG_CHUNK]
    k_dec = (k * jnp.exp(b_last - b)).astype(bf16)
    st_new = jnp.exp(b_last) * st + lax.dot_general(v_bf, k_dec, TN_DIMS, preferred_element_type=f32)
    return o, st_new


def _hgrn_chunk_matmul(qs, ks, vs, bs, sts):
    n_sub = HG_CHUNK // HG_SUB
    heads = range(len(qs))
    q_in, q_t, k_t, k_dec, v_bf, b_last = [], [], [], [], [], []
    for h in heads:
        q, k, b = qs[h], ks[h], bs[h]
        q_in.append((q * jnp.exp(b)).astype(bf16))
        q_t.append([])
        k_t.append([])
        for i in range(n_sub):
            lo, hi = i * HG_SUB, (i + 1) * HG_SUB
            if i == 0:
                q_t[h].append(q_in[h][:hi])
                k_t[h].append((k[:hi] * jnp.exp(-b[:hi])).astype(bf16))
            else:
                ref = b[lo - 1:lo]
                q_t[h].append((q[lo:hi] * jnp.exp(b[lo:hi] - ref)).astype(bf16))
                k_t[h].append((k[:hi] * jnp.exp(ref - b[:hi])).astype(bf16))
        b_last.append(b[HG_CHUNK - 1:HG_CHUNK])
        k_dec.append((k * jnp.exp(b_last[h] - b)).astype(bf16))
        v_bf.append(vs[h].astype(bf16))

    o_inter = [_dot_nt(q_in[h], sts[h].astype(bf16)) for h in heads]
    attn = [[_dot_nt(q_t[h][i], k_t[h][i]) for i in range(n_sub)] for h in heads]
    st_add = [lax.dot_general(v_bf[h], k_dec[h], TN_DIMS, preferred_element_type=f32) for h in heads]

    outs, new_sts = [], []
    for h in heads:
        pieces = []
        for i in range(n_sub):
            hi = (i + 1) * HG_SUB
            r = lax.broadcasted_iota(jnp.int32, (HG_SUB, hi), 0)
            c = lax.broadcasted_iota(jnp.int32, (HG_SUB, hi), 1)
            a = jnp.where(c - i * HG_SUB <= r, attn[h][i], 0.0)
            pieces.append(_dot(a.astype(bf16), v_bf[h][:hi]))
        outs.append(jnp.concatenate(pieces, axis=0) + o_inter[h])
        new_sts.append(jnp.exp(b_last[h]) * sts[h] + st_add[h])
    return outs, new_sts


def _hgrn_kernel(q_ref, lf_ref, k_ref, v_ref, gs_ref, gn_ref, s0_ref, o_ref, sout_ref, st_scr):
    tb = pl.program_id(1)
    n_tb = pl.num_programs(1)
    n_chunk = q_ref.shape[0] // HG_CHUNK
    n_sub = HG_CHUNK // HG_SUB

    @pl.when(tb == 0)
    def _():
        for h in range(HG_HEADS):
            st_scr[h] = s0_ref[h].T

    r = lax.broadcasted_iota(jnp.int32, (HG_CHUNK, HG_CHUNK), 0)
    c = lax.broadcasted_iota(jnp.int32, (HG_CHUNK, HG_CHUNK), 1)
    lower = jnp.where(c <= r, 1.0, 0.0).astype(bf16)
    head_cols = [slice(h * HG_DK, (h + 1) * HG_DK) for h in range(HG_HEADS)]

    def chunk(ci, carry):
        rows = pl.ds(pl.multiple_of(ci * HG_CHUNK, HG_CHUNK), HG_CHUNK)
        b_all = _dot_exact_rhs_lhs(lower, lf_ref[rows, :])

        def finish(h, o, st_new):
            cols = head_cols[h]
            st_scr[h] = st_new
            o = o * lax.rsqrt(jnp.mean(o * o, axis=-1, keepdims=True) + EPS)
            o_ref[rows, cols] = (o * gn_ref[:, cols] * gs_ref[rows, cols]).astype(o_ref.dtype)

        ends = [b_all[(i + 1) * HG_SUB - 1:(i + 1) * HG_SUB, :] for i in range(n_sub)]
        span = ends[0]
        for i in range(1, n_sub):
            span = jnp.minimum(span, ends[i] - ends[i - 1])
        bounded = jnp.min(span) >= HG_MATMUL_MIN_LOG

        @pl.when(bounded)
        def _():
            outs, new_sts = _hgrn_chunk_matmul(
                [q_ref[rows, cols] for cols in head_cols], [k_ref[rows, cols] for cols in head_cols],
                [v_ref[rows, cols] for cols in head_cols], [b_all[:, cols] for cols in head_cols],
                [st_scr[h] for h in range(HG_HEADS)])
            for h in range(HG_HEADS):
                finish(h, outs[h], new_sts[h])

        @pl.when(jnp.logical_not(bounded))
        def _():
            for h, cols in enumerate(head_cols):
                o, st_new = _hgrn_head_chunk(q_ref[rows, cols], k_ref[rows, cols], v_ref[rows, cols],
                                             b_all[:, cols], st_scr[h])
                finish(h, o, st_new)

        return carry

    lax.fori_loop(0, n_chunk, chunk, 0)

    @pl.when(tb == n_tb - 1)
    def _():
        for h in range(HG_HEADS):
            sout_ref[h] = st_scr[h].T


def _dot_exact_rhs_lhs(m, x):
    hi, mid, lo = _split3(x)
    return _dot(m, hi) + _dot(m, mid) + _dot(m, lo)


def _hgrn(qb, lf, kb, ib, gs, gnorm, s0, tt):
    b, t, _ = qb.shape
    blk = pl.BlockSpec((None, tt, HG_W), lambda i, j: (i, j, 0))
    sspec = pl.BlockSpec((None, HG_HEADS, HG_DK, HG_DV), lambda i, j: (i, 0, 0, 0))
    return pl.pallas_call(
        _hgrn_kernel,
        grid=(b, t // tt),
        in_specs=[blk, blk, blk, blk, blk, pl.BlockSpec((1, HG_W), lambda i, j: (0, 0)), sspec],
        out_specs=[blk, sspec],
        out_shape=[jax.ShapeDtypeStruct((b, t, HG_W), bf16),
                   jax.ShapeDtypeStruct((b, HG_HEADS, HG_DK, HG_DV), f32)],
        scratch_shapes=[pltpu.VMEM((HG_HEADS, HG_DV, HG_DK), f32)],
        compiler_params=_params("arbitrary", "arbitrary"),
        name="hgrn",
    )(qb, lf, kb, ib, gs, gnorm, s0)


def _ffn(hn_bf, wg_ref, wu_ref, wd_ref):
    g = _dot(hn_bf, wg_ref[...])
    u = _dot(hn_bf, wu_ref[...])
    return _dot((_silu(g) * u).astype(bf16), wd_ref[...])


def _resident(shape):
    return pl.BlockSpec(shape, lambda *_: (0,) * len(shape), pipeline_mode=pl.Buffered(1))


def _ffn_specs():
    return [_resident((D_MODEL, D_FF)), _resident((D_MODEL, D_FF)), _resident((D_FF, D_MODEL))]


def _even_post_kernel(x_ref, mod_ref, oa_ref, ob_ref, wo_ref, gf_ref, wg_ref, wu_ref, wd_ref, out_ref):
    nb, tt, d = x_ref.shape
    m = nb * tt
    o = (_dot(oa_ref[...].reshape(m, SB_WIDTH), wo_ref[:SB_WIDTH, :])
         + _dot(ob_ref[...].reshape(m, HG_W), wo_ref[SB_WIDTH:, :]))
    x1 = x_ref[...] + (1.0 + mod_ref[:, 2:3, :]) * o.reshape(nb, tt, d)
    hn = _norm_mod(x1, gf_ref[...], mod_ref[:, 4:5, :], mod_ref[:, 3:4, :])
    y = _ffn(hn.reshape(m, d).astype(bf16), wg_ref, wu_ref, wd_ref)
    out_ref[...] = x1 + (1.0 + mod_ref[:, 5:6, :]) * y.reshape(nb, tt, d)


def _even_post(x, mod, oa, ob, wo, gf, wg, wu, wd, nb, tt):
    b, t, d = x.shape
    blk = lambda width: pl.BlockSpec((nb, tt, width), lambda i, tb: (i, tb, 0))
    return pl.pallas_call(
        _even_post_kernel,
        grid=(b // nb, t // tt),
        in_specs=[blk(d), pl.BlockSpec((nb, N_MOD, d), lambda i, tb: (i, 0, 0)),
                  blk(SB_WIDTH), blk(HG_W), _resident((d, d)), _resident((1, d))] + _ffn_specs(),
        out_specs=blk(d),
        out_shape=jax.ShapeDtypeStruct((b, t, d), f32),
        compiler_params=_params("arbitrary", "arbitrary"),
        name="even_post_ffn",
    )(x, mod, oa, ob, wo, gf, wg, wu, wd)


def _linear_scan(a, u, h_init, tt):
    m, d = a.shape
    n_group = m // SUBLANES
    a3 = a.reshape(n_group, SUBLANES, d)
    u3 = u.reshape(n_group, SUBLANES, d)
    sub = lax.broadcasted_iota(jnp.int32, (1, SUBLANES, d), 1)
    step = 1
    while step < SUBLANES:
        keep = sub >= step
        a_prev = jnp.where(keep, pltpu.roll(a3, step, 1), 1.0)
        u_prev = jnp.where(keep, pltpu.roll(u3, step, 1), 0.0)
        u3 = a3 * u_prev + u3
        a3 = a3 * a_prev
        step *= 2
    groups_per_seq = tt // SUBLANES
    carry = None
    groups = []
    for gi in range(n_group):
        if gi % groups_per_seq == 0:
            carry = h_init[gi // groups_per_seq]
        h_group = u3[gi] + a3[gi] * carry
        groups.append(h_group)
        carry = h_group[SUBLANES - 1:SUBLANES, :]
    return jnp.concatenate(groups, axis=0)


def _odd_kernel(stream_start, pipelined, n_tb, x_ref, mod_ref, mod_ffn_ref, gm_ref, win_ref, cw_ref, cb_ref,
                wax_ref, ba_ref, bx_ref, lam_ref, wo_ref, conv0_ref, h0_ref, gf_ref, fn_ref,
                wg_ref, wu_ref, wd_ref, y_ref, conv_ref, hout_ref, xs_scr, hprev_scr, x1_scr, hn_scr):
    nb, tt, d = x_ref.shape
    m = nb * tt
    s = pl.program_id(0)
    tb = lax.rem(s, n_tb)

    def ffn_units(x1, hn_bf, out):
        acts = []
        for j in range(D_FF // MXU_COLS):
            cols = slice(j * MXU_COLS, (j + 1) * MXU_COLS)
            g = _dot(hn_bf, wg_ref[:, cols])
            u = _dot(hn_bf, wu_ref[:, cols])
            acts.append((_silu(g) * u).astype(bf16))
            yield
        act = jnp.concatenate(acts, axis=1)
        parts = []
        for n in range(d // MXU_COLS):
            cols = slice(n * MXU_COLS, (n + 1) * MXU_COLS)
            y = _dot(act, wd_ref[:, cols]).reshape(nb, tt, MXU_COLS)
            parts.append(x1[:, :, cols] + (1.0 + mod_ffn_ref[:, 5:6, cols]) * y)
            yield
        x2 = jnp.concatenate(parts, axis=-1)
        ms = jnp.mean(x2 * x2, axis=-1, keepdims=True)
        out["y"] = x2 * lax.rsqrt(ms + EPS) * fn_ref[...]

    def mixer_units(out):
        x = x_ref[...]
        hn = _norm_mod(x, gm_ref[...], mod_ref[:, 1:2, :], mod_ref[:, 0:1, :])
        hn = hn.reshape(m, d).astype(bf16)
        yield
        gate = _dot(hn, win_ref[:, :d])
        yield
        xbr = _dot(hn, win_ref[:, d:]).reshape(nb, tt, d)
        yield
        xs_scr[:, HALO:HALO + tt, :] = xbr
        xc = cb_ref[...] + xbr * cw_ref[CONV_W - 1:CONV_W, :]
        for k in range(1, CONV_W):
            xc = xc + xs_scr[:, HALO - k:HALO - k + tt, :] * cw_ref[CONV_W - 1 - k:CONV_W - k, :]
        tail = xs_scr[:, HALO + tt - (CONV_W - 1):HALO + tt, :]
        xs_scr[:, HALO - (CONV_W - 1):HALO, :] = tail
        xc = xc.reshape(m, d)
        xc_bf = xc.astype(bf16)
        yield
        lam = lam_ref[...]
        log_sig_lam = -(jnp.maximum(-lam, 0.0) + jnp.log1p(jnp.exp(-jnp.abs(lam))))
        y_in, h_last = [], []
        for blk in range(LRU_BLOCKS):
            cols = slice(blk * LRU_BLK, (blk + 1) * LRU_BLK)
            both = _dot(xc_bf[:, cols], wax_ref[blk])
            r = _sigmoid(both[:, :LRU_BLK] + ba_ref[:, cols])
            gi = _sigmoid(both[:, LRU_BLK:] + bx_ref[:, cols])
            log_a = C_SCALE * r * log_sig_lam[:, cols]
            a = jnp.exp(log_a)
            mult = jnp.sqrt(-jnp.tanh(log_a) * (1.0 + a * a))
            if stream_start:
                row = (lax.broadcasted_iota(jnp.int32, (m, LRU_BLK), 0) & (tt - 1)) + tb * tt
                mult = jnp.where(row == 0, 1.0, mult)
            h = _linear_scan(a, mult * gi * xc[:, cols], hprev_scr[:, :, cols], tt)
            h_last.append(h.reshape(nb, tt, LRU_BLK)[:, tt - 1:tt, :])
            y_in.append((jax.nn.gelu(gate[:, cols]) * h).astype(bf16))
            yield
        h_last = jnp.concatenate(h_last, axis=-1)
        hprev_scr[...] = h_last
        y_in = jnp.concatenate(y_in, axis=1)
        o = _dot(y_in, wo_ref[...])
        yield
        x1 = x + (1.0 + mod_ref[:, 2:3, :]) * o.reshape(nb, tt, d)
        hn2 = _norm_mod(x1, gf_ref[...], mod_ref[:, 4:5, :], mod_ref[:, 3:4, :])
        out.update(x1=x1, hn2=hn2.reshape(m, d).astype(bf16), tail=tail, h_last=h_last)

    @pl.when(tb == 0)
    def _():
        xs_scr[:, HALO - (CONV_W - 1):HALO, :] = conv0_ref[...]
        hprev_scr[...] = h0_ref[...]

    res = {}
    if pipelined:
        @pl.when(s == 0)
        def _():
            x1_scr[...] = jnp.zeros_like(x1_scr)
            hn_scr[...] = jnp.zeros_like(hn_scr)

        units = [ffn_units(x1_scr[...], hn_scr[...], res), mixer_units(res)]
        while units:
            units = [gen for gen in units if next(gen, "done") != "done"]
        y_ref[...] = res["y"]
        x1_scr[...] = res["x1"]
        hn_scr[...] = res["hn2"]

        @pl.when(s < pl.num_programs(0) - 1)
        def _():
            conv_ref[...] = res["tail"]
            hout_ref[...] = res["h_last"]
    else:
        for _ in mixer_units(res):
            pass
        conv_ref[...] = res["tail"]
        hout_ref[...] = res["h_last"]
        for _ in ffn_units(res["x1"], res["hn2"], res):
            pass
        y_ref[...] = res["y"]


def _odd_layer(x, mod, gm, win, cw, cb, wax, ba, bx, lam, wo, conv0, h0, gf, fn, wg, wu, wd,
               nb, tt, stream_start):
    b, t, d = x.shape
    n_tb = t // tt
    n_blocks = (b // nb) * n_tb
    pipelined = n_blocks > 1
    mix_blk = lambda s: jnp.minimum(s, n_blocks - 1)
    ffn_blk = (lambda s: jnp.maximum(s - 1, 0)) if pipelined else mix_blk
    rows_spec = lambda which: pl.BlockSpec((nb, tt, d), lambda s: (which(s) // n_tb, which(s) % n_tb, 0))
    per_seq = lambda rows, which: pl.BlockSpec((nb, rows, d), lambda s: (which(s) // n_tb, 0, 0))
    row = _resident((1, d))
    return pl.pallas_call(
        functools.partial(_odd_kernel, stream_start, pipelined, n_tb),
        grid=(n_blocks + (1 if pipelined else 0),),
        in_specs=[rows_spec(mix_blk), per_seq(N_MOD, mix_blk), per_seq(N_MOD, ffn_blk), row,
                  _resident((d, 2 * d)), _resident((CONV_W, d)),
                  row, _resident((LRU_BLOCKS, LRU_BLK, 2 * LRU_BLK)), row,
                  row, row, _resident((d, d)), per_seq(CONV_W - 1, mix_blk), per_seq(1, mix_blk),
                  row, row] + _ffn_specs(),
        out_specs=[rows_spec(ffn_blk), per_seq(CONV_W - 1, mix_blk), per_seq(1, mix_blk)],
        out_shape=[jax.ShapeDtypeStruct((b, t, d), f32),
                   jax.ShapeDtypeStruct((b, CONV_W - 1, d), f32),
                   jax.ShapeDtypeStruct((b, 1, d), f32)],
        scratch_shapes=[pltpu.VMEM((nb, HALO + tt, d), f32), pltpu.VMEM((nb, 1, d), f32),
                        pltpu.VMEM((nb, tt, d) if pipelined else (1, SUBLANES, LANES), f32),
                        pltpu.VMEM((nb * tt, d) if pipelined else (2 * SUBLANES, LANES), bf16)],
        compiler_params=_params("arbitrary"),
        name="odd_layer_ffn",
    )(x, mod, mod, gm, win, cw, cb, wax, ba, bx, lam, wo, conv0, h0, gf, fn, wg, wu, wd)


def _trunk(x, mods, past, weights, nb, tt, hg_tt):
    b, t, d = x.shape
    sample = past is not None
    row = lambda v: v.reshape(1, -1)

    qa, ka, va, ka_bf, va_bf, qb, lf, kb, ib, gs = _even_pre(
        x, mods[0], row(weights["norm_mix"][0]), weights["w_in_even"], weights["w_kv_t"],
        weights["hg_lb_logits"], 0, nb, tt, not sample)
    if sample:
        past_k, past_v, s0, conv0, h0 = past
        oa = _sb_sample(qa, ka_bf, va_bf, past_k, past_v, LANES)
        heads = lambda a: a.reshape(1, b, t, SB_HEADS, SB_DIM)
    else:
        oa = _sb_prompt(qa, ka_bf, va_bf)
        heads = lambda a: a.reshape(b, SB_HEADS, SB_DIM, t).transpose(0, 3, 1, 2)[None]
        s0 = jnp.zeros((b, HG_HEADS, HG_DK, HG_DV), f32)
        conv0 = jnp.zeros((b, CONV_W - 1, d), f32)
        h0 = jnp.zeros((b, 1, d), f32)
    ob, s_new = _hgrn(qb, lf, kb, ib, gs, row(weights["hg_gnorm"]), s0, hg_tt)
    x = _even_post(x, mods[0], oa, ob, weights["w_out_even"], row(weights["norm_ffn"][0]),
                   weights["ffn_wg"][0], weights["ffn_wu"][0], weights["ffn_wd"][0], nb, tt)

    y, conv_new, h_new = _odd_layer(
        x, mods[1], row(weights["norm_mix"][1]), weights["w_in_odd"], weights["conv_w"],
        row(weights["conv_b"]), weights["lru_wax"], row(weights["lru_ba"]), row(weights["lru_bx"]),
        row(weights["lru_lambda"]), weights["w_out_odd"], conv0, h0,
        row(weights["norm_ffn"][1]), row(weights["final_norm"]),
        weights["ffn_wg"][1], weights["ffn_wu"][1], weights["ffn_wd"][1],
        nb, tt, not sample)

    return (y, heads(ka), heads(va), s_new[None], conv_new[None], h_new.reshape(1, b, d))


def kernel(x_prompt, x_sample, cache_sb_k, cache_sb_v, state_hgrn, state_conv, state_lru, c_prompt, c_sample, norm_mix, norm_ffn, w_ada, b_ada, w_in_even, w_out_even, hg_gnorm, hg_lb_logits, w_in_odd, conv_w, conv_b, lru_wa, lru_ba, lru_wx, lru_bx, lru_lambda, w_out_odd, ffn_wg, ffn_wu, ffn_wd, final_norm):
    bp = x_prompt.shape[0]
    bs, ts, d = x_sample.shape
    weights = {
        "norm_mix": norm_mix, "norm_ffn": norm_ffn,
        "w_in_even": w_in_even[0].astype(bf16), "w_out_even": w_out_even[0].astype(bf16),
        "w_kv_t": jnp.stack([w_in_even[0][:, SEG:2 * SEG].T, w_in_even[0][:, 2 * SEG:3 * SEG].T]).astype(bf16),
        "hg_gnorm": hg_gnorm[0], "hg_lb_logits": hg_lb_logits,
        "w_in_odd": w_in_odd[0].astype(bf16), "conv_w": conv_w[0], "conv_b": conv_b[0],
        "lru_wax": jnp.concatenate([lru_wa[0], lru_wx[0]], axis=-1).astype(bf16),
        "lru_ba": lru_ba[0], "lru_bx": lru_bx[0], "lru_lambda": lru_lambda[0],
        "w_out_odd": w_out_odd[0].astype(bf16),
        "ffn_wg": ffn_wg.astype(bf16), "ffn_wu": ffn_wu.astype(bf16), "ffn_wd": ffn_wd.astype(bf16),
        "final_norm": final_norm,
    }
    mods = _ada(jnp.concatenate([c_prompt, c_sample], axis=0), w_ada, b_ada)
    mods = mods.reshape(DEPTH, bp + bs, N_MOD, d)

    out_p = _trunk(x_prompt, mods[:, :bp], None, weights, nb=1, tt=512, hg_tt=512)
    cache_t = lambda c: c[0].transpose(0, 2, 3, 1).reshape(bs, SB_WIDTH, -1)
    past = (cache_t(cache_sb_k), cache_t(cache_sb_v),
            state_hgrn[0], state_conv[0], state_lru[0].reshape(bs, 1, d))
    out_s = _trunk(x_sample, mods[:, bp:], past, weights, nb=bs, tt=ts, hg_tt=ts)

    y_p, k_p, v_p, s_p, conv_p, h_p = out_p
    y_s, k_s, v_s, s_s, conv_s, h_s = out_s
    return (y_p, y_s, k_p, v_p, s_p, conv_p, h_p, k_s, v_s, s_s, conv_s, h_s)
```

```python
import functools
import math

import jax
import jax.numpy as jnp
from jax import lax
from jax.experimental import pallas as pl
from jax.experimental.pallas import tpu as pltpu

f32 = jnp.float32
bf16 = jnp.bfloat16

D_MODEL = 1024
DEPTH = 2
EPS = 1e-6
N_MOD = 6

SB_HEADS = 8
SB_DIM = 64
SB_WIDTH = SB_HEADS * SB_DIM
HG_HEADS = 4
HG_DK = 128
HG_DV = 128
HG_W = HG_HEADS * HG_DK
SEG = 512
N_SEG = 7
HG_CHUNK = 64
HG_SUB = 16
HG_MATMUL_MIN_LOG = -40.0
LRU_BLOCKS = 8
LRU_BLK = D_MODEL // LRU_BLOCKS
CONV_W = 4
C_SCALE = 8.0
D_FF = 2816

SB_WINDOW = 2
SB_QBLOCKS = 2
SB_SKIP_LOG = -105.0

MXU_COLS = 256

LANES = 128
SUBLANES = 8
HALO = 8
VMEM_LIMIT = 60 * 1024 * 1024

NT_DIMS = (((1,), (1,)), ((), ()))
TN_DIMS = (((0,), (0,)), ((), ()))


def _dot(a, b):
    return jnp.dot(a, b, preferred_element_type=f32)


def _dot_nt(a, b):
    return lax.dot_general(a, b, NT_DIMS, preferred_element_type=f32)


def _split3(x):
    hi = x.astype(bf16)
    r1 = x - hi.astype(f32)
    mid = r1.astype(bf16)
    lo = (r1 - mid.astype(f32)).astype(bf16)
    return hi, mid, lo


def _split2(x):
    hi = x.astype(bf16)
    return hi, (x - hi.astype(f32)).astype(bf16)


def _sigmoid(x):
    return jax.nn.sigmoid(x)


def _silu(x):
    return x * _sigmoid(x)


def _norm_mod(x, g, scale, shift):
    ms = jnp.mean(x * x, axis=-1, keepdims=True)
    return x * lax.rsqrt(ms + EPS) * (g * (1.0 + scale)) + shift


def _params(*sem):
    return pltpu.CompilerParams(dimension_semantics=sem, vmem_limit_bytes=VMEM_LIMIT)


def _ada_kernel(c_ref, w_ref, b_ref, o_ref):
    c = c_ref[...]
    sc = _silu(c)
    w = w_ref[...]
    s_hi = sc.astype(bf16)
    s_lo = (sc - s_hi.astype(f32)).astype(bf16)
    w_hi = w.astype(bf16)
    w_lo = (w - w_hi.astype(f32)).astype(bf16)
    o_ref[...] = _dot(s_hi, w_hi) + _dot(s_lo, w_hi) + _dot(s_hi, w_lo) + b_ref[...]


def _ada(c_all, w_ada, b_ada):
    rows = c_all.shape[0]
    n_out = N_MOD * D_MODEL
    tn = 1536
    return pl.pallas_call(
        _ada_kernel,
        grid=(DEPTH, n_out // tn),
        in_specs=[
            pl.BlockSpec((rows, D_MODEL), lambda l, j: (0, 0)),
            pl.BlockSpec((None, D_MODEL, tn), lambda l, j: (l, 0, j)),
            pl.BlockSpec((None, 1, tn), lambda l, j: (l, 0, j)),
        ],
        out_specs=pl.BlockSpec((None, rows, tn), lambda l, j: (l, 0, j)),
        out_shape=jax.ShapeDtypeStruct((DEPTH, rows, n_out), f32),
        compiler_params=_params("arbitrary", "arbitrary"),
        name="ada_mod",
    )(c_all, w_ada, b_ada.reshape(DEPTH, 1, n_out))


def _even_pre_kernel(layer, kv_transposed, x_ref, mod_ref, g_ref, w_ref, wkvt_ref, lbl_ref,
                     qa_ref, ka_ref, va_ref, kab_ref, vab_ref,
                     qb_ref, lf_ref, kb_ref, ib_ref, gs_ref):
    nb, tt, d = x_ref.shape
    m = nb * tt
    hn = _norm_mod(x_ref[...], g_ref[...], mod_ref[:, 1:2, :], mod_ref[:, 0:1, :])
    hn = hn.reshape(m, d).astype(bf16)

    def seg(i):
        return _dot(hn, w_ref[:, i * SEG:(i + 1) * SEG]).reshape(nb, tt, SEG)

    qa_ref[...] = (seg(0) * (1.0 / math.sqrt(SB_DIM))).astype(bf16)
    if kv_transposed:
        for i, (full_ref, blocks_ref) in enumerate(((ka_ref, kab_ref), (va_ref, vab_ref))):
            kt = _dot_nt(wkvt_ref[i], hn)
            full_ref[...] = kt
            for c in range(m // LANES):
                blocks_ref[c] = kt[:, c * LANES:(c + 1) * LANES].astype(bf16)
    else:
        ka = seg(1)
        ka_ref[...] = ka
        kab_ref[...] = ka.astype(bf16)
        va = seg(2)
        va_ref[...] = va
        vab_ref[...] = va.astype(bf16)
    qb_ref[...] = _silu(seg(3)) * (HG_DK ** -0.5)

    lg = lbl_ref[...]
    e = jnp.exp(lg - jnp.max(lg, axis=0, keepdims=True))
    lb = jnp.sum(e[:layer + 1], axis=0, keepdims=True) / jnp.sum(e, axis=0, keepdims=True)
    fb = seg(4)
    f = lb + (1.0 - lb) * _sigmoid(fb)
    lf_ref[...] = jnp.log(f)
    kb_ref[...] = (1.0 - lb) * _sigmoid(-fb)
    ib_ref[...] = seg(5)
    gs_ref[...] = _silu(seg(6))


def _even_pre(x, mod, g, w_bf, wkvt_bf, lb_logits, layer, nb, tt, kv_transposed):
    b, t, d = x.shape
    grid = (b // nb, t // tt)
    blk = lambda width: pl.BlockSpec((nb, tt, width), lambda i, j: (i, j, 0))
    out_f = jax.ShapeDtypeStruct((b, t, SEG), f32)
    out_b = jax.ShapeDtypeStruct((b, t, SEG), bf16)
    if kv_transposed:
        assert nb == 1 and tt % LANES == 0
        kv_f, kv_b = jax.ShapeDtypeStruct((b, SEG, t), f32), jax.ShapeDtypeStruct((b, t // LANES, SEG, LANES), bf16)
        kv_f_spec = pl.BlockSpec((None, SEG, tt), lambda i, j: (i, 0, j))
        kv_b_spec = pl.BlockSpec((None, tt // LANES, SEG, LANES), lambda i, j: (i, j, 0, 0))
    else:
        kv_f, kv_b, kv_f_spec, kv_b_spec = out_f, out_b, blk(SEG), blk(SEG)
    return pl.pallas_call(
        functools.partial(_even_pre_kernel, layer, kv_transposed),
        grid=grid,
        in_specs=[
            blk(d),
            pl.BlockSpec((nb, N_MOD, d), lambda i, j: (i, 0, 0)),
            _resident((1, d)),
            _resident((d, N_SEG * SEG)),
            _resident((2, SEG, d)),
            _resident((DEPTH + 1, HG_W)),
        ],
        out_specs=[blk(SEG), kv_f_spec, kv_f_spec, kv_b_spec, kv_b_spec] + [blk(SEG)] * 5,
        out_shape=[out_b, kv_f, kv_f, kv_b, kv_b, out_f, out_f, out_f, out_f, out_f],
        compiler_params=_params("arbitrary", "arbitrary"),
        name="even_pre",
    )(x, mod, g, w_bf, wkvt_bf, lb_logits)


def _sb_consts(tk, scale=1.0):
    r = lax.broadcasted_iota(jnp.int32, (tk, LANES + tk), 0)
    c = lax.broadcasted_iota(jnp.int32, (tk, LANES + tk), 1)
    m = jnp.where((c < LANES) | (r > c - LANES), scale, 0.0).astype(bf16)
    return jnp.concatenate([m, m], axis=0)


def _sb_groups(groups):
    zs = [[_dot(q2, k_blk) if transposed else _dot_nt(q2, k_blk)
           for k_blk, _, _, _, transposed in blocks] for q2, blocks, _, _ in groups]
    l1mbs, sums = [], []
    for g, (_, blocks, _, _) in enumerate(groups):
        l1mbs.append([])
        sums.append([])
        for z, (_, _, sum_matrix, mask, _) in zip(zs[g], blocks):
            l1mb = -(jnp.maximum(z, 0.0) + jnp.log(1.0 + jnp.exp(-jnp.abs(z))))
            if mask is not None:
                l1mb = jnp.where(mask, l1mb, 0.0)
            l1mbs[g].append(l1mb)
            sums[g].append(_dot(jnp.concatenate(_split2(l1mb), axis=1), sum_matrix))
    results = []
    for g, (_, blocks, carry, acc) in enumerate(groups):
        for z, l1mb, s, (_, v_blk, _, mask, transposed) in zip(zs[g], l1mbs[g], sums[g], blocks):
            tk = z.shape[1]
            after = carry[:, :tk] + s[:, LANES:]
            w = jnp.exp(z + l1mb + after)
            if mask is not None:
                w = jnp.where(mask, w, 0.0)
            w = w.astype(bf16)
            acc = acc + (_dot_nt(w, v_blk) if transposed else _dot(w, v_blk))
            carry = carry + s[:, :LANES]
        results.append((carry, acc))
    return results


def _sb_blocks(q2, blocks, carry, acc):
    return _sb_groups([(q2, blocks, carry, acc)])[0]


def _sb_finish(j0, carry, acc, step):
    def cond(state):
        j, carry, _ = state
        return (j >= 0) & (jnp.max(carry) > SB_SKIP_LOG)

    def body(state):
        j, carry, acc = state
        carry, acc = step(j, carry, acc)
        return j - 1, carry, acc

    return lax.while_loop(cond, body, (j0, carry, acc))[2]


def _stack_heads(q):
    lane = lax.broadcasted_iota(jnp.int32, q.shape, 1)
    zero = jnp.zeros_like(q)
    return jnp.concatenate([jnp.where(lane < SB_DIM, q, zero), jnp.where(lane >= SB_DIM, q, zero)], axis=0)


def _unstack_heads(acc2):
    tq = acc2.shape[0] // 2
    lane = lax.broadcasted_iota(jnp.int32, (tq, LANES), 1)
    return jnp.where(lane < SB_DIM, acc2[:tq], acc2[tq:])


def _causal_mask2(tq):
    r = lax.broadcasted_iota(jnp.int32, (2 * tq, tq), 0) & (tq - 1)
    c = lax.broadcasted_iota(jnp.int32, (2 * tq, tq), 1)
    return c < r


def _sb_prompt_kernel(q_ref, k_ref, v_ref, o_ref):
    tq = k_ref.shape[2]
    n_q = q_ref.shape[0] // tq
    i0 = pl.program_id(2) * n_q
    sum_matrix = _sb_consts(tq)
    diag_mask = _causal_mask2(tq)
    zero = jnp.zeros((2 * tq, LANES), f32)

    groups = []
    for g in range(n_q):
        i = i0 + g
        q2 = _stack_heads(q_ref[g * tq:(g + 1) * tq, :])
        blocks = [(k_ref[i], v_ref[i], sum_matrix, diag_mask, True)]
        for c in range(1, SB_WINDOW + 1):
            scale = jnp.where(i >= c, 1.0, 0.0)
            j = jnp.maximum(i - c, 0)
            blocks.append((k_ref[j], (v_ref[j].astype(f32) * scale).astype(bf16), _sb_consts(tq, scale),
                           None, True))
        groups.append((q2, blocks, zero, zero))

    for g, (carry, acc) in enumerate(_sb_groups(groups)):
        q2 = groups[g][0]

        def step(j, carry, acc, q2=q2):
            return _sb_blocks(q2, [(k_ref[j], v_ref[j], sum_matrix, None, True)], carry, acc)

        acc = _sb_finish(i0 + g - SB_WINDOW - 1, carry, acc, step)
        o_ref[g * tq:(g + 1) * tq, :] = _unstack_heads(acc).astype(o_ref.dtype)


def _sb_prompt(q, k_blocks, v_blocks):
    b, t, _ = q.shape
    tk = LANES
    tq = SB_QBLOCKS * tk
    n_pair = SB_WIDTH // LANES
    qspec = pl.BlockSpec((None, tq, LANES), lambda bi, p, i: (bi, i, p))
    kvspec = pl.BlockSpec((None, t // tk, LANES, tk), lambda bi, p, i: (bi, 0, p, 0))
    return pl.pallas_call(
        _sb_prompt_kernel,
        grid=(b, n_pair, t // tq),
        in_specs=[qspec, kvspec, kvspec],
        out_specs=qspec,
        out_shape=jax.ShapeDtypeStruct((b, t, SB_WIDTH), bf16),
        compiler_params=_params("arbitrary", "arbitrary", "arbitrary"),
        name="sb_prompt",
    )(q, k_blocks, v_blocks)


def _sb_sample_kernel(tk, n_past, q_ref, k_ref, v_ref, wk_ref, wv_ref, pk_hbm, pv_hbm, o_ref,
                      kbuf, vbuf, sem):
    tq = q_ref.shape[0]
    b = pl.program_id(0)
    zero = jnp.zeros((2 * tq, LANES), f32)
    sum_matrix = _sb_consts(tk)
    diag_matrix = _sb_consts(tq)
    diag_mask = _causal_mask2(tq)

    def fetch(j):
        keys = pl.ds(pl.multiple_of(j * tk, tk), tk)
        copies = [pltpu.make_async_copy(pk_hbm.at[b, :, keys], kbuf, sem.at[0]),
                  pltpu.make_async_copy(pv_hbm.at[b, :, keys], vbuf, sem.at[1])]
        for cp in copies:
            cp.start()
        for cp in copies:
            cp.wait()

    groups = []
    for pair in range(SB_WIDTH // LANES):
        cols = slice(pair * LANES, (pair + 1) * LANES)
        q2 = _stack_heads(q_ref[:, cols])
        blocks = [(k_ref[:, cols], v_ref[:, cols], diag_matrix, diag_mask, False)]
        for c in range(1, SB_WINDOW + 1):
            keys = slice((SB_WINDOW - c) * tk, (SB_WINDOW - c + 1) * tk)
            blocks.append((wk_ref[cols, keys].astype(bf16), wv_ref[cols, keys].astype(bf16),
                           sum_matrix, None, True))
        groups.append((q2, blocks, zero, zero))

    for pair, (carry, acc) in enumerate(_sb_groups(groups)):
        cols = slice(pair * LANES, (pair + 1) * LANES)
        q2 = groups[pair][0]

        def step(j, carry, acc, cols=cols, q2=q2):
            fetch(j)
            blk = (kbuf[cols, :].astype(bf16), vbuf[cols, :].astype(bf16), sum_matrix, None, True)
            return _sb_blocks(q2, [blk], carry, acc)

        acc = _sb_finish(n_past - SB_WINDOW - 1, carry, acc, step)
        o_ref[:, cols] = _unstack_heads(acc).astype(o_ref.dtype)


def _sb_sample(q, k, v, past_kt, past_vt, tk):
    b, t, _ = q.shape
    n_past = past_kt.shape[2] // tk
    assert n_past % SB_WINDOW == 0 and n_past >= SB_WINDOW
    qspec = pl.BlockSpec((None, t, SB_WIDTH), lambda bi: (bi, 0, 0))
    wspec = pl.BlockSpec((None, SB_WIDTH, SB_WINDOW * tk), lambda bi: (bi, 0, n_past // SB_WINDOW - 1))
    hbm = pl.BlockSpec(memory_space=pl.ANY)
    return pl.pallas_call(
        functools.partial(_sb_sample_kernel, tk, n_past),
        grid=(b,),
        in_specs=[qspec, qspec, qspec, wspec, wspec, hbm, hbm],
        out_specs=qspec,
        out_shape=jax.ShapeDtypeStruct((b, t, SB_WIDTH), bf16),
        scratch_shapes=[pltpu.VMEM((SB_WIDTH, tk), f32), pltpu.VMEM((SB_WIDTH, tk), f32),
                        pltpu.SemaphoreType.DMA((2,))],
        compiler_params=_params("arbitrary"),
        name="sb_sample",
    )(q, k, v, past_kt, past_vt, past_kt, past_vt)


def _hgrn_head_chunk(q, k, v, b, st):
    n_sub = HG_CHUNK // HG_SUB
    v_bf = v.astype(bf16)
    o_inter = _dot_nt((q * jnp.exp(b)).astype(bf16), st.astype(bf16))
    t_idx = lax.broadcasted_iota(jnp.int32, (HG_SUB, HG_SUB, HG_DK), 0)
    s_idx = lax.broadcasted_iota(jnp.int32, (HG_SUB, HG_SUB, HG_DK), 1)
    causal = s_idx <= t_idx
    outs = []
    for i in range(n_sub):
        lo = i * HG_SUB
        rows = slice(lo, lo + HG_SUB)
        qi, ki, bi = q[rows], k[rows], b[rows]
        diff = bi[:, None, :] - bi[None, :, :]
        dec = jnp.exp(jnp.where(causal, diff, -1e30))
        a_diag = jnp.sum(qi[:, None, :] * ki[None, :, :] * dec, axis=-1)
        o_i = _dot(a_diag.astype(bf16), v_bf[rows])
        if i > 0:
            ref = b[lo - 1:lo]
            q_t = qi * jnp.exp(bi - ref)
            k_t = k[:lo] * jnp.exp(ref - b[:lo])
            a_off = _dot_nt(q_t.astype(bf16), k_t.astype(bf16))
            o_i = o_i + _dot(a_off.astype(bf16), v_bf[:lo])
        outs.append(o_inter[rows] + o_i)
    o = jnp.concatenate(outs, axis=0)
    b_last = b[HG_CHUNK - 1:HG_CHUNK]
    k_dec = (k * jnp.exp(b_last - b)).astype(bf16)
    st_new = jnp.exp(b_last) * st + lax.dot_general(v_bf, k_dec, TN_DIMS, preferred_element_type=f32)
    return o, st_new


def _hgrn_chunk_matmul(qs, ks, vs, bs, sts):
    n_sub = HG_CHUNK // HG_SUB
    heads = range(len(qs))
    q_in, q_t, k_t, k_dec, v_bf, b_last = [], [], [], [], [], []
    for h in heads:
        q, k, b = qs[h], ks[h], bs[h]
        q_in.append((q * jnp.exp(b)).astype(bf16))
        q_t.append([])
        k_t.append([])
        for i in range(n_sub):
            lo, hi = i * HG_SUB, (i + 1) * HG_SUB
            if i == 0:
                q_t[h].append(q_in[h][:hi])
                k_t[h].append((k[:hi] * jnp.exp(-b[:hi])).astype(bf16))
            else:
                ref = b[lo - 1:lo]
                q_t[h].append((q[lo:hi] * jnp.exp(b[lo:hi] - ref)).astype(bf16))
                k_t[h].append((k[:hi] * jnp.exp(ref - b[:hi])).astype(bf16))
        b_last.append(b[HG_CHUNK - 1:HG_CHUNK])
        k_dec.append((k * jnp.exp(b_last[h] - b)).astype(bf16))
        v_bf.append(vs[h].astype(bf16))

    o_inter = [_dot_nt(q_in[h], sts[h].astype(bf16)) for h in heads]
    attn = [[_dot_nt(q_t[h][i], k_t[h][i]) for i in range(n_sub)] for h in heads]
    st_add = [lax.dot_general(v_bf[h], k_dec[h], TN_DIMS, preferred_element_type=f32) for h in heads]

    outs, new_sts = [], []
    for h in heads:
        pieces = []
        for i in range(n_sub):
            hi = (i + 1) * HG_SUB
            r = lax.broadcasted_iota(jnp.int32, (HG_SUB, hi), 0)
            c = lax.broadcasted_iota(jnp.int32, (HG_SUB, hi), 1)
            a = jnp.where(c - i * HG_SUB <= r, attn[h][i], 0.0)
            pieces.append(_dot(a.astype(bf16), v_bf[h][:hi]))
        outs.append(jnp.concatenate(pieces, axis=0) + o_inter[h])
        new_sts.append(jnp.exp(b_last[h]) * sts[h] + st_add[h])
    return outs, new_sts


def _hgrn_kernel(q_ref, lf_ref, k_ref, v_ref, gs_ref, gn_ref, s0_ref, o_ref, sout_ref, st_scr):
    tb = pl.program_id(1)
    n_tb = pl.num_programs(1)
    n_chunk = q_ref.shape[0] // HG_CHUNK
    n_sub = HG_CHUNK // HG_SUB

    @pl.when(tb == 0)
    def _():
        for h in range(HG_HEADS):
            st_scr[h] = s0_ref[h].T

    r = lax.broadcasted_iota(jnp.int32, (HG_CHUNK, HG_CHUNK), 0)
    c = lax.broadcasted_iota(jnp.int32, (HG_CHUNK, HG_CHUNK), 1)
    lower = jnp.where(c <= r, 1.0, 0.0).astype(bf16)
    head_cols = [slice(h * HG_DK, (h + 1) * HG_DK) for h in range(HG_HEADS)]

    def chunk(ci, carry):
        rows = pl.ds(pl.multiple_of(ci * HG_CHUNK, HG_CHUNK), HG_CHUNK)
        b_all = _dot_exact_rhs_lhs(lower, lf_ref[rows, :])

        def finish(h, o, st_new):
            cols = head_cols[h]
            st_scr[h] = st_new
            o = o * lax.rsqrt(jnp.mean(o * o, axis=-1, keepdims=True) + EPS)
            o_ref[rows, cols] = (o * gn_ref[:, cols] * gs_ref[rows, cols]).astype(o_ref.dtype)

        ends = [b_all[(i + 1) * HG_SUB - 1:(i + 1) * HG_SUB, :] for i in range(n_sub)]
        span = ends[0]
        for i in range(1, n_sub):
            span = jnp.minimum(span, ends[i] - ends[i - 1])
        bounded = jnp.min(span) >= HG_MATMUL_MIN_LOG

        @pl.when(bounded)
        def _():
            outs, new_sts = _hgrn_chunk_matmul(
                [q_ref[rows, cols] for cols in head_cols], [k_ref[rows, cols] for cols in head_cols],
                [v_ref[rows, cols] for cols in head_cols], [b_all[:, cols] for cols in head_cols],
                [st_scr[h] for h in range(HG_HEADS)])
            for h in range(HG_HEADS):
                finish(h, outs[h], new_sts[h])

        @pl.when(jnp.logical_not(bounded))
        def _():
            for h, cols in enumerate(head_cols):
                o, st_new = _hgrn_head_chunk(q_ref[rows, cols], k_ref[rows, cols], v_ref[rows, cols],
                                             b_all[:, cols], st_scr[h])
                finish(h, o, st_new)

        return carry

    lax.fori_loop(0, n_chunk, chunk, 0)

    @pl.when(tb == n_tb - 1)
    def _():
        for h in range(HG_HEADS):
            sout_ref[h] = st_scr[h].T


def _dot_exact_rhs_lhs(m, x):
    hi, mid, lo = _split3(x)
    return _dot(m, hi) + _dot(m, mid) + _dot(m, lo)


def _hgrn(qb, lf, kb, ib, gs, gnorm, s0, tt):
    b, t, _ = qb.shape
    blk = pl.BlockSpec((None, tt, HG_W), lambda i, j: (i, j, 0))
    sspec = pl.BlockSpec((None, HG_HEADS, HG_DK, HG_DV), lambda i, j: (i, 0, 0, 0))
    return pl.pallas_call(
        _hgrn_kernel,
        grid=(b, t // tt),
        in_specs=[blk, blk, blk, blk, blk, pl.BlockSpec((1, HG_W), lambda i, j: (0, 0)), sspec],
        out_specs=[blk, sspec],
        out_shape=[jax.ShapeDtypeStruct((b, t, HG_W), bf16),
                   jax.ShapeDtypeStruct((b, HG_HEADS, HG_DK, HG_DV), f32)],
        scratch_shapes=[pltpu.VMEM((HG_HEADS, HG_DV, HG_DK), f32)],
        compiler_params=_params("arbitrary", "arbitrary"),
        name="hgrn",
    )(qb, lf, kb, ib, gs, gnorm, s0)


def _ffn(hn_bf, wg_ref, wu_ref, wd_ref):
    g = _dot(hn_bf, wg_ref[...])
    u = _dot(hn_bf, wu_ref[...])
    return _dot((_silu(g) * u).astype(bf16), wd_ref[...])


def _resident(shape):
    return pl.BlockSpec(shape, lambda *_: (0,) * len(shape), pipeline_mode=pl.Buffered(1))


def _ffn_specs():
    return [_resident((D_MODEL, D_FF)), _resident((D_MODEL, D_FF)), _resident((D_FF, D_MODEL))]


def _even_post_kernel(x_ref, mod_ref, oa_ref, ob_ref, wo_ref, gf_ref, wg_ref, wu_ref, wd_ref, out_ref):
    nb, tt, d = x_ref.shape
    m = nb * tt
    o = (_dot(oa_ref[...].reshape(m, SB_WIDTH), wo_ref[:SB_WIDTH, :])
         + _dot(ob_ref[...].reshape(m, HG_W), wo_ref[SB_WIDTH:, :]))
    x1 = x_ref[...] + (1.0 + mod_ref[:, 2:3, :]) * o.reshape(nb, tt, d)
    hn = _norm_mod(x1, gf_ref[...], mod_ref[:, 4:5, :], mod_ref[:, 3:4, :])
    y = _ffn(hn.reshape(m, d).astype(bf16), wg_ref, wu_ref, wd_ref)
    out_ref[...] = x1 + (1.0 + mod_ref[:, 5:6, :]) * y.reshape(nb, tt, d)


def _even_post(x, mod, oa, ob, wo, gf, wg, wu, wd, nb, tt):
    b, t, d = x.shape
    blk = lambda width: pl.BlockSpec((nb, tt, width), lambda i, tb: (i, tb, 0))
    return pl.pallas_call(
        _even_post_kernel,
        grid=(b // nb, t // tt),
        in_specs=[blk(d), pl.BlockSpec((nb, N_MOD, d), lambda i, tb: (i, 0, 0)),
                  blk(SB_WIDTH), blk(HG_W), _resident((d, d)), _resident((1, d))] + _ffn_specs(),
        out_specs=blk(d),
        out_shape=jax.ShapeDtypeStruct((b, t, d), f32),
        compiler_params=_params("arbitrary", "arbitrary"),
        name="even_post_ffn",
    )(x, mod, oa, ob, wo, gf, wg, wu, wd)


def _linear_scan(a, u, h_init, tt):
    m, d = a.shape
    n_group = m // SUBLANES
    a3 = a.reshape(n_group, SUBLANES, d)
    u3 = u.reshape(n_group, SUBLANES, d)
    sub = lax.broadcasted_iota(jnp.int32, (1, SUBLANES, d), 1)
    step = 1
    while step < SUBLANES:
        keep = sub >= step
        a_prev = jnp.where(keep, pltpu.roll(a3, step, 1), 1.0)
        u_prev = jnp.where(keep, pltpu.roll(u3, step, 1), 0.0)
        u3 = a3 * u_prev + u3
        a3 = a3 * a_prev
        step *= 2
    groups_per_seq = tt // SUBLANES
    carry = None
    groups = []
    for gi in range(n_group):
        if gi % groups_per_seq == 0:
            carry = h_init[gi // groups_per_seq]
        h_group = u3[gi] + a3[gi] * carry
        groups.append(h_group)
        carry = h_group[SUBLANES - 1:SUBLANES, :]
    return jnp.concatenate(groups, axis=0)


def _odd_kernel(stream_start, pipelined, n_tb, x_ref, mod_ref, mod_ffn_ref, gm_ref, win_ref, cw_ref, cb_ref,
                wax_ref, ba_ref, bx_ref, lam_ref, wo_ref, conv0_ref, h0_ref, gf_ref, fn_ref,
                wg_ref, wu_ref, wd_ref, y_ref, conv_ref, hout_ref, xs_scr, hprev_scr, x1_scr, hn_scr):
    nb, tt, d = x_ref.shape
    m = nb * tt
    s = pl.program_id(0)
    tb = lax.rem(s, n_tb)

    def ffn_units(x1, hn_bf, out):
        acts = []
        for j in range(D_FF // MXU_COLS):
            cols = slice(j * MXU_COLS, (j + 1) * MXU_COLS)
            g = _dot(hn_bf, wg_ref[:, cols])
            u = _dot(hn_bf, wu_ref[:, cols])
            acts.append((_silu(g) * u).astype(bf16))
            yield
        act = jnp.concatenate(acts, axis=1)
        parts = []
        for n in range(d // MXU_COLS):
            cols = slice(n * MXU_COLS, (n + 1) * MXU_COLS)
            y = _dot(act, wd_ref[:, cols]).reshape(nb, tt, MXU_COLS)
            parts.append(x1[:, :, cols] + (1.0 + mod_ffn_ref[:, 5:6, cols]) * y)
            yield
        x2 = jnp.concatenate(parts, axis=-1)
        ms = jnp.mean(x2 * x2, axis=-1, keepdims=True)
        out["y"] = x2 * lax.rsqrt(ms + EPS) * fn_ref[...]

    def mixer_units(out):
        x = x_ref[...]
        hn = _norm_mod(x, gm_ref[...], mod_ref[:, 1:2, :], mod_ref[:, 0:1, :])
        hn = hn.reshape(m, d).astype(bf16)
        yield
        gate = _dot(hn, win_ref[:, :d])
        yield
        xbr = _dot(hn, win_ref[:, d:]).reshape(nb, tt, d)
        yield
        xs_scr[:, HALO:HALO + tt, :] = xbr
        xc = cb_ref[...] + xbr * cw_ref[CONV_W - 1:CONV_W, :]
        for k in range(1, CONV_W):
            xc = xc + xs_scr[:, HALO - k:HALO - k + tt, :] * cw_ref[CONV_W - 1 - k:CONV_W - k, :]
        tail = xs_scr[:, HALO + tt - (CONV_W - 1):HALO + tt, :]
        xs_scr[:, HALO - (CONV_W - 1):HALO, :] = tail
        xc = xc.reshape(m, d)
        xc_bf = xc.astype(bf16)
        yield
        lam = lam_ref[...]
        log_sig_lam = -(jnp.maximum(-lam, 0.0) + jnp.log1p(jnp.exp(-jnp.abs(lam))))
        y_in, h_last = [], []
        for blk in range(LRU_BLOCKS):
            cols = slice(blk * LRU_BLK, (blk + 1) * LRU_BLK)
            both = _dot(xc_bf[:, cols], wax_ref[blk])
            r = _sigmoid(both[:, :LRU_BLK] + ba_ref[:, cols])
            gi = _sigmoid(both[:, LRU_BLK:] + bx_ref[:, cols])
            log_a = C_SCALE * r * log_sig_lam[:, cols]
            a = jnp.exp(log_a)
            mult = jnp.sqrt(-jnp.tanh(log_a) * (1.0 + a * a))
            if stream_start:
                row = (lax.broadcasted_iota(jnp.int32, (m, LRU_BLK), 0) & (tt - 1)) + tb * tt
                mult = jnp.where(row == 0, 1.0, mult)
            h = _linear_scan(a, mult * gi * xc[:, cols], hprev_scr[:, :, cols], tt)
            h_last.append(h.reshape(nb, tt, LRU_BLK)[:, tt - 1:tt, :])
            y_in.append((jax.nn.gelu(gate[:, cols]) * h).astype(bf16))
            yield
        h_last = jnp.concatenate(h_last, axis=-1)
        hprev_scr[...] = h_last
        y_in = jnp.concatenate(y_in, axis=1)
        o = _dot(y_in, wo_ref[...])
        yield
        x1 = x + (1.0 + mod_ref[:, 2:3, :]) * o.reshape(nb, tt, d)
        hn2 = _norm_mod(x1, gf_ref[...], mod_ref[:, 4:5, :], mod_ref[:, 3:4, :])
        out.update(x1=x1, hn2=hn2.reshape(m, d).astype(bf16), tail=tail, h_last=h_last)

    @pl.when(tb == 0)
    def _():
        xs_scr[:, HALO - (CONV_W - 1):HALO, :] = conv0_ref[...]
        hprev_scr[...] = h0_ref[...]

    res = {}
    if pipelined:
        @pl.when(s == 0)
        def _():
            x1_scr[...] = jnp.zeros_like(x1_scr)
            hn_scr[...] = jnp.zeros_like(hn_scr)

        units = [ffn_units(x1_scr[...], hn_scr[...], res), mixer_units(res)]
        while units:
            units = [gen for gen in units if next(gen, "done") != "done"]
        y_ref[...] = res["y"]
        x1_scr[...] = res["x1"]
        hn_scr[...] = res["hn2"]

        @pl.when(s < pl.num_programs(0) - 1)
        def _():
            conv_ref[...] = res["tail"]
            hout_ref[...] = res["h_last"]
    else:
        for _ in mixer_units(res):
            pass
        conv_ref[...] = res["tail"]
        hout_ref[...] = res["h_last"]
        for _ in ffn_units(res["x1"], res["hn2"], res):
            pass
        y_ref[...] = res["y"]


def _odd_layer(x, mod, gm, win, cw, cb, wax, ba, bx, lam, wo, conv0, h0, gf, fn, wg, wu, wd,
               nb, tt, stream_start):
    b, t, d = x.shape
    n_tb = t // tt
    n_blocks = (b // nb) * n_tb
    pipelined = n_blocks > 1
    mix_blk = lambda s: jnp.minimum(s, n_blocks - 1)
    ffn_blk = (lambda s: jnp.maximum(s - 1, 0)) if pipelined else mix_blk
    rows_spec = lambda which: pl.BlockSpec((nb, tt, d), lambda s: (which(s) // n_tb, which(s) % n_tb, 0))
    per_seq = lambda rows, which: pl.BlockSpec((nb, rows, d), lambda s: (which(s) // n_tb, 0, 0))
    row = _resident((1, d))
    return pl.pallas_call(
        functools.partial(_odd_kernel, stream_start, pipelined, n_tb),
        grid=(n_blocks + (1 if pipelined else 0),),
        in_specs=[rows_spec(mix_blk), per_seq(N_MOD, mix_blk), per_seq(N_MOD, ffn_blk), row,
                  _resident((d, 2 * d)), _resident((CONV_W, d)),
                  row, _resident((LRU_BLOCKS, LRU_BLK, 2 * LRU_BLK)), row,
                  row, row, _resident((d, d)), per_seq(CONV_W - 1, mix_blk), per_seq(1, mix_blk),
                  row, row] + _ffn_specs(),
        out_specs=[rows_spec(ffn_blk), per_seq(CONV_W - 1, mix_blk), per_seq(1, mix_blk)],
        out_shape=[jax.ShapeDtypeStruct((b, t, d), f32),
                   jax.ShapeDtypeStruct((b, CONV_W - 1, d), f32),
                   jax.ShapeDtypeStruct((b, 1, d), f32)],
        scratch_shapes=[pltpu.VMEM((nb, HALO + tt, d), f32), pltpu.VMEM((nb, 1, d), f32),
                        pltpu.VMEM((nb, tt, d) if pipelined else (1, SUBLANES, LANES), f32),
                        pltpu.VMEM((nb * tt, d) if pipelined else (2 * SUBLANES, LANES), bf16)],
        compiler_params=_params("arbitrary"),
        name="odd_layer_ffn",
    )(x, mod, mod, gm, win, cw, cb, wax, ba, bx, lam, wo, conv0, h0, gf, fn, wg, wu, wd)


def _trunk(x, mods, past, weights, nb, tt, hg_tt):
    b, t, d = x.shape
    sample = past is not None
    row = lambda v: v.reshape(1, -1)

    qa, ka, va, ka_bf, va_bf, qb, lf, kb, ib, gs = _even_pre(
        x, mods[0], row(weights["norm_mix"][0]), weights["w_in_even"], weights["w_kv_t"],
        weights["hg_lb_logits"], 0, nb, tt, not sample)
    if sample:
        past_k, past_v, s0, conv0, h0 = past
        oa = _sb_sample(qa, ka_bf, va_bf, past_k, past_v, LANES)
        heads = lambda a: a.reshape(1, b, t, SB_HEADS, SB_DIM)
    else:
        oa = _sb_prompt(qa, ka_bf, va_bf)
        heads = lambda a: a.reshape(b, SB_HEADS, SB_DIM, t).transpose(0, 3, 1, 2)[None]
        s0 = jnp.zeros((b, HG_HEADS, HG_DK, HG_DV), f32)
        conv0 = jnp.zeros((b, CONV_W - 1, d), f32)
        h0 = jnp.zeros((b, 1, d), f32)
    ob, s_new = _hgrn(qb, lf, kb, ib, gs, row(weights["hg_gnorm"]), s0, hg_tt)
    x = _even_post(x, mods[0], oa, ob, weights["w_out_even"], row(weights["norm_ffn"][0]),
                   weights["ffn_wg"][0], weights["ffn_wu"][0], weights["ffn_wd"][0], nb, tt)

    y, conv_new, h_new = _odd_layer(
        x, mods[1], row(weights["norm_mix"][1]), weights["w_in_odd"], weights["conv_w"],
        row(weights["conv_b"]), weights["lru_wax"], row(weights["lru_ba"]), row(weights["lru_bx"]),
        row(weights["lru_lambda"]), weights["w_out_odd"], conv0, h0,
        row(weights["norm_ffn"][1]), row(weights["final_norm"]),
        weights["ffn_wg"][1], weights["ffn_wu"][1], weights["ffn_wd"][1],
        nb, tt, not sample)

    return (y, heads(ka), heads(va), s_new[None], conv_new[None], h_new.reshape(1, b, d))


def kernel(x_prompt, x_sample, cache_sb_k, cache_sb_v, state_hgrn, state_conv, state_lru, c_prompt, c_sample, norm_mix, norm_ffn, w_ada, b_ada, w_in_even, w_out_even, hg_gnorm, hg_lb_logits, w_in_odd, conv_w, conv_b, lru_wa, lru_ba, lru_wx, lru_bx, lru_lambda, w_out_odd, ffn_wg, ffn_wu, ffn_wd, final_norm):
    bp = x_prompt.shape[0]
    bs, ts, d = x_sample.shape
    w_in_even_bf = w_in_even[0].astype(bf16)
    weights = {
        "norm_mix": norm_mix, "norm_ffn": norm_ffn,
        "w_in_even": w_in_even_bf, "w_out_even": w_out_even[0].astype(bf16),
        "w_kv_t": w_in_even_bf[:, SEG:3 * SEG].T.reshape(2, SEG, d),
        "hg_gnorm": hg_gnorm[0], "hg_lb_logits": hg_lb_logits,
        "w_in_odd": w_in_odd[0].astype(bf16), "conv_w": conv_w[0], "conv_b": conv_b[0],
        "lru_wax": jnp.concatenate([lru_wa[0], lru_wx[0]], axis=-1).astype(bf16),
        "lru_ba": lru_ba[0], "lru_bx": lru_bx[0], "lru_lambda": lru_lambda[0],
        "w_out_odd": w_out_odd[0].astype(bf16),
        "ffn_wg": [w.astype(bf16) for w in ffn_wg], "ffn_wu": [w.astype(bf16) for w in ffn_wu],
        "ffn_wd": [w.astype(bf16) for w in ffn_wd],
        "final_norm": final_norm,
    }
    mods = _ada(jnp.concatenate([c_prompt, c_sample], axis=0), w_ada, b_ada)
    mods = mods.reshape(DEPTH, bp + bs, N_MOD, d)

    out_p = _trunk(x_prompt, mods[:, :bp], None, weights, nb=1, tt=512, hg_tt=512)
    cache_t = lambda c: c[0].transpose(0, 2, 3, 1).reshape(bs, SB_WIDTH, -1)
    past = (cache_t(cache_sb_k), cache_t(cache_sb_v),
            state_hgrn[0], state_conv[0], state_lru[0].reshape(bs, 1, d))
    out_s = _trunk(x_sample, mods[:, bp:], past, weights, nb=bs, tt=ts, hg_tt=ts)

    y_p, k_p, v_p, s_p, conv_p, h_p = out_p
    y_s, k_s, v_s, s_s, conv_s, h_s = out_s
    return (y_p, y_s, k_p, v_p, s_p, conv_p, h_p, k_s, v_s, s_s, conv_s, h_s)
```

```python
import functools
import math

import jax
import jax.numpy as jnp
from jax import lax
from jax.experimental import pallas as pl
from jax.experimental.pallas import tpu as pltpu

f32 = jnp.float32
bf16 = jnp.bfloat16

D_MODEL = 1024
DEPTH = 2
EPS = 1e-6
N_MOD = 6

SB_HEADS = 8
SB_DIM = 64
SB_WIDTH = SB_HEADS * SB_DIM
HG_HEADS = 4
HG_DK = 128
HG_DV = 128
HG_W = HG_HEADS * HG_DK
SEG = 512
N_SEG = 7
HG_CHUNK = 64
HG_SUB = 16
HG_MATMUL_MIN_LOG = -40.0
LRU_BLOCKS = 8
LRU_BLK = D_MODEL // LRU_BLOCKS
CONV_W = 4
C_SCALE = 8.0
D_FF = 2816

SB_WINDOW = 2
SB_QBLOCKS = 2
SB_SKIP_LOG = -105.0

MXU_COLS = 256

LANES = 128
SUBLANES = 8
HALO = 8
VMEM_LIMIT = 60 * 1024 * 1024

NT_DIMS = (((1,), (1,)), ((), ()))
TN_DIMS = (((0,), (0,)), ((), ()))


def _dot(a, b):
    return jnp.dot(a, b, preferred_element_type=f32)


def _dot_nt(a, b):
    return lax.dot_general(a, b, NT_DIMS, preferred_element_type=f32)


def _split3(x):
    hi = x.astype(bf16)
    r1 = x - hi.astype(f32)
    mid = r1.astype(bf16)
    lo = (r1 - mid.astype(f32)).astype(bf16)
    return hi, mid, lo


def _split2(x):
    hi = x.astype(bf16)
    return hi, (x - hi.astype(f32)).astype(bf16)


def _sigmoid(x):
    return jax.nn.sigmoid(x)


def _silu(x):
    return x * _sigmoid(x)


def _norm_mod(x, g, scale, shift):
    ms = jnp.mean(x * x, axis=-1, keepdims=True)
    return x * lax.rsqrt(ms + EPS) * (g * (1.0 + scale)) + shift


def _params(*sem):
    return pltpu.CompilerParams(dimension_semantics=sem, vmem_limit_bytes=VMEM_LIMIT)


def _ada_kernel(c_ref, w_ref, b_ref, o_ref):
    c = c_ref[...]
    sc = _silu(c)
    w = w_ref[...]
    s_hi = sc.astype(bf16)
    s_lo = (sc - s_hi.astype(f32)).astype(bf16)
    w_hi = w.astype(bf16)
    w_lo = (w - w_hi.astype(f32)).astype(bf16)
    o_ref[...] = _dot(s_hi, w_hi) + _dot(s_lo, w_hi) + _dot(s_hi, w_lo) + b_ref[...]


def _ada(c_all, w_ada, b_ada):
    rows = c_all.shape[0]
    n_out = N_MOD * D_MODEL
    tn = 1536
    return pl.pallas_call(
        _ada_kernel,
        grid=(DEPTH, n_out // tn),
        in_specs=[
            pl.BlockSpec((rows, D_MODEL), lambda l, j: (0, 0)),
            pl.BlockSpec((None, D_MODEL, tn), lambda l, j: (l, 0, j)),
            pl.BlockSpec((None, 1, tn), lambda l, j: (l, 0, j)),
        ],
        out_specs=pl.BlockSpec((None, rows, tn), lambda l, j: (l, 0, j)),
        out_shape=jax.ShapeDtypeStruct((DEPTH, rows, n_out), f32),
        compiler_params=_params("arbitrary", "arbitrary"),
        name="ada_mod",
    )(c_all, w_ada, b_ada.reshape(DEPTH, 1, n_out))


def _even_pre_kernel(layer, kv_transposed, x_ref, mod_ref, g_ref, w_ref, wkvt_ref, lbl_ref,
                     qa_ref, ka_ref, va_ref, kab_ref, vab_ref,
                     qb_ref, lf_ref, kb_ref, ib_ref, gs_ref):
    nb, tt, d = x_ref.shape
    m = nb * tt
    hn = _norm_mod(x_ref[...], g_ref[...], mod_ref[:, 1:2, :], mod_ref[:, 0:1, :])
    hn = hn.reshape(m, d).astype(bf16)

    def seg(i):
        return _dot(hn, w_ref[:, i * SEG:(i + 1) * SEG]).reshape(nb, tt, SEG)

    qa_ref[...] = (seg(0) * (1.0 / math.sqrt(SB_DIM))).astype(bf16)
    if kv_transposed:
        for i, (full_ref, blocks_ref) in enumerate(((ka_ref, kab_ref), (va_ref, vab_ref))):
            kt = _dot_nt(wkvt_ref[i], hn)
            full_ref[...] = kt
            for c in range(m // LANES):
                blocks_ref[c] = kt[:, c * LANES:(c + 1) * LANES].astype(bf16)
    else:
        ka = seg(1)
        ka_ref[...] = ka
        kab_ref[...] = ka.astype(bf16)
        va = seg(2)
        va_ref[...] = va
        vab_ref[...] = va.astype(bf16)
    qb_ref[...] = _silu(seg(3)) * (HG_DK ** -0.5)

    lg = lbl_ref[...]
    e = jnp.exp(lg - jnp.max(lg, axis=0, keepdims=True))
    lb = jnp.sum(e[:layer + 1], axis=0, keepdims=True) / jnp.sum(e, axis=0, keepdims=True)
    fb = seg(4)
    f = lb + (1.0 - lb) * _sigmoid(fb)
    lf_ref[...] = jnp.log(f)
    kb_ref[...] = (1.0 - lb) * _sigmoid(-fb)
    ib_ref[...] = seg(5)
    gs_ref[...] = _silu(seg(6))


def _even_pre(x, mod, g, w_bf, wkvt_bf, lb_logits, layer, nb, tt, kv_transposed):
    b, t, d = x.shape
    grid = (b // nb, t // tt)
    blk = lambda width: pl.BlockSpec((nb, tt, width), lambda i, j: (i, j, 0))
    out_f = jax.ShapeDtypeStruct((b, t, SEG), f32)
    out_b = jax.ShapeDtypeStruct((b, t, SEG), bf16)
    if kv_transposed:
        assert nb == 1 and tt % LANES == 0
        kv_f, kv_b = jax.ShapeDtypeStruct((b, SEG, t), f32), jax.ShapeDtypeStruct((b, t // LANES, SEG, LANES), bf16)
        kv_f_spec = pl.BlockSpec((None, SEG, tt), lambda i, j: (i, 0, j))
        kv_b_spec = pl.BlockSpec((None, tt // LANES, SEG, LANES), lambda i, j: (i, j, 0, 0))
    else:
        kv_f, kv_b, kv_f_spec, kv_b_spec = out_f, out_b, blk(SEG), blk(SEG)
    return pl.pallas_call(
        functools.partial(_even_pre_kernel, layer, kv_transposed),
        grid=grid,
        in_specs=[
            blk(d),
            pl.BlockSpec((nb, N_MOD, d), lambda i, j: (i, 0, 0)),
            _resident((1, d)),
            _resident((d, N_SEG * SEG)),
            _resident((2, SEG, d)),
            _resident((DEPTH + 1, HG_W)),
        ],
        out_specs=[blk(SEG), kv_f_spec, kv_f_spec, kv_b_spec, kv_b_spec] + [blk(SEG)] * 5,
        out_shape=[out_b, kv_f, kv_f, kv_b, kv_b, out_f, out_f, out_f, out_f, out_f],
        compiler_params=_params("arbitrary", "arbitrary"),
        name="even_pre",
    )(x, mod, g, w_bf, wkvt_bf, lb_logits)


def _sb_consts(tk, scale=1.0):
    r = lax.broadcasted_iota(jnp.int32, (tk, LANES + tk), 0)
    c = lax.broadcasted_iota(jnp.int32, (tk, LANES + tk), 1)
    m = jnp.where((c < LANES) | (r > c - LANES), scale, 0.0).astype(bf16)
    return jnp.concatenate([m, m], axis=0)


def _sb_groups(groups):
    zs = [[_dot(q2, k_blk) if transposed else _dot_nt(q2, k_blk)
           for k_blk, _, _, _, transposed in blocks] for q2, blocks, _, _ in groups]
    l1mbs, sums = [], []
    for g, (_, blocks, _, _) in enumerate(groups):
        l1mbs.append([])
        sums.append([])
        for z, (_, _, sum_matrix, mask, _) in zip(zs[g], blocks):
            l1mb = -(jnp.maximum(z, 0.0) + jnp.log(1.0 + jnp.exp(-jnp.abs(z))))
            if mask is not None:
                l1mb = jnp.where(mask, l1mb, 0.0)
            l1mbs[g].append(l1mb)
            sums[g].append(_dot(jnp.concatenate(_split2(l1mb), axis=1), sum_matrix))
    results = []
    for g, (_, blocks, carry, acc) in enumerate(groups):
        for z, l1mb, s, (_, v_blk, _, mask, transposed) in zip(zs[g], l1mbs[g], sums[g], blocks):
            tk = z.shape[1]
            after = carry[:, :tk] + s[:, LANES:]
            w = jnp.exp(z + l1mb + after)
            if mask is not None:
                w = jnp.where(mask, w, 0.0)
            w = w.astype(bf16)
            acc = acc + (_dot_nt(w, v_blk) if transposed else _dot(w, v_blk))
            carry = carry + s[:, :LANES]
        results.append((carry, acc))
    return results


def _sb_blocks(q2, blocks, carry, acc):
    return _sb_groups([(q2, blocks, carry, acc)])[0]


def _sb_finish(j0, carry, acc, step):
    def cond(state):
        j, carry, _ = state
        return (j >= 0) & (jnp.max(carry) > SB_SKIP_LOG)

    def body(state):
        j, carry, acc = state
        carry, acc = step(j, carry, acc)
        return j - 1, carry, acc

    return lax.while_loop(cond, body, (j0, carry, acc))[2]


def _stack_heads(q):
    lane = lax.broadcasted_iota(jnp.int32, q.shape, 1)
    zero = jnp.zeros_like(q)
    return jnp.concatenate([jnp.where(lane < SB_DIM, q, zero), jnp.where(lane >= SB_DIM, q, zero)], axis=0)


def _unstack_heads(acc2):
    tq = acc2.shape[0] // 2
    lane = lax.broadcasted_iota(jnp.int32, (tq, LANES), 1)
    return jnp.where(lane < SB_DIM, acc2[:tq], acc2[tq:])


def _causal_mask2(tq):
    r = lax.broadcasted_iota(jnp.int32, (2 * tq, tq), 0) & (tq - 1)
    c = lax.broadcasted_iota(jnp.int32, (2 * tq, tq), 1)
    return c < r


def _sb_prompt_kernel(q_ref, k_ref, v_ref, o_ref):
    tq = k_ref.shape[2]
    n_q = q_ref.shape[0] // tq
    i0 = pl.program_id(2) * n_q
    sum_matrix = _sb_consts(tq)
    diag_mask = _causal_mask2(tq)
    zero = jnp.zeros((2 * tq, LANES), f32)

    groups = []
    for g in range(n_q):
        i = i0 + g
        q2 = _stack_heads(q_ref[g * tq:(g + 1) * tq, :])
        blocks = [(k_ref[i], v_ref[i], sum_matrix, diag_mask, True)]
        for c in range(1, SB_WINDOW + 1):
            scale = jnp.where(i >= c, 1.0, 0.0)
            j = jnp.maximum(i - c, 0)
            blocks.append((k_ref[j], (v_ref[j].astype(f32) * scale).astype(bf16), _sb_consts(tq, scale),
                           None, True))
        groups.append((q2, blocks, zero, zero))

    for g, (carry, acc) in enumerate(_sb_groups(groups)):
        q2 = groups[g][0]

        def step(j, carry, acc, q2=q2):
            return _sb_blocks(q2, [(k_ref[j], v_ref[j], sum_matrix, None, True)], carry, acc)

        acc = _sb_finish(i0 + g - SB_WINDOW - 1, carry, acc, step)
        o_ref[g * tq:(g + 1) * tq, :] = _unstack_heads(acc).astype(o_ref.dtype)


def _sb_prompt(q, k_blocks, v_blocks):
    b, t, _ = q.shape
    tk = LANES
    tq = SB_QBLOCKS * tk
    n_pair = SB_WIDTH // LANES
    qspec = pl.BlockSpec((None, tq, LANES), lambda bi, p, i: (bi, i, p))
    kvspec = pl.BlockSpec((None, t // tk, LANES, tk), lambda bi, p, i: (bi, 0, p, 0))
    return pl.pallas_call(
        _sb_prompt_kernel,
        grid=(b, n_pair, t // tq),
        in_specs=[qspec, kvspec, kvspec],
        out_specs=qspec,
        out_shape=jax.ShapeDtypeStruct((b, t, SB_WIDTH), bf16),
        compiler_params=_params("arbitrary", "arbitrary", "arbitrary"),
        name="sb_prompt",
    )(q, k_blocks, v_blocks)


def _sb_sample_kernel(tk, n_past, q_ref, k_ref, v_ref, wk_ref, wv_ref, pk_hbm, pv_hbm, o_ref,
                      kbuf, vbuf, sem):
    tq = q_ref.shape[0]
    b = pl.program_id(0)
    zero = jnp.zeros((2 * tq, LANES), f32)
    sum_matrix = _sb_consts(tk)
    diag_matrix = _sb_consts(tq)
    diag_mask = _causal_mask2(tq)

    def fetch(j):
        keys = pl.ds(pl.multiple_of(j * tk, tk), tk)
        copies = [pltpu.make_async_copy(pk_hbm.at[b, :, keys], kbuf, sem.at[0]),
                  pltpu.make_async_copy(pv_hbm.at[b, :, keys], vbuf, sem.at[1])]
        for cp in copies:
            cp.start()
        for cp in copies:
            cp.wait()

    groups = []
    for pair in range(SB_WIDTH // LANES):
        cols = slice(pair * LANES, (pair + 1) * LANES)
        q2 = _stack_heads(q_ref[:, cols])
        blocks = [(k_ref[:, cols], v_ref[:, cols], diag_matrix, diag_mask, False)]
        for c in range(1, SB_WINDOW + 1):
            keys = slice((SB_WINDOW - c) * tk, (SB_WINDOW - c + 1) * tk)
            blocks.append((wk_ref[cols, keys].astype(bf16), wv_ref[cols, keys].astype(bf16),
                           sum_matrix, None, True))
        groups.append((q2, blocks, zero, zero))

    for pair, (carry, acc) in enumerate(_sb_groups(groups)):
        cols = slice(pair * LANES, (pair + 1) * LANES)
        q2 = groups[pair][0]

        def step(j, carry, acc, cols=cols, q2=q2):
            fetch(j)
            blk = (kbuf[cols, :].astype(bf16), vbuf[cols, :].astype(bf16), sum_matrix, None, True)
            return _sb_blocks(q2, [blk], carry, acc)

        acc = _sb_finish(n_past - SB_WINDOW - 1, carry, acc, step)
        o_ref[:, cols] = _unstack_heads(acc).astype(o_ref.dtype)


def _sb_sample(q, k, v, past_kt, past_vt, tk):
    b, t, _ = q.shape
    n_past = past_kt.shape[2] // tk
    assert n_past % SB_WINDOW == 0 and n_past >= SB_WINDOW
    qspec = pl.BlockSpec((None, t, SB_WIDTH), lambda bi: (bi, 0, 0))
    wspec = pl.BlockSpec((None, SB_WIDTH, SB_WINDOW * tk), lambda bi: (bi, 0, n_past // SB_WINDOW - 1))
    hbm = pl.BlockSpec(memory_space=pl.ANY)
    return pl.pallas_call(
        functools.partial(_sb_sample_kernel, tk, n_past),
        grid=(b,),
        in_specs=[qspec, qspec, qspec, wspec, wspec, hbm, hbm],
        out_specs=qspec,
        out_shape=jax.ShapeDtypeStruct((b, t, SB_WIDTH), bf16),
        scratch_shapes=[pltpu.VMEM((SB_WIDTH, tk), f32), pltpu.VMEM((SB_WIDTH, tk), f32),
                        pltpu.SemaphoreType.DMA((2,))],
        compiler_params=_params("arbitrary"),
        name="sb_sample",
    )(q, k, v, past_kt, past_vt, past_kt, past_vt)


def _hgrn_head_chunk(q, k, v, b, st):
    n_sub = HG_CHUNK // HG_SUB
    v_bf = v.astype(bf16)
    o_inter = _dot_nt((q * jnp.exp(b)).astype(bf16), st.astype(bf16))
    t_idx = lax.broadcasted_iota(jnp.int32, (HG_SUB, HG_SUB, HG_DK), 0)
    s_idx = lax.broadcasted_iota(jnp.int32, (HG_SUB, HG_SUB, HG_DK), 1)
    causal = s_idx <= t_idx
    outs = []
    for i in range(n_sub):
        lo = i * HG_SUB
        rows = slice(lo, lo + HG_SUB)
        qi, ki, bi = q[rows], k[rows], b[rows]
        diff = bi[:, None, :] - bi[None, :, :]
        dec = jnp.exp(jnp.where(causal, diff, -1e30))
        a_diag = jnp.sum(qi[:, None, :] * ki[None, :, :] * dec, axis=-1)
        o_i = _dot(a_diag.astype(bf16), v_bf[rows])
        if i > 0:
            ref = b[lo - 1:lo]
            q_t = qi * jnp.exp(bi - ref)
            k_t = k[:lo] * jnp.exp(ref - b[:lo])
            a_off = _dot_nt(q_t.astype(bf16), k_t.astype(bf16))
            o_i = o_i + _dot(a_off.astype(bf16), v_bf[:lo])
        outs.append(o_inter[rows] + o_i)
    o = jnp.concatenate(outs, axis=0)
    b_last = b[HG_CHUNK - 1:HG_CHUNK]
    k_dec = (k * jnp.exp(b_last - b)).astype(bf16)
    st_new = jnp.exp(b_last) * st + lax.dot_general(v_bf, k_dec, TN_DIMS, preferred_element_type=f32)
    return o, st_new


def _hgrn_chunk_matmul_units(qs, ks, vs, bs, sts, out):
    n_sub = HG_CHUNK // HG_SUB
    heads = range(len(qs))
    q_in, q_t, k_t, k_dec, v_bf, b_last = [], [], [], [], [], []
    for h in heads:
        q, k, b = qs[h], ks[h], bs[h]
        q_in.append((q * jnp.exp(b)).astype(bf16))
        q_t.append([])
        k_t.append([])
        for i in range(n_sub):
            lo, hi = i * HG_SUB, (i + 1) * HG_SUB
            if i == 0:
                q_t[h].append(q_in[h][:hi])
                k_t[h].append((k[:hi] * jnp.exp(-b[:hi])).astype(bf16))
            else:
                ref = b[lo - 1:lo]
                q_t[h].append((q[lo:hi] * jnp.exp(b[lo:hi] - ref)).astype(bf16))
                k_t[h].append((k[:hi] * jnp.exp(ref - b[:hi])).astype(bf16))
        b_last.append(b[HG_CHUNK - 1:HG_CHUNK])
        k_dec.append((k * jnp.exp(b_last[h] - b)).astype(bf16))
        v_bf.append(vs[h].astype(bf16))
    yield

    o_inter = [_dot_nt(q_in[h], sts[h].astype(bf16)) for h in heads]
    attn = [[_dot_nt(q_t[h][i], k_t[h][i]) for i in range(n_sub)] for h in heads]
    st_add = [lax.dot_general(v_bf[h], k_dec[h], TN_DIMS, preferred_element_type=f32) for h in heads]
    yield

    outs, new_sts = [], []
    for h in heads:
        pieces = []
        for i in range(n_sub):
            hi = (i + 1) * HG_SUB
            r = lax.broadcasted_iota(jnp.int32, (HG_SUB, hi), 0)
            c = lax.broadcasted_iota(jnp.int32, (HG_SUB, hi), 1)
            a = jnp.where(c - i * HG_SUB <= r, attn[h][i], 0.0)
            pieces.append(_dot(a.astype(bf16), v_bf[h][:hi]))
        outs.append(jnp.concatenate(pieces, axis=0) + o_inter[h])
        new_sts.append(jnp.exp(b_last[h]) * sts[h] + st_add[h])
    out.update(o=outs, st=new_sts)


def _hgrn_chunk_matmul(qs, ks, vs, bs, sts):
    out = {}
    for _ in _hgrn_chunk_matmul_units(qs, ks, vs, bs, sts, out):
        pass
    return out["o"], out["st"]


def _hgrn_kernel(q_ref, lf_ref, k_ref, v_ref, gs_ref, gn_ref, s0_ref, o_ref, sout_ref, st_scr):
    tb = pl.program_id(1)
    n_tb = pl.num_programs(1)
    n_chunk = q_ref.shape[0] // HG_CHUNK
    n_sub = HG_CHUNK // HG_SUB

    @pl.when(tb == 0)
    def _():
        for h in range(HG_HEADS):
            st_scr[h] = s0_ref[h].T

    r = lax.broadcasted_iota(jnp.int32, (HG_CHUNK, HG_CHUNK), 0)
    c = lax.broadcasted_iota(jnp.int32, (HG_CHUNK, HG_CHUNK), 1)
    lower = jnp.where(c <= r, 1.0, 0.0).astype(bf16)
    head_cols = [slice(h * HG_DK, (h + 1) * HG_DK) for h in range(HG_HEADS)]

    def chunk(ci, carry):
        rows = pl.ds(pl.multiple_of(ci * HG_CHUNK, HG_CHUNK), HG_CHUNK)
        b_all = _dot_exact_rhs_lhs(lower, lf_ref[rows, :])

        def finish(h, o, st_new):
            cols = head_cols[h]
            st_scr[h] = st_new
            o = o * lax.rsqrt(jnp.mean(o * o, axis=-1, keepdims=True) + EPS)
            o_ref[rows, cols] = (o * gn_ref[:, cols] * gs_ref[rows, cols]).astype(o_ref.dtype)

        ends = [b_all[(i + 1) * HG_SUB - 1:(i + 1) * HG_SUB, :] for i in range(n_sub)]
        span = ends[0]
        for i in range(1, n_sub):
            span = jnp.minimum(span, ends[i] - ends[i - 1])
        bounded = jnp.min(span) >= HG_MATMUL_MIN_LOG

        @pl.when(bounded)
        def _():
            outs, new_sts = _hgrn_chunk_matmul(
                [q_ref[rows, cols] for cols in head_cols], [k_ref[rows, cols] for cols in head_cols],
                [v_ref[rows, cols] for cols in head_cols], [b_all[:, cols] for cols in head_cols],
                [st_scr[h] for h in range(HG_HEADS)])
            for h in range(HG_HEADS):
                finish(h, outs[h], new_sts[h])

        @pl.when(jnp.logical_not(bounded))
        def _():
            for h, cols in enumerate(head_cols):
                o, st_new = _hgrn_head_chunk(q_ref[rows, cols], k_ref[rows, cols], v_ref[rows, cols],
                                             b_all[:, cols], st_scr[h])
                finish(h, o, st_new)

        return carry

    lax.fori_loop(0, n_chunk, chunk, 0)

    @pl.when(tb == n_tb - 1)
    def _():
        for h in range(HG_HEADS):
            sout_ref[h] = st_scr[h].T


def _dot_exact_rhs_lhs(m, x):
    hi, mid, lo = _split3(x)
    return _dot(m, hi) + _dot(m, mid) + _dot(m, lo)


def _hgrn(qb, lf, kb, ib, gs, gnorm, s0, tt):
    b, t, _ = qb.shape
    blk = pl.BlockSpec((None, tt, HG_W), lambda i, j: (i, j, 0))
    sspec = pl.BlockSpec((None, HG_HEADS, HG_DK, HG_DV), lambda i, j: (i, 0, 0, 0))
    return pl.pallas_call(
        _hgrn_kernel,
        grid=(b, t // tt),
        in_specs=[blk, blk, blk, blk, blk, pl.BlockSpec((1, HG_W), lambda i, j: (0, 0)), sspec],
        out_specs=[blk, sspec],
        out_shape=[jax.ShapeDtypeStruct((b, t, HG_W), bf16),
                   jax.ShapeDtypeStruct((b, HG_HEADS, HG_DK, HG_DV), f32)],
        scratch_shapes=[pltpu.VMEM((HG_HEADS, HG_DV, HG_DK), f32)],
        compiler_params=_params("arbitrary", "arbitrary"),
        name="hgrn",
    )(qb, lf, kb, ib, gs, gnorm, s0)


def _round_robin(*gens, units=None):
    turns = list(zip(gens, units or [1] * len(gens)))
    while turns:
        turns = [(gen, n) for gen, n in turns if all(next(gen, "done") != "done" for _ in range(n))]


def _even_fused_kernel(layer, n_tb, x_ref, mod_ref, g_ref, w_ref, wkvt_ref, lbl_ref, gn_ref, s0_ref,
                       qa_ref, ka_ref, va_ref, kab_ref, vab_ref, ob_ref, sout_ref,
                       q_scr, b_scr, k_scr, v_scr, gs_scr, st_scr):
    _, tt, d = x_ref.shape
    s = pl.program_id(0)
    hg_blk = jnp.maximum(s - 1, 0)
    tb_h = lax.rem(hg_blk, n_tb)
    n_chunk = tt // HG_CHUNK
    n_sub = HG_CHUNK // HG_SUB
    head_cols = [slice(h * HG_DK, (h + 1) * HG_DK) for h in range(HG_HEADS)]
    operand_scr = (q_scr, b_scr, k_scr, v_scr, gs_scr)

    @pl.when(s == 0)
    def _():
        for scr in operand_scr:
            scr[...] = jnp.zeros_like(scr)

    @pl.when(tb_h == 0)
    def _():
        for h in range(HG_HEADS):
            st_scr[h] = s0_ref[h].T

    r = lax.broadcasted_iota(jnp.int32, (HG_CHUNK, HG_CHUNK), 0)
    c = lax.broadcasted_iota(jnp.int32, (HG_CHUNK, HG_CHUNK), 1)
    lower = jnp.where(c <= r, 1.0, 0.0).astype(bf16)

    def proj_units(out):
        hn = _norm_mod(x_ref[...], g_ref[...], mod_ref[:, 1:2, :], mod_ref[:, 0:1, :])
        hn = hn.reshape(tt, d).astype(bf16)
        yield

        def seg(i):
            return _dot(hn, w_ref[:, i * SEG:(i + 1) * SEG])

        qa_ref[...] = (seg(0) * (1.0 / math.sqrt(SB_DIM))).astype(bf16).reshape(1, tt, SEG)
        yield
        for i, (full_ref, blocks_ref) in enumerate(((ka_ref, kab_ref), (va_ref, vab_ref))):
            kt = _dot_nt(wkvt_ref[i], hn)
            full_ref[...] = kt
            for cb in range(tt // LANES):
                blocks_ref[cb] = kt[:, cb * LANES:(cb + 1) * LANES].astype(bf16)
            yield
        q = _silu(seg(3)) * (HG_DK ** -0.5)
        yield
        lg = lbl_ref[...]
        e = jnp.exp(lg - jnp.max(lg, axis=0, keepdims=True))
        lb = jnp.sum(e[:layer + 1], axis=0, keepdims=True) / jnp.sum(e, axis=0, keepdims=True)
        fb = seg(4)
        lf = jnp.log(lb + (1.0 - lb) * _sigmoid(fb))
        k = (1.0 - lb) * _sigmoid(-fb)
        yield
        b = jnp.concatenate([_dot_exact_rhs_lhs(lower, lf[ci * HG_CHUNK:(ci + 1) * HG_CHUNK])
                             for ci in range(n_chunk)], axis=0)
        yield
        v = seg(5)
        yield
        gs = _silu(seg(6))
        out.update(q=q, b=b, k=k, v=v, gs=gs)

    def store_operands(out):
        for scr, name in zip(operand_scr, ("q", "b", "k", "v", "gs")):
            scr[...] = out[name]

    def finish(rows, h, o):
        cols = head_cols[h]
        o = o * lax.rsqrt(jnp.mean(o * o, axis=-1, keepdims=True) + EPS)
        ob_ref[0, rows, cols] = (o * gn_ref[:, cols] * gs_scr[rows, cols]).astype(ob_ref.dtype)

    def hgrn_units():
        sts = [st_scr[h] for h in range(HG_HEADS)]
        for ci in range(n_chunk):
            rows = slice(ci * HG_CHUNK, (ci + 1) * HG_CHUNK)
            out = {}
            yield from _hgrn_chunk_matmul_units(
                [q_scr[rows, cols] for cols in head_cols], [k_scr[rows, cols] for cols in head_cols],
                [v_scr[rows, cols] for cols in head_cols], [b_scr[rows, cols] for cols in head_cols],
                sts, out)
            sts = out["st"]
            for h in range(HG_HEADS):
                finish(rows, h, out["o"][h])
            yield
        for h in range(HG_HEADS):
            st_scr[h] = sts[h]

    span = None
    for ci in range(n_chunk):
        prev = None
        for i in range(n_sub):
            row = ci * HG_CHUNK + (i + 1) * HG_SUB - 1
            end = b_scr[row:row + 1, :]
            this = end if prev is None else end - prev
            span = this if span is None else jnp.minimum(span, this)
            prev = end
    bounded = jnp.min(span) >= HG_MATMUL_MIN_LOG

    @pl.when(bounded)
    def _():
        res = {}
        _round_robin(hgrn_units(), proj_units(res), units=[3, 1])
        store_operands(res)

    @pl.when(jnp.logical_not(bounded))
    def _():
        def chunk(ci, carry):
            rows = pl.ds(pl.multiple_of(ci * HG_CHUNK, HG_CHUNK), HG_CHUNK)
            for h, cols in enumerate(head_cols):
                o, st_new = _hgrn_head_chunk(q_scr[rows, cols], k_scr[rows, cols], v_scr[rows, cols],
                                             b_scr[rows, cols], st_scr[h])
                st_scr[h] = st_new
                finish(rows, h, o)
            return carry

        lax.fori_loop(0, n_chunk, chunk, 0)
        res = {}
        _round_robin(proj_units(res))
        store_operands(res)

    @pl.when((tb_h == n_tb - 1) & (s > 0))
    def _():
        for h in range(HG_HEADS):
            sout_ref[h] = st_scr[h].T


def _even_fused(x, mod, g, w_bf, wkvt_bf, lb_logits, gnorm, s0, layer, tt):
    b, t, d = x.shape
    n_tb = t // tt
    n_blocks = b * n_tb
    proj_blk = lambda s: jnp.minimum(s, n_blocks - 1)
    hgrn_blk = lambda s: jnp.maximum(s - 1, 0)
    rows = lambda width, which: pl.BlockSpec((1, tt, width), lambda s: (which(s) // n_tb, which(s) % n_tb, 0))
    operand = pltpu.VMEM((tt, HG_W), f32)
    return pl.pallas_call(
        functools.partial(_even_fused_kernel, layer, n_tb),
        grid=(n_blocks + 1,),
        in_specs=[
            rows(d, proj_blk),
            pl.BlockSpec((1, N_MOD, d), lambda s: (proj_blk(s) // n_tb, 0, 0)),
            _resident((1, d)),
            _resident((d, N_SEG * SEG)),
            _resident((2, SEG, d)),
            _resident((DEPTH + 1, HG_W)),
            _resident((1, HG_W)),
            pl.BlockSpec((None, HG_HEADS, HG_DK, HG_DV), lambda s: (hgrn_blk(s) // n_tb, 0, 0, 0)),
        ],
        out_specs=[
            rows(SEG, proj_blk),
            pl.BlockSpec((None, SEG, tt), lambda s: (proj_blk(s) // n_tb, 0, proj_blk(s) % n_tb)),
            pl.BlockSpec((None, SEG, tt), lambda s: (proj_blk(s) // n_tb, 0, proj_blk(s) % n_tb)),
            pl.BlockSpec((None, tt // LANES, SEG, LANES), lambda s: (proj_blk(s) // n_tb, proj_blk(s) % n_tb, 0, 0)),
            pl.BlockSpec((None, tt // LANES, SEG, LANES), lambda s: (proj_blk(s) // n_tb, proj_blk(s) % n_tb, 0, 0)),
            rows(HG_W, hgrn_blk),
            pl.BlockSpec((None, HG_HEADS, HG_DK, HG_DV), lambda s: (hgrn_blk(s) // n_tb, 0, 0, 0)),
        ],
        out_shape=[
            jax.ShapeDtypeStruct((b, t, SEG), bf16),
            jax.ShapeDtypeStruct((b, SEG, t), f32), jax.ShapeDtypeStruct((b, SEG, t), f32),
            jax.ShapeDtypeStruct((b, t // LANES, SEG, LANES), bf16),
            jax.ShapeDtypeStruct((b, t // LANES, SEG, LANES), bf16),
            jax.ShapeDtypeStruct((b, t, HG_W), bf16),
            jax.ShapeDtypeStruct((b, HG_HEADS, HG_DK, HG_DV), f32),
        ],
        scratch_shapes=[operand] * 5 + [pltpu.VMEM((HG_HEADS, HG_DV, HG_DK), f32)],
        compiler_params=_params("arbitrary"),
        name="even_pre_hgrn",
    )(x, mod, g, w_bf, wkvt_bf, lb_logits, gnorm, s0)


def _ffn(hn_bf, wg_ref, wu_ref, wd_ref):
    g = _dot(hn_bf, wg_ref[...])
    u = _dot(hn_bf, wu_ref[...])
    return _dot((_silu(g) * u).astype(bf16), wd_ref[...])


def _resident(shape):
    return pl.BlockSpec(shape, lambda *_: (0,) * len(shape), pipeline_mode=pl.Buffered(1))


def _ffn_specs(layer):
    one = lambda rows, cols: pl.BlockSpec((None, rows, cols), lambda *_: (layer, 0, 0),
                                          pipeline_mode=pl.Buffered(1))
    return [one(D_MODEL, D_FF), one(D_MODEL, D_FF), one(D_FF, D_MODEL)]


def _even_post_kernel(x_ref, mod_ref, oa_ref, ob_ref, wo_ref, gf_ref, wg_ref, wu_ref, wd_ref, out_ref):
    nb, tt, d = x_ref.shape
    m = nb * tt
    o = (_dot(oa_ref[...].reshape(m, SB_WIDTH), wo_ref[:SB_WIDTH, :])
         + _dot(ob_ref[...].reshape(m, HG_W), wo_ref[SB_WIDTH:, :]))
    x1 = x_ref[...] + (1.0 + mod_ref[:, 2:3, :]) * o.reshape(nb, tt, d)
    hn = _norm_mod(x1, gf_ref[...], mod_ref[:, 4:5, :], mod_ref[:, 3:4, :])
    y = _ffn(hn.reshape(m, d).astype(bf16), wg_ref, wu_ref, wd_ref)
    out_ref[...] = x1 + (1.0 + mod_ref[:, 5:6, :]) * y.reshape(nb, tt, d)


def _even_post(x, mod, oa, ob, wo, gf, wg, wu, wd, layer, nb, tt):
    b, t, d = x.shape
    blk = lambda width: pl.BlockSpec((nb, tt, width), lambda i, tb: (i, tb, 0))
    return pl.pallas_call(
        _even_post_kernel,
        grid=(b // nb, t // tt),
        in_specs=[blk(d), pl.BlockSpec((nb, N_MOD, d), lambda i, tb: (i, 0, 0)),
                  blk(SB_WIDTH), blk(HG_W), _resident((d, d)), _resident((1, d))] + _ffn_specs(layer),
        out_specs=blk(d),
        out_shape=jax.ShapeDtypeStruct((b, t, d), f32),
        compiler_params=_params("arbitrary", "arbitrary"),
        name="even_post_ffn",
    )(x, mod, oa, ob, wo, gf, wg, wu, wd)


def _linear_scan(a, u, h_init, tt):
    m, d = a.shape
    n_group = m // SUBLANES
    a3 = a.reshape(n_group, SUBLANES, d)
    u3 = u.reshape(n_group, SUBLANES, d)
    sub = lax.broadcasted_iota(jnp.int32, (1, SUBLANES, d), 1)
    step = 1
    while step < SUBLANES:
        keep = sub >= step
        a_prev = jnp.where(keep, pltpu.roll(a3, step, 1), 1.0)
        u_prev = jnp.where(keep, pltpu.roll(u3, step, 1), 0.0)
        u3 = a3 * u_prev + u3
        a3 = a3 * a_prev
        step *= 2
    groups_per_seq = tt // SUBLANES
    carry = None
    groups = []
    for gi in range(n_group):
        if gi % groups_per_seq == 0:
            carry = h_init[gi // groups_per_seq]
        h_group = u3[gi] + a3[gi] * carry
        groups.append(h_group)
        carry = h_group[SUBLANES - 1:SUBLANES, :]
    return jnp.concatenate(groups, axis=0)


def _odd_kernel(stream_start, pipelined, n_tb, x_ref, mod_ref, mod_ffn_ref, gm_ref, win_ref, cw_ref, cb_ref,
                wax_ref, ba_ref, bx_ref, lam_ref, wo_ref, conv0_ref, h0_ref, gf_ref, fn_ref,
                wg_ref, wu_ref, wd_ref, y_ref, conv_ref, hout_ref, xs_scr, hprev_scr, x1_scr, hn_scr):
    nb, tt, d = x_ref.shape
    m = nb * tt
    s = pl.program_id(0)
    tb = lax.rem(s, n_tb)

    def ffn_units(x1, hn_bf, out):
        acts = []
        for j in range(D_FF // MXU_COLS):
            cols = slice(j * MXU_COLS, (j + 1) * MXU_COLS)
            g = _dot(hn_bf, wg_ref[:, cols])
            u = _dot(hn_bf, wu_ref[:, cols])
            acts.append((_silu(g) * u).astype(bf16))
            yield
        act = jnp.concatenate(acts, axis=1)
        parts = []
        for n in range(d // MXU_COLS):
            cols = slice(n * MXU_COLS, (n + 1) * MXU_COLS)
            y = _dot(act, wd_ref[:, cols]).reshape(nb, tt, MXU_COLS)
            parts.append(x1[:, :, cols] + (1.0 + mod_ffn_ref[:, 5:6, cols]) * y)
            yield
        x2 = jnp.concatenate(parts, axis=-1)
        ms = jnp.mean(x2 * x2, axis=-1, keepdims=True)
        out["y"] = x2 * lax.rsqrt(ms + EPS) * fn_ref[...]

    def mixer_units(out):
        x = x_ref[...]
        hn = _norm_mod(x, gm_ref[...], mod_ref[:, 1:2, :], mod_ref[:, 0:1, :])
        hn = hn.reshape(m, d).astype(bf16)
        yield
        gate = _dot(hn, win_ref[:, :d])
        yield
        xbr = _dot(hn, win_ref[:, d:]).reshape(nb, tt, d)
        yield
        xs_scr[:, HALO:HALO + tt, :] = xbr
        xc = cb_ref[...] + xbr * cw_ref[CONV_W - 1:CONV_W, :]
        for k in range(1, CONV_W):
            xc = xc + xs_scr[:, HALO - k:HALO - k + tt, :] * cw_ref[CONV_W - 1 - k:CONV_W - k, :]
        tail = xs_scr[:, HALO + tt - (CONV_W - 1):HALO + tt, :]
        xs_scr[:, HALO - (CONV_W - 1):HALO, :] = tail
        xc = xc.reshape(m, d)
        xc_bf = xc.astype(bf16)
        yield
        lam = lam_ref[...]
        log_sig_lam = -(jnp.maximum(-lam, 0.0) + jnp.log1p(jnp.exp(-jnp.abs(lam))))
        y_in, h_last = [], []
        for blk in range(LRU_BLOCKS):
            cols = slice(blk * LRU_BLK, (blk + 1) * LRU_BLK)
            both = _dot(xc_bf[:, cols], wax_ref[blk])
            r = _sigmoid(both[:, :LRU_BLK] + ba_ref[:, cols])
            gi = _sigmoid(both[:, LRU_BLK:] + bx_ref[:, cols])
            log_a = C_SCALE * r * log_sig_lam[:, cols]
            a = jnp.exp(log_a)
            mult = jnp.sqrt(-jnp.tanh(log_a) * (1.0 + a * a))
            if stream_start:
                row = (lax.broadcasted_iota(jnp.int32, (m, LRU_BLK), 0) & (tt - 1)) + tb * tt
                mult = jnp.where(row == 0, 1.0, mult)
            h = _linear_scan(a, mult * gi * xc[:, cols], hprev_scr[:, :, cols], tt)
            h_last.append(h.reshape(nb, tt, LRU_BLK)[:, tt - 1:tt, :])
            y_in.append((jax.nn.gelu(gate[:, cols]) * h).astype(bf16))
            yield
        h_last = jnp.concatenate(h_last, axis=-1)
        hprev_scr[...] = h_last
        y_in = jnp.concatenate(y_in, axis=1)
        o = _dot(y_in, wo_ref[...])
        yield
        x1 = x + (1.0 + mod_ref[:, 2:3, :]) * o.reshape(nb, tt, d)
        hn2 = _norm_mod(x1, gf_ref[...], mod_ref[:, 4:5, :], mod_ref[:, 3:4, :])
        out.update(x1=x1, hn2=hn2.reshape(m, d).astype(bf16), tail=tail, h_last=h_last)

    @pl.when(tb == 0)
    def _():
        xs_scr[:, HALO - (CONV_W - 1):HALO, :] = conv0_ref[...]
        hprev_scr[...] = h0_ref[...]

    res = {}
    if pipelined:
        @pl.when(s == 0)
        def _():
            x1_scr[...] = jnp.zeros_like(x1_scr)
            hn_scr[...] = jnp.zeros_like(hn_scr)

        units = [ffn_units(x1_scr[...], hn_scr[...], res), mixer_units(res)]
        while units:
            units = [gen for gen in units if next(gen, "done") != "done"]
        y_ref[...] = res["y"]
        x1_scr[...] = res["x1"]
        hn_scr[...] = res["hn2"]

        @pl.when(s < pl.num_programs(0) - 1)
        def _():
            conv_ref[...] = res["tail"]
            hout_ref[...] = res["h_last"]
    else:
        for _ in mixer_units(res):
            pass
        conv_ref[...] = res["tail"]
        hout_ref[...] = res["h_last"]
        for _ in ffn_units(res["x1"], res["hn2"], res):
            pass
        y_ref[...] = res["y"]


def _odd_layer(x, mod, gm, win, cw, cb, wax, ba, bx, lam, wo, conv0, h0, gf, fn, wg, wu, wd,
               layer, nb, tt, stream_start):
    b, t, d = x.shape
    n_tb = t // tt
    n_blocks = (b // nb) * n_tb
    pipelined = n_blocks > 1
    mix_blk = lambda s: jnp.minimum(s, n_blocks - 1)
    ffn_blk = (lambda s: jnp.maximum(s - 1, 0)) if pipelined else mix_blk
    rows_spec = lambda which: pl.BlockSpec((nb, tt, d), lambda s: (which(s) // n_tb, which(s) % n_tb, 0))
    per_seq = lambda rows, which: pl.BlockSpec((nb, rows, d), lambda s: (which(s) // n_tb, 0, 0))
    row = _resident((1, d))
    return pl.pallas_call(
        functools.partial(_odd_kernel, stream_start, pipelined, n_tb),
        grid=(n_blocks + (1 if pipelined else 0),),
        in_specs=[rows_spec(mix_blk), per_seq(N_MOD, mix_blk), per_seq(N_MOD, ffn_blk), row,
                  _resident((d, 2 * d)), _resident((CONV_W, d)),
                  row, _resident((LRU_BLOCKS, LRU_BLK, 2 * LRU_BLK)), row,
                  row, row, _resident((d, d)), per_seq(CONV_W - 1, mix_blk), per_seq(1, mix_blk),
                  row, row] + _ffn_specs(layer),
        out_specs=[rows_spec(ffn_blk), per_seq(CONV_W - 1, mix_blk), per_seq(1, mix_blk)],
        out_shape=[jax.ShapeDtypeStruct((b, t, d), f32),
                   jax.ShapeDtypeStruct((b, CONV_W - 1, d), f32),
                   jax.ShapeDtypeStruct((b, 1, d), f32)],
        scratch_shapes=[pltpu.VMEM((nb, HALO + tt, d), f32), pltpu.VMEM((nb, 1, d), f32),
                        pltpu.VMEM((nb, tt, d) if pipelined else (1, SUBLANES, LANES), f32),
                        pltpu.VMEM((nb * tt, d) if pipelined else (2 * SUBLANES, LANES), bf16)],
        compiler_params=_params("arbitrary"),
        name="odd_layer_ffn",
    )(x, mod, mod, gm, win, cw, cb, wax, ba, bx, lam, wo, conv0, h0, gf, fn, wg, wu, wd)


def _trunk(x, mods, past, weights, nb, tt, hg_tt):
    b, t, d = x.shape
    sample = past is not None
    row = lambda v: v.reshape(1, -1)

    if sample:
        past_k, past_v, s0, conv0, h0 = past
        qa, ka, va, ka_bf, va_bf, qb, lf, kb, ib, gs = _even_pre(
            x, mods[0], row(weights["norm_mix"][0]), weights["w_in_even"], weights["w_kv_t"],
            weights["hg_lb_logits"], 0, nb, tt, False)
        oa = _sb_sample(qa, ka_bf, va_bf, past_k, past_v, LANES)
        ob, s_new = _hgrn(qb, lf, kb, ib, gs, row(weights["hg_gnorm"]), s0, hg_tt)
        heads = lambda a: a.reshape(1, b, t, SB_HEADS, SB_DIM)
    else:
        s0 = jnp.zeros((b, HG_HEADS, HG_DK, HG_DV), f32)
        conv0 = jnp.zeros((b, CONV_W - 1, d), f32)
        h0 = jnp.zeros((b, 1, d), f32)
        qa, ka, va, ka_bf, va_bf, ob, s_new = _even_fused(
            x, mods[0], row(weights["norm_mix"][0]), weights["w_in_even"], weights["w_kv_t"],
            weights["hg_lb_logits"], row(weights["hg_gnorm"]), s0, 0, tt)
        oa = _sb_prompt(qa, ka_bf, va_bf)
        heads = lambda a: a.reshape(b, SB_HEADS, SB_DIM, t).transpose(0, 3, 1, 2)[None]
    x = _even_post(x, mods[0], oa, ob, weights["w_out_even"], row(weights["norm_ffn"][0]),
                   weights["ffn_wg"], weights["ffn_wu"], weights["ffn_wd"], 0, nb, tt)

    y, conv_new, h_new = _odd_layer(
        x, mods[1], row(weights["norm_mix"][1]), weights["w_in_odd"], weights["conv_w"],
        row(weights["conv_b"]), weights["lru_wax"], row(weights["lru_ba"]), row(weights["lru_bx"]),
        row(weights["lru_lambda"]), weights["w_out_odd"], conv0, h0,
        row(weights["norm_ffn"][1]), row(weights["final_norm"]),
        weights["ffn_wg"], weights["ffn_wu"], weights["ffn_wd"],
        1, nb, tt, not sample)

    return (y, heads(ka), heads(va), s_new[None], conv_new[None], h_new.reshape(1, b, d))


def kernel(x_prompt, x_sample, cache_sb_k, cache_sb_v, state_hgrn, state_conv, state_lru, c_prompt, c_sample, norm_mix, norm_ffn, w_ada, b_ada, w_in_even, w_out_even, hg_gnorm, hg_lb_logits, w_in_odd, conv_w, conv_b, lru_wa, lru_ba, lru_wx, lru_bx, lru_lambda, w_out_odd, ffn_wg, ffn_wu, ffn_wd, final_norm):
    bp = x_prompt.shape[0]
    bs, ts, d = x_sample.shape
    w_in_even_bf = w_in_even[0].astype(bf16)
    weights = {
        "norm_mix": norm_mix, "norm_ffn": norm_ffn,
        "w_in_even": w_in_even_bf, "w_out_even": w_out_even[0].astype(bf16),
        "w_kv_t": w_in_even_bf[:, SEG:3 * SEG].T.reshape(2, SEG, d),
        "hg_gnorm": hg_gnorm[0], "hg_lb_logits": hg_lb_logits,
        "w_in_odd": w_in_odd[0].astype(bf16), "conv_w": conv_w[0], "conv_b": conv_b[0],
        "lru_wax": jnp.concatenate([lru_wa[0], lru_wx[0]], axis=-1).astype(bf16),
        "lru_ba": lru_ba[0], "lru_bx": lru_bx[0], "lru_lambda": lru_lambda[0],
        "w_out_odd": w_out_odd[0].astype(bf16),
        "ffn_wg": ffn_wg.astype(bf16), "ffn_wu": ffn_wu.astype(bf16), "ffn_wd": ffn_wd.astype(bf16),
        "final_norm": final_norm,
    }
    mods = _ada(jnp.concatenate([c_prompt, c_sample], axis=0), w_ada, b_ada)
    mods = mods.reshape(DEPTH, bp + bs, N_MOD, d)

    out_p = _trunk(x_prompt, mods[:, :bp], None, weights, nb=1, tt=512, hg_tt=512)
    cache_t = lambda c: c[0].transpose(0, 2, 3, 1).reshape(bs, SB_WIDTH, -1)
    past = (cache_t(cache_sb_k), cache_t(cache_sb_v),
            state_hgrn[0], state_conv[0], state_lru[0].reshape(bs, 1, d))
    out_s = _trunk(x_sample, mods[:, bp:], past, weights, nb=bs, tt=ts, hg_tt=ts)

    y_p, k_p, v_p, s_p, conv_p, h_p = out_p
    y_s, k_s, v_s, s_s, conv_s, h_s = out_s
    return (y_p, y_s, k_p, v_p, s_p, conv_p, h_p, k_s, v_s, s_s, conv_s, h_s)
```

```python
import functools
import math

import jax
import jax.numpy as jnp
from jax import lax
from jax.experimental import pallas as pl
from jax.experimental.pallas import tpu as pltpu

f32 = jnp.float32
bf16 = jnp.bfloat16

D_MODEL = 1024
DEPTH = 2
EPS = 1e-6
N_MOD = 6

SB_HEADS = 8
SB_DIM = 64
SB_WIDTH = SB_HEADS * SB_DIM
HG_HEADS = 4
HG_DK = 128
HG_DV = 128
HG_W = HG_HEADS * HG_DK
SEG = 512
N_SEG = 7
HG_CHUNK = 64
HG_SUB = 16
HG_MATMUL_MIN_LOG = -40.0
LRU_BLOCKS = 8
LRU_BLK = D_MODEL // LRU_BLOCKS
CONV_W = 4
C_SCALE = 8.0
D_FF = 2816

SB_WINDOW = 2
SB_QBLOCKS = 2
SB_SKIP_LOG = -105.0

MXU_COLS = 256

LANES = 128
SUBLANES = 8
HALO = 8
VMEM_LIMIT = 60 * 1024 * 1024

NT_DIMS = (((1,), (1,)), ((), ()))
TN_DIMS = (((0,), (0,)), ((), ()))


def _dot(a, b):
    return jnp.dot(a, b, preferred_element_type=f32)


def _dot_nt(a, b):
    return lax.dot_general(a, b, NT_DIMS, preferred_element_type=f32)


def _split3(x):
    hi = x.astype(bf16)
    r1 = x - hi.astype(f32)
    mid = r1.astype(bf16)
    lo = (r1 - mid.astype(f32)).astype(bf16)
    return hi, mid, lo


def _split2(x):
    hi = x.astype(bf16)
    return hi, (x - hi.astype(f32)).astype(bf16)


def _sigmoid(x):
    return jax.nn.sigmoid(x)


def _silu(x):
    return x * _sigmoid(x)


def _norm_mod(x, g, scale, shift):
    ms = jnp.mean(x * x, axis=-1, keepdims=True)
    return x * lax.rsqrt(ms + EPS) * (g * (1.0 + scale)) + shift


def _params(*sem):
    return pltpu.CompilerParams(dimension_semantics=sem, vmem_limit_bytes=VMEM_LIMIT)


def _ada_kernel(c_ref, w_ref, b_ref, o_ref):
    c = c_ref[...]
    sc = _silu(c)
    w = w_ref[...]
    s_hi = sc.astype(bf16)
    s_lo = (sc - s_hi.astype(f32)).astype(bf16)
    w_hi = w.astype(bf16)
    w_lo = (w - w_hi.astype(f32)).astype(bf16)
    o_ref[...] = _dot(s_hi, w_hi) + _dot(s_lo, w_hi) + _dot(s_hi, w_lo) + b_ref[...]


def _ada(c_all, w_ada, b_ada):
    rows = c_all.shape[0]
    n_out = N_MOD * D_MODEL
    tn = 1536
    return pl.pallas_call(
        _ada_kernel,
        grid=(DEPTH, n_out // tn),
        in_specs=[
            pl.BlockSpec((rows, D_MODEL), lambda l, j: (0, 0)),
            pl.BlockSpec((None, D_MODEL, tn), lambda l, j: (l, 0, j)),
            pl.BlockSpec((None, 1, tn), lambda l, j: (l, 0, j)),
        ],
        out_specs=pl.BlockSpec((None, rows, tn), lambda l, j: (l, 0, j)),
        out_shape=jax.ShapeDtypeStruct((DEPTH, rows, n_out), f32),
        compiler_params=_params("arbitrary", "arbitrary"),
        name="ada_mod",
    )(c_all, w_ada, b_ada.reshape(DEPTH, 1, n_out))


def _even_pre_kernel(layer, kv_transposed, x_ref, mod_ref, g_ref, w_ref, wkvt_ref, lbl_ref,
                     qa_ref, ka_ref, va_ref, kab_ref, vab_ref,
                     qb_ref, lf_ref, kb_ref, ib_ref, gs_ref):
    nb, tt, d = x_ref.shape
    m = nb * tt
    hn = _norm_mod(x_ref[...], g_ref[...], mod_ref[:, 1:2, :], mod_ref[:, 0:1, :])
    hn = hn.reshape(m, d).astype(bf16)

    def seg(i):
        return _dot(hn, w_ref[:, i * SEG:(i + 1) * SEG]).reshape(nb, tt, SEG)

    qa_ref[...] = (seg(0) * (1.0 / math.sqrt(SB_DIM))).astype(bf16)
    if kv_transposed:
        for i, (full_ref, blocks_ref) in enumerate(((ka_ref, kab_ref), (va_ref, vab_ref))):
            kt = _dot_nt(wkvt_ref[i], hn)
            full_ref[...] = kt
            for c in range(m // LANES):
                blocks_ref[c] = kt[:, c * LANES:(c + 1) * LANES].astype(bf16)
    else:
        ka = seg(1)
        ka_ref[...] = ka
        kab_ref[...] = ka.astype(bf16)
        va = seg(2)
        va_ref[...] = va
        vab_ref[...] = va.astype(bf16)
    qb_ref[...] = _silu(seg(3)) * (HG_DK ** -0.5)

    lg = lbl_ref[...]
    e = jnp.exp(lg - jnp.max(lg, axis=0, keepdims=True))
    lb = jnp.sum(e[:layer + 1], axis=0, keepdims=True) / jnp.sum(e, axis=0, keepdims=True)
    fb = seg(4)
    f = lb + (1.0 - lb) * _sigmoid(fb)
    lf_ref[...] = jnp.log(f)
    kb_ref[...] = (1.0 - lb) * _sigmoid(-fb)
    ib_ref[...] = seg(5)
    gs_ref[...] = _silu(seg(6))


def _even_pre(x, mod, g, w_bf, wkvt_bf, lb_logits, layer, nb, tt, kv_transposed):
    b, t, d = x.shape
    grid = (b // nb, t // tt)
    blk = lambda width: pl.BlockSpec((nb, tt, width), lambda i, j: (i, j, 0))
    out_f = jax.ShapeDtypeStruct((b, t, SEG), f32)
    out_b = jax.ShapeDtypeStruct((b, t, SEG), bf16)
    if kv_transposed:
        assert nb == 1 and tt % LANES == 0
        kv_f, kv_b = jax.ShapeDtypeStruct((b, SEG, t), f32), jax.ShapeDtypeStruct((b, t // LANES, SEG, LANES), bf16)
        kv_f_spec = pl.BlockSpec((None, SEG, tt), lambda i, j: (i, 0, j))
        kv_b_spec = pl.BlockSpec((None, tt // LANES, SEG, LANES), lambda i, j: (i, j, 0, 0))
    else:
        kv_f, kv_b, kv_f_spec, kv_b_spec = out_f, out_b, blk(SEG), blk(SEG)
    return pl.pallas_call(
        functools.partial(_even_pre_kernel, layer, kv_transposed),
        grid=grid,
        in_specs=[
            blk(d),
            pl.BlockSpec((nb, N_MOD, d), lambda i, j: (i, 0, 0)),
            _resident((1, d)),
            _resident((d, N_SEG * SEG)),
            _resident((2, SEG, d)),
            _resident((DEPTH + 1, HG_W)),
        ],
        out_specs=[blk(SEG), kv_f_spec, kv_f_spec, kv_b_spec, kv_b_spec] + [blk(SEG)] * 5,
        out_shape=[out_b, kv_f, kv_f, kv_b, kv_b, out_f, out_f, out_f, out_f, out_f],
        compiler_params=_params("arbitrary", "arbitrary"),
        name="even_pre",
    )(x, mod, g, w_bf, wkvt_bf, lb_logits)


def _sb_consts(tk, scale=1.0):
    r = lax.broadcasted_iota(jnp.int32, (tk, LANES + tk), 0)
    c = lax.broadcasted_iota(jnp.int32, (tk, LANES + tk), 1)
    m = jnp.where((c < LANES) | (r > c - LANES), scale, 0.0).astype(bf16)
    return jnp.concatenate([m, m], axis=0)


def _sb_groups_units(groups, out):
    zs = [[_dot(q2, k_blk) if transposed else _dot_nt(q2, k_blk)
           for k_blk, _, _, _, transposed in blocks] for q2, blocks, _, _ in groups]
    yield
    l1mbs, sums = [], []
    for g, (_, blocks, _, _) in enumerate(groups):
        l1mbs.append([])
        sums.append([])
        for z, (_, _, sum_matrix, mask, _) in zip(zs[g], blocks):
            l1mb = -(jnp.maximum(z, 0.0) + jnp.log(1.0 + jnp.exp(-jnp.abs(z))))
            if mask is not None:
                l1mb = jnp.where(mask, l1mb, 0.0)
            l1mbs[g].append(l1mb)
            sums[g].append(_dot(jnp.concatenate(_split2(l1mb), axis=1), sum_matrix))
    yield
    results = []
    for g, (_, blocks, carry, acc) in enumerate(groups):
        for z, l1mb, s, (_, v_blk, _, mask, transposed) in zip(zs[g], l1mbs[g], sums[g], blocks):
            tk = z.shape[1]
            after = carry[:, :tk] + s[:, LANES:]
            w = jnp.exp(z + l1mb + after)
            if mask is not None:
                w = jnp.where(mask, w, 0.0)
            w = w.astype(bf16)
            acc = acc + (_dot_nt(w, v_blk) if transposed else _dot(w, v_blk))
            carry = carry + s[:, :LANES]
        results.append((carry, acc))
    out["results"] = results


def _sb_groups(groups):
    out = {}
    for _ in _sb_groups_units(groups, out):
        pass
    return out["results"]


def _sb_blocks(q2, blocks, carry, acc):
    return _sb_groups([(q2, blocks, carry, acc)])[0]


def _sb_finish(j0, carry, acc, step):
    def cond(state):
        j, carry, _ = state
        return (j >= 0) & (jnp.max(carry) > SB_SKIP_LOG)

    def body(state):
        j, carry, acc = state
        carry, acc = step(j, carry, acc)
        return j - 1, carry, acc

    return lax.while_loop(cond, body, (j0, carry, acc))[2]


def _stack_heads(q):
    lane = lax.broadcasted_iota(jnp.int32, q.shape, 1)
    zero = jnp.zeros_like(q)
    return jnp.concatenate([jnp.where(lane < SB_DIM, q, zero), jnp.where(lane >= SB_DIM, q, zero)], axis=0)


def _unstack_heads(acc2):
    tq = acc2.shape[0] // 2
    lane = lax.broadcasted_iota(jnp.int32, (tq, LANES), 1)
    return jnp.where(lane < SB_DIM, acc2[:tq], acc2[tq:])


def _causal_mask2(tq):
    r = lax.broadcasted_iota(jnp.int32, (2 * tq, tq), 0) & (tq - 1)
    c = lax.broadcasted_iota(jnp.int32, (2 * tq, tq), 1)
    return c < r


def _sb_prompt_groups(qs, k_of, v_of, i0, consts):
    sum_matrix, diag_mask, zero = consts
    tq = qs[0].shape[0]
    groups = []
    for g, q in enumerate(qs):
        i = i0 + g
        blocks = [(k_of(i), v_of(i), sum_matrix, diag_mask, True)]
        for c in range(1, SB_WINDOW + 1):
            scale = jnp.where(i >= c, 1.0, 0.0)
            j = jnp.maximum(i - c, 0)
            blocks.append((k_of(j), (v_of(j).astype(f32) * scale).astype(bf16), _sb_consts(tq, scale), None, True))
        groups.append((_stack_heads(q), blocks, zero, zero))
    return groups


def _sb_prompt_consts(tq):
    return _sb_consts(tq), _causal_mask2(tq), jnp.zeros((2 * tq, LANES), f32)


def _sb_prompt_finish(group, result, i, k_of, v_of, consts):
    q2 = group[0]
    carry, acc = result

    def step(j, carry, acc):
        return _sb_blocks(q2, [(k_of(j), v_of(j), consts[0], None, True)], carry, acc)

    return _unstack_heads(_sb_finish(i - SB_WINDOW - 1, carry, acc, step))


def _sb_prompt_kernel(q_ref, k_ref, v_ref, o_ref):
    tq = k_ref.shape[2]
    n_q = q_ref.shape[0] // tq
    i0 = pl.program_id(2) * n_q
    consts = _sb_prompt_consts(tq)
    k_of, v_of = (lambda j: k_ref[j]), (lambda j: v_ref[j])
    groups = _sb_prompt_groups([q_ref[g * tq:(g + 1) * tq, :] for g in range(n_q)], k_of, v_of, i0, consts)
    for g, result in enumerate(_sb_groups(groups)):
        o = _sb_prompt_finish(groups[g], result, i0 + g, k_of, v_of, consts)
        o_ref[g * tq:(g + 1) * tq, :] = o.astype(o_ref.dtype)


def _sb_prompt(q, k_blocks, v_blocks):
    b, t, _ = q.shape
    tk = LANES
    tq = SB_QBLOCKS * tk
    n_pair = SB_WIDTH // LANES
    qspec = pl.BlockSpec((None, tq, LANES), lambda bi, p, i: (bi, i, p))
    kvspec = pl.BlockSpec((None, t // tk, LANES, tk), lambda bi, p, i: (bi, 0, p, 0))
    return pl.pallas_call(
        _sb_prompt_kernel,
        grid=(b, n_pair, t // tq),
        in_specs=[qspec, kvspec, kvspec],
        out_specs=qspec,
        out_shape=jax.ShapeDtypeStruct((b, t, SB_WIDTH), bf16),
        compiler_params=_params("arbitrary", "arbitrary", "arbitrary"),
        name="sb_prompt",
    )(q, k_blocks, v_blocks)


def _sb_sample_kernel(tk, n_past, q_ref, k_ref, v_ref, wk_ref, wv_ref, pk_hbm, pv_hbm, o_ref,
                      kbuf, vbuf, sem):
    tq = q_ref.shape[0]
    b = pl.program_id(0)
    zero = jnp.zeros((2 * tq, LANES), f32)
    sum_matrix = _sb_consts(tk)
    diag_matrix = _sb_consts(tq)
    diag_mask = _causal_mask2(tq)

    def fetch(j):
        keys = pl.ds(pl.multiple_of(j * tk, tk), tk)
        copies = [pltpu.make_async_copy(pk_hbm.at[b, :, keys], kbuf, sem.at[0]),
                  pltpu.make_async_copy(pv_hbm.at[b, :, keys], vbuf, sem.at[1])]
        for cp in copies:
            cp.start()
        for cp in copies:
            cp.wait()

    groups = []
    for pair in range(SB_WIDTH // LANES):
        cols = slice(pair * LANES, (pair + 1) * LANES)
        q2 = _stack_heads(q_ref[:, cols])
        blocks = [(k_ref[:, cols], v_ref[:, cols], diag_matrix, diag_mask, False)]
        for c in range(1, SB_WINDOW + 1):
            keys = slice((SB_WINDOW - c) * tk, (SB_WINDOW - c + 1) * tk)
            blocks.append((wk_ref[cols, keys].astype(bf16), wv_ref[cols, keys].astype(bf16),
                           sum_matrix, None, True))
        groups.append((q2, blocks, zero, zero))

    for pair, (carry, acc) in enumerate(_sb_groups(groups)):
        cols = slice(pair * LANES, (pair + 1) * LANES)
        q2 = groups[pair][0]

        def step(j, carry, acc, cols=cols, q2=q2):
            fetch(j)
            blk = (kbuf[cols, :].astype(bf16), vbuf[cols, :].astype(bf16), sum_matrix, None, True)
            return _sb_blocks(q2, [blk], carry, acc)

        acc = _sb_finish(n_past - SB_WINDOW - 1, carry, acc, step)
        o_ref[:, cols] = _unstack_heads(acc).astype(o_ref.dtype)


def _sb_sample(q, k, v, past_kt, past_vt, tk):
    b, t, _ = q.shape
    n_past = past_kt.shape[2] // tk
    assert n_past % SB_WINDOW == 0 and n_past >= SB_WINDOW
    qspec = pl.BlockSpec((None, t, SB_WIDTH), lambda bi: (bi, 0, 0))
    wspec = pl.BlockSpec((None, SB_WIDTH, SB_WINDOW * tk), lambda bi: (bi, 0, n_past // SB_WINDOW - 1))
    hbm = pl.BlockSpec(memory_space=pl.ANY)
    return pl.pallas_call(
        functools.partial(_sb_sample_kernel, tk, n_past),
        grid=(b,),
        in_specs=[qspec, qspec, qspec, wspec, wspec, hbm, hbm],
        out_specs=qspec,
        out_shape=jax.ShapeDtypeStruct((b, t, SB_WIDTH), bf16),
        scratch_shapes=[pltpu.VMEM((SB_WIDTH, tk), f32), pltpu.VMEM((SB_WIDTH, tk), f32),
                        pltpu.SemaphoreType.DMA((2,))],
        compiler_params=_params("arbitrary"),
        name="sb_sample",
    )(q, k, v, past_kt, past_vt, past_kt, past_vt)


def _hgrn_head_chunk(q, k, v, b, st):
    n_sub = HG_CHUNK // HG_SUB
    v_bf = v.astype(bf16)
    o_inter = _dot_nt((q * jnp.exp(b)).astype(bf16), st.astype(bf16))
    t_idx = lax.broadcasted_iota(jnp.int32, (HG_SUB, HG_SUB, HG_DK), 0)
    s_idx = lax.broadcasted_iota(jnp.int32, (HG_SUB, HG_SUB, HG_DK), 1)
    causal = s_idx <= t_idx
    outs = []
    for i in range(n_sub):
        lo = i * HG_SUB
        rows = slice(lo, lo + HG_SUB)
        qi, ki, bi = q[rows], k[rows], b[rows]
        diff = bi[:, None, :] - bi[None, :, :]
        dec = jnp.exp(jnp.where(causal, diff, -1e30))
        a_diag = jnp.sum(qi[:, None, :] * ki[None, :, :] * dec, axis=-1)
        o_i = _dot(a_diag.astype(bf16), v_bf[rows])
        if i > 0:
            ref = b[lo - 1:lo]
            q_t = qi * jnp.exp(bi - ref)
            k_t = k[:lo] * jnp.exp(ref - b[:lo])
            a_off = _dot_nt(q_t.astype(bf16), k_t.astype(bf16))
            o_i = o_i + _dot(a_off.astype(bf16), v_bf[:lo])
        outs.append(o_inter[rows] + o_i)
    o = jnp.concatenate(outs, axis=0)
    b_last = b[HG_CHUNK - 1:HG_CHUNK]
    k_dec = (k * jnp.exp(b_last - b)).astype(bf16)
    st_new = jnp.exp(b_last) * st + lax.dot_general(v_bf, k_dec, TN_DIMS, preferred_element_type=f32)
    return o, st_new


def _hgrn_chunk_matmul_units(qs, ks, vs, bs, sts, out):
    n_sub = HG_CHUNK // HG_SUB
    heads = range(len(qs))
    q_in, q_t, k_t, k_dec, v_bf, b_last = [], [], [], [], [], []
    for h in heads:
        q, k, b = qs[h], ks[h], bs[h]
        q_in.append((q * jnp.exp(b)).astype(bf16))
        q_t.append([])
        k_t.append([])
        for i in range(n_sub):
            lo, hi = i * HG_SUB, (i + 1) * HG_SUB
            if i == 0:
                q_t[h].append(q_in[h][:hi])
                k_t[h].append((k[:hi] * jnp.exp(-b[:hi])).astype(bf16))
            else:
                ref = b[lo - 1:lo]
                q_t[h].append((q[lo:hi] * jnp.exp(b[lo:hi] - ref)).astype(bf16))
                k_t[h].append((k[:hi] * jnp.exp(ref - b[:hi])).astype(bf16))
        b_last.append(b[HG_CHUNK - 1:HG_CHUNK])
        k_dec.append((k * jnp.exp(b_last[h] - b)).astype(bf16))
        v_bf.append(vs[h].astype(bf16))
    yield

    o_inter = [_dot_nt(q_in[h], sts[h].astype(bf16)) for h in heads]
    attn = [[_dot_nt(q_t[h][i], k_t[h][i]) for i in range(n_sub)] for h in heads]
    st_add = [lax.dot_general(v_bf[h], k_dec[h], TN_DIMS, preferred_element_type=f32) for h in heads]
    yield

    outs, new_sts = [], []
    for h in heads:
        pieces = []
        for i in range(n_sub):
            hi = (i + 1) * HG_SUB
            r = lax.broadcasted_iota(jnp.int32, (HG_SUB, hi), 0)
            c = lax.broadcasted_iota(jnp.int32, (HG_SUB, hi), 1)
            a = jnp.where(c - i * HG_SUB <= r, attn[h][i], 0.0)
            pieces.append(_dot(a.astype(bf16), v_bf[h][:hi]))
        outs.append(jnp.concatenate(pieces, axis=0) + o_inter[h])
        new_sts.append(jnp.exp(b_last[h]) * sts[h] + st_add[h])
    out.update(o=outs, st=new_sts)


def _hgrn_chunk_matmul(qs, ks, vs, bs, sts):
    out = {}
    for _ in _hgrn_chunk_matmul_units(qs, ks, vs, bs, sts, out):
        pass
    return out["o"], out["st"]


def _hgrn_kernel(q_ref, lf_ref, k_ref, v_ref, gs_ref, gn_ref, s0_ref, o_ref, sout_ref, st_scr):
    tb = pl.program_id(1)
    n_tb = pl.num_programs(1)
    n_chunk = q_ref.shape[0] // HG_CHUNK
    n_sub = HG_CHUNK // HG_SUB

    @pl.when(tb == 0)
    def _():
        for h in range(HG_HEADS):
            st_scr[h] = s0_ref[h].T

    r = lax.broadcasted_iota(jnp.int32, (HG_CHUNK, HG_CHUNK), 0)
    c = lax.broadcasted_iota(jnp.int32, (HG_CHUNK, HG_CHUNK), 1)
    lower = jnp.where(c <= r, 1.0, 0.0).astype(bf16)
    head_cols = [slice(h * HG_DK, (h + 1) * HG_DK) for h in range(HG_HEADS)]

    def chunk(ci, carry):
        rows = pl.ds(pl.multiple_of(ci * HG_CHUNK, HG_CHUNK), HG_CHUNK)
        b_all = _dot_exact_rhs_lhs(lower, lf_ref[rows, :])

        def finish(h, o, st_new):
            cols = head_cols[h]
            st_scr[h] = st_new
            o = o * lax.rsqrt(jnp.mean(o * o, axis=-1, keepdims=True) + EPS)
            o_ref[rows, cols] = (o * gn_ref[:, cols] * gs_ref[rows, cols]).astype(o_ref.dtype)

        ends = [b_all[(i + 1) * HG_SUB - 1:(i + 1) * HG_SUB, :] for i in range(n_sub)]
        span = ends[0]
        for i in range(1, n_sub):
            span = jnp.minimum(span, ends[i] - ends[i - 1])
        bounded = jnp.min(span) >= HG_MATMUL_MIN_LOG

        @pl.when(bounded)
        def _():
            outs, new_sts = _hgrn_chunk_matmul(
                [q_ref[rows, cols] for cols in head_cols], [k_ref[rows, cols] for cols in head_cols],
                [v_ref[rows, cols] for cols in head_cols], [b_all[:, cols] for cols in head_cols],
                [st_scr[h] for h in range(HG_HEADS)])
            for h in range(HG_HEADS):
                finish(h, outs[h], new_sts[h])

        @pl.when(jnp.logical_not(bounded))
        def _():
            for h, cols in enumerate(head_cols):
                o, st_new = _hgrn_head_chunk(q_ref[rows, cols], k_ref[rows, cols], v_ref[rows, cols],
                                             b_all[:, cols], st_scr[h])
                finish(h, o, st_new)

        return carry

    lax.fori_loop(0, n_chunk, chunk, 0)

    @pl.when(tb == n_tb - 1)
    def _():
        for h in range(HG_HEADS):
            sout_ref[h] = st_scr[h].T


def _dot_exact_rhs_lhs(m, x):
    hi, mid, lo = _split3(x)
    return _dot(m, hi) + _dot(m, mid) + _dot(m, lo)


def _hgrn(qb, lf, kb, ib, gs, gnorm, s0, tt):
    b, t, _ = qb.shape
    blk = pl.BlockSpec((None, tt, HG_W), lambda i, j: (i, j, 0))
    sspec = pl.BlockSpec((None, HG_HEADS, HG_DK, HG_DV), lambda i, j: (i, 0, 0, 0))
    return pl.pallas_call(
        _hgrn_kernel,
        grid=(b, t // tt),
        in_specs=[blk, blk, blk, blk, blk, pl.BlockSpec((1, HG_W), lambda i, j: (0, 0)), sspec],
        out_specs=[blk, sspec],
        out_shape=[jax.ShapeDtypeStruct((b, t, HG_W), bf16),
                   jax.ShapeDtypeStruct((b, HG_HEADS, HG_DK, HG_DV), f32)],
        scratch_shapes=[pltpu.VMEM((HG_HEADS, HG_DV, HG_DK), f32)],
        compiler_params=_params("arbitrary", "arbitrary"),
        name="hgrn",
    )(qb, lf, kb, ib, gs, gnorm, s0)


def _round_robin(*gens, units=None):
    turns = list(zip(gens, units or [1] * len(gens)))
    while turns:
        turns = [(gen, n) for gen, n in turns if all(next(gen, "done") != "done" for _ in range(n))]


def _even_fused_kernel(layer, n_tb, x_ref, mod_ref, g_ref, w_ref, wkvt_ref, lbl_ref, gn_ref, s0_ref,
                       qa_ref, ka_ref, va_ref, kab_ref, vab_ref, ob_ref, sout_ref,
                       q_scr, b_scr, k_scr, v_scr, gs_scr, st_scr):
    _, tt, d = x_ref.shape
    s = pl.program_id(0)
    hg_blk = jnp.maximum(s - 1, 0)
    tb_h = lax.rem(hg_blk, n_tb)
    n_chunk = tt // HG_CHUNK
    n_sub = HG_CHUNK // HG_SUB
    head_cols = [slice(h * HG_DK, (h + 1) * HG_DK) for h in range(HG_HEADS)]
    operand_scr = (q_scr, b_scr, k_scr, v_scr, gs_scr)

    @pl.when(s == 0)
    def _():
        for scr in operand_scr:
            scr[...] = jnp.zeros_like(scr)

    @pl.when(tb_h == 0)
    def _():
        for h in range(HG_HEADS):
            st_scr[h] = s0_ref[h].T

    r = lax.broadcasted_iota(jnp.int32, (HG_CHUNK, HG_CHUNK), 0)
    c = lax.broadcasted_iota(jnp.int32, (HG_CHUNK, HG_CHUNK), 1)
    lower = jnp.where(c <= r, 1.0, 0.0).astype(bf16)

    def proj_units(out):
        hn = _norm_mod(x_ref[...], g_ref[...], mod_ref[:, 1:2, :], mod_ref[:, 0:1, :])
        hn = hn.reshape(tt, d).astype(bf16)
        yield

        def seg(i):
            return _dot(hn, w_ref[:, i * SEG:(i + 1) * SEG])

        qa_ref[...] = (seg(0) * (1.0 / math.sqrt(SB_DIM))).astype(bf16).reshape(1, tt, SEG)
        yield
        for i, (full_ref, blocks_ref) in enumerate(((ka_ref, kab_ref), (va_ref, vab_ref))):
            kt = _dot_nt(wkvt_ref[i], hn)
            full_ref[...] = kt
            for cb in range(tt // LANES):
                blocks_ref[cb] = kt[:, cb * LANES:(cb + 1) * LANES].astype(bf16)
            yield
        q = _silu(seg(3)) * (HG_DK ** -0.5)
        yield
        lg = lbl_ref[...]
        e = jnp.exp(lg - jnp.max(lg, axis=0, keepdims=True))
        lb = jnp.sum(e[:layer + 1], axis=0, keepdims=True) / jnp.sum(e, axis=0, keepdims=True)
        fb = seg(4)
        lf = jnp.log(lb + (1.0 - lb) * _sigmoid(fb))
        k = (1.0 - lb) * _sigmoid(-fb)
        yield
        b = jnp.concatenate([_dot_exact_rhs_lhs(lower, lf[ci * HG_CHUNK:(ci + 1) * HG_CHUNK])
                             for ci in range(n_chunk)], axis=0)
        yield
        v = seg(5)
        yield
        gs = _silu(seg(6))
        out.update(q=q, b=b, k=k, v=v, gs=gs)

    def store_operands(out):
        for scr, name in zip(operand_scr, ("q", "b", "k", "v", "gs")):
            scr[...] = out[name]

    def finish(rows, h, o):
        cols = head_cols[h]
        o = o * lax.rsqrt(jnp.mean(o * o, axis=-1, keepdims=True) + EPS)
        ob_ref[0, rows, cols] = (o * gn_ref[:, cols] * gs_scr[rows, cols]).astype(ob_ref.dtype)

    def hgrn_units():
        sts = [st_scr[h] for h in range(HG_HEADS)]
        for ci in range(n_chunk):
            rows = slice(ci * HG_CHUNK, (ci + 1) * HG_CHUNK)
            out = {}
            yield from _hgrn_chunk_matmul_units(
                [q_scr[rows, cols] for cols in head_cols], [k_scr[rows, cols] for cols in head_cols],
                [v_scr[rows, cols] for cols in head_cols], [b_scr[rows, cols] for cols in head_cols],
                sts, out)
            sts = out["st"]
            for h in range(HG_HEADS):
                finish(rows, h, out["o"][h])
            yield
        for h in range(HG_HEADS):
            st_scr[h] = sts[h]

    span = None
    for ci in range(n_chunk):
        prev = None
        for i in range(n_sub):
            row = ci * HG_CHUNK + (i + 1) * HG_SUB - 1
            end = b_scr[row:row + 1, :]
            this = end if prev is None else end - prev
            span = this if span is None else jnp.minimum(span, this)
            prev = end
    bounded = jnp.min(span) >= HG_MATMUL_MIN_LOG

    @pl.when(bounded)
    def _():
        res = {}
        _round_robin(hgrn_units(), proj_units(res), units=[3, 1])
        store_operands(res)

    @pl.when(jnp.logical_not(bounded))
    def _():
        def chunk(ci, carry):
            rows = pl.ds(pl.multiple_of(ci * HG_CHUNK, HG_CHUNK), HG_CHUNK)
            for h, cols in enumerate(head_cols):
                o, st_new = _hgrn_head_chunk(q_scr[rows, cols], k_scr[rows, cols], v_scr[rows, cols],
                                             b_scr[rows, cols], st_scr[h])
                st_scr[h] = st_new
                finish(rows, h, o)
            return carry

        lax.fori_loop(0, n_chunk, chunk, 0)
        res = {}
        _round_robin(proj_units(res))
        store_operands(res)

    @pl.when((tb_h == n_tb - 1) & (s > 0))
    def _():
        for h in range(HG_HEADS):
            sout_ref[h] = st_scr[h].T


def _even_fused(x, mod, g, w_bf, wkvt_bf, lb_logits, gnorm, s0, layer, tt):
    b, t, d = x.shape
    n_tb = t // tt
    n_blocks = b * n_tb
    proj_blk = lambda s: jnp.minimum(s, n_blocks - 1)
    hgrn_blk = lambda s: jnp.maximum(s - 1, 0)
    rows = lambda width, which: pl.BlockSpec((1, tt, width), lambda s: (which(s) // n_tb, which(s) % n_tb, 0))
    operand = pltpu.VMEM((tt, HG_W), f32)
    return pl.pallas_call(
        functools.partial(_even_fused_kernel, layer, n_tb),
        grid=(n_blocks + 1,),
        in_specs=[
            rows(d, proj_blk),
            pl.BlockSpec((1, N_MOD, d), lambda s: (proj_blk(s) // n_tb, 0, 0)),
            _resident((1, d)),
            _resident((d, N_SEG * SEG)),
            _resident((2, SEG, d)),
            _resident((DEPTH + 1, HG_W)),
            _resident((1, HG_W)),
            pl.BlockSpec((None, HG_HEADS, HG_DK, HG_DV), lambda s: (hgrn_blk(s) // n_tb, 0, 0, 0)),
        ],
        out_specs=[
            rows(SEG, proj_blk),
            pl.BlockSpec((None, SEG, tt), lambda s: (proj_blk(s) // n_tb, 0, proj_blk(s) % n_tb)),
            pl.BlockSpec((None, SEG, tt), lambda s: (proj_blk(s) // n_tb, 0, proj_blk(s) % n_tb)),
            pl.BlockSpec((None, tt // LANES, SEG, LANES), lambda s: (proj_blk(s) // n_tb, proj_blk(s) % n_tb, 0, 0)),
            pl.BlockSpec((None, tt // LANES, SEG, LANES), lambda s: (proj_blk(s) // n_tb, proj_blk(s) % n_tb, 0, 0)),
            rows(HG_W, hgrn_blk),
            pl.BlockSpec((None, HG_HEADS, HG_DK, HG_DV), lambda s: (hgrn_blk(s) // n_tb, 0, 0, 0)),
        ],
        out_shape=[
            jax.ShapeDtypeStruct((b, t, SEG), bf16),
            jax.ShapeDtypeStruct((b, SEG, t), f32), jax.ShapeDtypeStruct((b, SEG, t), f32),
            jax.ShapeDtypeStruct((b, t // LANES, SEG, LANES), bf16),
            jax.ShapeDtypeStruct((b, t // LANES, SEG, LANES), bf16),
            jax.ShapeDtypeStruct((b, t, HG_W), bf16),
            jax.ShapeDtypeStruct((b, HG_HEADS, HG_DK, HG_DV), f32),
        ],
        scratch_shapes=[operand] * 5 + [pltpu.VMEM((HG_HEADS, HG_DV, HG_DK), f32)],
        compiler_params=_params("arbitrary"),
        name="even_pre_hgrn",
    )(x, mod, g, w_bf, wkvt_bf, lb_logits, gnorm, s0)


def _ffn(hn_bf, wg_ref, wu_ref, wd_ref):
    g = _dot(hn_bf, wg_ref[...])
    u = _dot(hn_bf, wu_ref[...])
    return _dot((_silu(g) * u).astype(bf16), wd_ref[...])


def _resident(shape):
    return pl.BlockSpec(shape, lambda *_: (0,) * len(shape), pipeline_mode=pl.Buffered(1))


def _ffn_specs(layer):
    one = lambda rows, cols: pl.BlockSpec((None, rows, cols), lambda *_: (layer, 0, 0),
                                          pipeline_mode=pl.Buffered(1))
    return [one(D_MODEL, D_FF), one(D_MODEL, D_FF), one(D_FF, D_MODEL)]


def _even_post_kernel(x_ref, mod_ref, oa_ref, ob_ref, wo_ref, gf_ref, wg_ref, wu_ref, wd_ref, out_ref):
    nb, tt, d = x_ref.shape
    m = nb * tt
    o = (_dot(oa_ref[...].reshape(m, SB_WIDTH), wo_ref[:SB_WIDTH, :])
         + _dot(ob_ref[...].reshape(m, HG_W), wo_ref[SB_WIDTH:, :]))
    x1 = x_ref[...] + (1.0 + mod_ref[:, 2:3, :]) * o.reshape(nb, tt, d)
    hn = _norm_mod(x1, gf_ref[...], mod_ref[:, 4:5, :], mod_ref[:, 3:4, :])
    y = _ffn(hn.reshape(m, d).astype(bf16), wg_ref, wu_ref, wd_ref)
    out_ref[...] = x1 + (1.0 + mod_ref[:, 5:6, :]) * y.reshape(nb, tt, d)


def _even_post(x, mod, oa, ob, wo, gf, wg, wu, wd, layer, nb, tt):
    b, t, d = x.shape
    blk = lambda width: pl.BlockSpec((nb, tt, width), lambda i, tb: (i, tb, 0))
    return pl.pallas_call(
        _even_post_kernel,
        grid=(b // nb, t // tt),
        in_specs=[blk(d), pl.BlockSpec((nb, N_MOD, d), lambda i, tb: (i, 0, 0)),
                  blk(SB_WIDTH), blk(HG_W), _resident((d, d)), _resident((1, d))] + _ffn_specs(layer),
        out_specs=blk(d),
        out_shape=jax.ShapeDtypeStruct((b, t, d), f32),
        compiler_params=_params("arbitrary", "arbitrary"),
        name="even_post_ffn",
    )(x, mod, oa, ob, wo, gf, wg, wu, wd)


def _even_post_attn_kernel(n_tb, x_ref, mod_ref, ob_ref, q_ref, k_ref, v_ref, wo_ref, gf_ref,
                           wg_ref, wu_ref, wd_ref, out_ref, oa_scr):
    _, tt, d = x_ref.shape
    tq = LANES
    n_q = tt // tq
    n_pair = SB_WIDTH // LANES
    s = pl.program_id(0)
    attn_blk = jnp.minimum(s, pl.num_programs(0) - 2)
    i0 = lax.rem(attn_blk, n_tb) * n_q
    consts = _sb_prompt_consts(tq)
    pair_cols = [slice(p * LANES, (p + 1) * LANES) for p in range(n_pair)]
    k_of = [lambda j, cols=cols: k_ref[j, cols, :] for cols in pair_cols]
    v_of = [lambda j, cols=cols: v_ref[j, cols, :] for cols in pair_cols]

    @pl.when(s == 0)
    def _():
        oa_scr[...] = jnp.zeros_like(oa_scr)

    def post_units(oa):
        o = _dot(oa, wo_ref[:SB_WIDTH, :])
        yield
        o = o + _dot(ob_ref[...].reshape(tt, HG_W), wo_ref[SB_WIDTH:, :])
        yield
        x1 = x_ref[...] + (1.0 + mod_ref[:, 2:3, :]) * o.reshape(1, tt, d)
        hn = _norm_mod(x1, gf_ref[...], mod_ref[:, 4:5, :], mod_ref[:, 3:4, :])
        hn = hn.reshape(tt, d).astype(bf16)
        yield
        acts = []
        for j in range(D_FF // MXU_COLS):
            cols = slice(j * MXU_COLS, (j + 1) * MXU_COLS)
            g = _dot(hn, wg_ref[:, cols])
            u = _dot(hn, wu_ref[:, cols])
            acts.append((_silu(g) * u).astype(bf16))
            yield
        act = jnp.concatenate(acts, axis=1)
        for n in range(d // MXU_COLS):
            cols = slice(n * MXU_COLS, (n + 1) * MXU_COLS)
            y = _dot(act, wd_ref[:, cols]).reshape(1, tt, MXU_COLS)
            out_ref[:, :, cols] = x1[:, :, cols] + (1.0 + mod_ref[:, 5:6, cols]) * y
            yield

    def attn_units(out):
        for p, cols in enumerate(pair_cols):
            qs = [q_ref[0, g * tq:(g + 1) * tq, cols] for g in range(n_q)]
            groups = _sb_prompt_groups(qs, k_of[p], v_of[p], i0, consts)
            res = {}
            yield from _sb_groups_units(groups, res)
            out[p] = res["results"]
            yield

    results = {}
    _round_robin(post_units(oa_scr[...]), attn_units(results), units=[3, 2])

    reach = None
    for p, cols in enumerate(pair_cols):
        for g, (carry, acc) in enumerate(results[p]):
            oa_scr[g * tq:(g + 1) * tq, cols] = _unstack_heads(acc).astype(oa_scr.dtype)
            reach = carry if reach is None else jnp.maximum(reach, carry)

    @pl.when(jnp.max(reach) > SB_SKIP_LOG)
    def _():
        for p, cols in enumerate(pair_cols):
            def one(g, carry, p=p, cols=cols):
                rows = pl.ds(pl.multiple_of(g * tq, tq), tq)
                groups = _sb_prompt_groups([q_ref[0, rows, cols]], k_of[p], v_of[p], i0 + g, consts)
                o = _sb_prompt_finish(groups[0], _sb_groups(groups)[0], i0 + g, k_of[p], v_of[p], consts)
                oa_scr[rows, cols] = o.astype(oa_scr.dtype)
                return carry

            lax.fori_loop(0, n_q, one, 0)


def _even_post_attn(x, mod, ob, q, k_blocks, v_blocks, wo, gf, wg, wu, wd, layer, tt):
    b, t, d = x.shape
    n_tb = t // tt
    n_blocks = b * n_tb
    attn_blk = lambda s: jnp.minimum(s, n_blocks - 1)
    post_blk = lambda s: jnp.maximum(s - 1, 0)
    rows = lambda width, which: pl.BlockSpec((1, tt, width), lambda s: (which(s) // n_tb, which(s) % n_tb, 0))
    keys = pl.BlockSpec((None, t // LANES, SB_WIDTH, LANES), lambda s: (attn_blk(s) // n_tb, 0, 0, 0),
                        pipeline_mode=pl.Buffered(1))
    return pl.pallas_call(
        functools.partial(_even_post_attn_kernel, n_tb),
        grid=(n_blocks + 1,),
        in_specs=[rows(d, post_blk), pl.BlockSpec((1, N_MOD, d), lambda s: (post_blk(s) // n_tb, 0, 0)),
                  rows(HG_W, post_blk), rows(SB_WIDTH, attn_blk), keys, keys,
                  _resident((d, d)), _resident((1, d))] + _ffn_specs(layer),
        out_specs=rows(d, post_blk),
        out_shape=jax.ShapeDtypeStruct((b, t, d), f32),
        scratch_shapes=[pltpu.VMEM((tt, SB_WIDTH), bf16)],
        compiler_params=_params("arbitrary"),
        name="even_attn_post_ffn",
    )(x, mod, ob, q, k_blocks, v_blocks, wo, gf, wg, wu, wd)


def _linear_scan(a, u, h_init, tt):
    m, d = a.shape
    n_group = m // SUBLANES
    a3 = a.reshape(n_group, SUBLANES, d)
    u3 = u.reshape(n_group, SUBLANES, d)
    sub = lax.broadcasted_iota(jnp.int32, (1, SUBLANES, d), 1)
    step = 1
    while step < SUBLANES:
        keep = sub >= step
        a_prev = jnp.where(keep, pltpu.roll(a3, step, 1), 1.0)
        u_prev = jnp.where(keep, pltpu.roll(u3, step, 1), 0.0)
        u3 = a3 * u_prev + u3
        a3 = a3 * a_prev
        step *= 2
    groups_per_seq = tt // SUBLANES
    carry = None
    groups = []
    for gi in range(n_group):
        if gi % groups_per_seq == 0:
            carry = h_init[gi // groups_per_seq]
        h_group = u3[gi] + a3[gi] * carry
        groups.append(h_group)
        carry = h_group[SUBLANES - 1:SUBLANES, :]
    return jnp.concatenate(groups, axis=0)


def _odd_kernel(stream_start, pipelined, n_tb, x_ref, mod_ref, mod_ffn_ref, gm_ref, win_ref, cw_ref, cb_ref,
                wax_ref, ba_ref, bx_ref, lam_ref, wo_ref, conv0_ref, h0_ref, gf_ref, fn_ref,
                wg_ref, wu_ref, wd_ref, y_ref, conv_ref, hout_ref, xs_scr, hprev_scr, x1_scr, hn_scr):
    nb, tt, d = x_ref.shape
    m = nb * tt
    s = pl.program_id(0)
    tb = lax.rem(s, n_tb)

    def ffn_units(x1, hn_bf, out):
        acts = []
        for j in range(D_FF // MXU_COLS):
            cols = slice(j * MXU_COLS, (j + 1) * MXU_COLS)
            g = _dot(hn_bf, wg_ref[:, cols])
            u = _dot(hn_bf, wu_ref[:, cols])
            acts.append((_silu(g) * u).astype(bf16))
            yield
        act = jnp.concatenate(acts, axis=1)
        parts = []
        for n in range(d // MXU_COLS):
            cols = slice(n * MXU_COLS, (n + 1) * MXU_COLS)
            y = _dot(act, wd_ref[:, cols]).reshape(nb, tt, MXU_COLS)
            parts.append(x1[:, :, cols] + (1.0 + mod_ffn_ref[:, 5:6, cols]) * y)
            yield
        x2 = jnp.concatenate(parts, axis=-1)
        ms = jnp.mean(x2 * x2, axis=-1, keepdims=True)
        out["y"] = x2 * lax.rsqrt(ms + EPS) * fn_ref[...]

    def mixer_units(out):
        x = x_ref[...]
        hn = _norm_mod(x, gm_ref[...], mod_ref[:, 1:2, :], mod_ref[:, 0:1, :])
        hn = hn.reshape(m, d).astype(bf16)
        yield
        gate = _dot(hn, win_ref[:, :d])
        yield
        xbr = _dot(hn, win_ref[:, d:]).reshape(nb, tt, d)
        yield
        xs_scr[:, HALO:HALO + tt, :] = xbr
        xc = cb_ref[...] + xbr * cw_ref[CONV_W - 1:CONV_W, :]
        for k in range(1, CONV_W):
            xc = xc + xs_scr[:, HALO - k:HALO - k + tt, :] * cw_ref[CONV_W - 1 - k:CONV_W - k, :]
        tail = xs_scr[:, HALO + tt - (CONV_W - 1):HALO + tt, :]
        xs_scr[:, HALO - (CONV_W - 1):HALO, :] = tail
        xc = xc.reshape(m, d)
        xc_bf = xc.astype(bf16)
        yield
        lam = lam_ref[...]
        log_sig_lam = -(jnp.maximum(-lam, 0.0) + jnp.log1p(jnp.exp(-jnp.abs(lam))))
        y_in, h_last = [], []
        for blk in range(LRU_BLOCKS):
            cols = slice(blk * LRU_BLK, (blk + 1) * LRU_BLK)
            both = _dot(xc_bf[:, cols], wax_ref[blk])
            r = _sigmoid(both[:, :LRU_BLK] + ba_ref[:, cols])
            gi = _sigmoid(both[:, LRU_BLK:] + bx_ref[:, cols])
            log_a = C_SCALE * r * log_sig_lam[:, cols]
            a = jnp.exp(log_a)
            mult = jnp.sqrt(-jnp.tanh(log_a) * (1.0 + a * a))
            if stream_start:
                row = (lax.broadcasted_iota(jnp.int32, (m, LRU_BLK), 0) & (tt - 1)) + tb * tt
                mult = jnp.where(row == 0, 1.0, mult)
            h = _linear_scan(a, mult * gi * xc[:, cols], hprev_scr[:, :, cols], tt)
            h_last.append(h.reshape(nb, tt, LRU_BLK)[:, tt - 1:tt, :])
            y_in.append((jax.nn.gelu(gate[:, cols]) * h).astype(bf16))
            yield
        h_last = jnp.concatenate(h_last, axis=-1)
        hprev_scr[...] = h_last
        y_in = jnp.concatenate(y_in, axis=1)
        o = _dot(y_in, wo_ref[...])
        yield
        x1 = x + (1.0 + mod_ref[:, 2:3, :]) * o.reshape(nb, tt, d)
        hn2 = _norm_mod(x1, gf_ref[...], mod_ref[:, 4:5, :], mod_ref[:, 3:4, :])
        out.update(x1=x1, hn2=hn2.reshape(m, d).astype(bf16), tail=tail, h_last=h_last)

    @pl.when(tb == 0)
    def _():
        xs_scr[:, HALO - (CONV_W - 1):HALO, :] = conv0_ref[...]
        hprev_scr[...] = h0_ref[...]

    res = {}
    if pipelined:
        @pl.when(s == 0)
        def _():
            x1_scr[...] = jnp.zeros_like(x1_scr)
            hn_scr[...] = jnp.zeros_like(hn_scr)

        units = [ffn_units(x1_scr[...], hn_scr[...], res), mixer_units(res)]
        while units:
            units = [gen for gen in units if next(gen, "done") != "done"]
        y_ref[...] = res["y"]
        x1_scr[...] = res["x1"]
        hn_scr[...] = res["hn2"]

        @pl.when(s < pl.num_programs(0) - 1)
        def _():
            conv_ref[...] = res["tail"]
            hout_ref[...] = res["h_last"]
    else:
        for _ in mixer_units(res):
            pass
        conv_ref[...] = res["tail"]
        hout_ref[...] = res["h_last"]
        for _ in ffn_units(res["x1"], res["hn2"], res):
            pass
        y_ref[...] = res["y"]


def _odd_layer(x, mod, gm, win, cw, cb, wax, ba, bx, lam, wo, conv0, h0, gf, fn, wg, wu, wd,
               layer, nb, tt, stream_start):
    b, t, d = x.shape
    n_tb = t // tt
    n_blocks = (b // nb) * n_tb
    pipelined = n_blocks > 1
    mix_blk = lambda s: jnp.minimum(s, n_blocks - 1)
    ffn_blk = (lambda s: jnp.maximum(s - 1, 0)) if pipelined else mix_blk
    rows_spec = lambda which: pl.BlockSpec((nb, tt, d), lambda s: (which(s) // n_tb, which(s) % n_tb, 0))
    per_seq = lambda rows, which: pl.BlockSpec((nb, rows, d), lambda s: (which(s) // n_tb, 0, 0))
    row = _resident((1, d))
    return pl.pallas_call(
        functools.partial(_odd_kernel, stream_start, pipelined, n_tb),
        grid=(n_blocks + (1 if pipelined else 0),),
        in_specs=[rows_spec(mix_blk), per_seq(N_MOD, mix_blk), per_seq(N_MOD, ffn_blk), row,
                  _resident((d, 2 * d)), _resident((CONV_W, d)),
                  row, _resident((LRU_BLOCKS, LRU_BLK, 2 * LRU_BLK)), row,
                  row, row, _resident((d, d)), per_seq(CONV_W - 1, mix_blk), per_seq(1, mix_blk),
                  row, row] + _ffn_specs(layer),
        out_specs=[rows_spec(ffn_blk), per_seq(CONV_W - 1, mix_blk), per_seq(1, mix_blk)],
        out_shape=[jax.ShapeDtypeStruct((b, t, d), f32),
                   jax.ShapeDtypeStruct((b, CONV_W - 1, d), f32),
                   jax.ShapeDtypeStruct((b, 1, d), f32)],
        scratch_shapes=[pltpu.VMEM((nb, HALO + tt, d), f32), pltpu.VMEM((nb, 1, d), f32),
                        pltpu.VMEM((nb, tt, d) if pipelined else (1, SUBLANES, LANES), f32),
                        pltpu.VMEM((nb * tt, d) if pipelined else (2 * SUBLANES, LANES), bf16)],
        compiler_params=_params("arbitrary"),
        name="odd_layer_ffn",
    )(x, mod, mod, gm, win, cw, cb, wax, ba, bx, lam, wo, conv0, h0, gf, fn, wg, wu, wd)


def _trunk(x, mods, past, weights, nb, tt, hg_tt):
    b, t, d = x.shape
    sample = past is not None
    row = lambda v: v.reshape(1, -1)

    if sample:
        past_k, past_v, s0, conv0, h0 = past
        qa, ka, va, ka_bf, va_bf, qb, lf, kb, ib, gs = _even_pre(
            x, mods[0], row(weights["norm_mix"][0]), weights["w_in_even"], weights["w_kv_t"],
            weights["hg_lb_logits"], 0, nb, tt, False)
        oa = _sb_sample(qa, ka_bf, va_bf, past_k, past_v, LANES)
        ob, s_new = _hgrn(qb, lf, kb, ib, gs, row(weights["hg_gnorm"]), s0, hg_tt)
        x = _even_post(x, mods[0], oa, ob, weights["w_out_even"], row(weights["norm_ffn"][0]),
                       weights["ffn_wg"], weights["ffn_wu"], weights["ffn_wd"], 0, nb, tt)
        heads = lambda a: a.reshape(1, b, t, SB_HEADS, SB_DIM)
    else:
        s0 = jnp.zeros((b, HG_HEADS, HG_DK, HG_DV), f32)
        conv0 = jnp.zeros((b, CONV_W - 1, d), f32)
        h0 = jnp.zeros((b, 1, d), f32)
        qa, ka, va, ka_bf, va_bf, ob, s_new = _even_fused(
            x, mods[0], row(weights["norm_mix"][0]), weights["w_in_even"], weights["w_kv_t"],
            weights["hg_lb_logits"], row(weights["hg_gnorm"]), s0, 0, tt)
        x = _even_post_attn(x, mods[0], ob, qa, ka_bf, va_bf, weights["w_out_even"],
                            row(weights["norm_ffn"][0]), weights["ffn_wg"], weights["ffn_wu"],
                            weights["ffn_wd"], 0, tt)
        heads = lambda a: a.reshape(b, SB_HEADS, SB_DIM, t).transpose(0, 3, 1, 2)[None]

    y, conv_new, h_new = _odd_layer(
        x, mods[1], row(weights["norm_mix"][1]), weights["w_in_odd"], weights["conv_w"],
        row(weights["conv_b"]), weights["lru_wax"], row(weights["lru_ba"]), row(weights["lru_bx"]),
        row(weights["lru_lambda"]), weights["w_out_odd"], conv0, h0,
        row(weights["norm_ffn"][1]), row(weights["final_norm"]),
        weights["ffn_wg"], weights["ffn_wu"], weights["ffn_wd"],
        1, nb, tt, not sample)

    return (y, heads(ka), heads(va), s_new[None], conv_new[None], h_new.reshape(1, b, d))


def kernel(x_prompt, x_sample, cache_sb_k, cache_sb_v, state_hgrn, state_conv, state_lru, c_prompt, c_sample, norm_mix, norm_ffn, w_ada, b_ada, w_in_even, w_out_even, hg_gnorm, hg_lb_logits, w_in_odd, conv_w, conv_b, lru_wa, lru_ba, lru_wx, lru_bx, lru_lambda, w_out_odd, ffn_wg, ffn_wu, ffn_wd, final_norm):
    bp = x_prompt.shape[0]
    bs, ts, d = x_sample.shape
    w_in_even_bf = w_in_even[0].astype(bf16)
    weights = {
        "norm_mix": norm_mix, "norm_ffn": norm_ffn,
        "w_in_even": w_in_even_bf, "w_out_even": w_out_even[0].astype(bf16),
        "w_kv_t": w_in_even_bf[:, SEG:3 * SEG].T.reshape(2, SEG, d),
        "hg_gnorm": hg_gnorm[0], "hg_lb_logits": hg_lb_logits,
        "w_in_odd": w_in_odd[0].astype(bf16), "conv_w": conv_w[0], "conv_b": conv_b[0],
        "lru_wax": jnp.concatenate([lru_wa[0], lru_wx[0]], axis=-1).astype(bf16),
        "lru_ba": lru_ba[0], "lru_bx": lru_bx[0], "lru_lambda": lru_lambda[0],
        "w_out_odd": w_out_odd[0].astype(bf16),
        "ffn_wg": ffn_wg.astype(bf16), "ffn_wu": ffn_wu.astype(bf16), "ffn_wd": ffn_wd.astype(bf16),
        "final_norm": final_norm,
    }
    mods = _ada(jnp.concatenate([c_prompt, c_sample], axis=0), w_ada, b_ada)
    mods = mods.reshape(DEPTH, bp + bs, N_MOD, d)

    out_p = _trunk(x_prompt, mods[:, :bp], None, weights, nb=1, tt=512, hg_tt=512)
    cache_t = lambda c: c[0].transpose(0, 2, 3, 1).reshape(bs, SB_WIDTH, -1)
    past = (cache_t(cache_sb_k), cache_t(cache_sb_v),
            state_hgrn[0], state_conv[0], state_lru[0].reshape(bs, 1, d))
    out_s = _trunk(x_sample, mods[:, bp:], past, weights, nb=bs, tt=ts, hg_tt=ts)

    y_p, k_p, v_p, s_p, conv_p, h_p = out_p
    y_s, k_s, v_s, s_s, conv_s, h_s = out_s
    return (y_p, y_s, k_p, v_p, s_p, conv_p, h_p, k_s, v_s, s_s, conv_s, h_s)
```

```python
import functools
import math

import jax
import jax.numpy as jnp
from jax import lax
from jax.experimental import pallas as pl
from jax.experimental.pallas import tpu as pltpu

f32 = jnp.float32
bf16 = jnp.bfloat16

D_MODEL = 1024
DEPTH = 2
EPS = 1e-6
N_MOD = 6

SB_HEADS = 8
SB_DIM = 64
SB_WIDTH = SB_HEADS * SB_DIM
HG_HEADS = 4
HG_DK = 128
HG_DV = 128
HG_W = HG_HEADS * HG_DK
SEG = 512
N_SEG = 7
HG_CHUNK = 64
HG_SUB = 32
HG_MATMUL_MIN_LOG = -40.0
LRU_BLOCKS = 8
LRU_BLK = D_MODEL // LRU_BLOCKS
CONV_W = 4
C_SCALE = 8.0
D_FF = 2816

SB_WINDOW = 2
SB_SKIP_LOG = -105.0

MXU_COLS = 256

LANES = 128
SUBLANES = 8
HALO = 8
VMEM_LIMIT = 60 * 1024 * 1024

NT_DIMS = (((1,), (1,)), ((), ()))
TN_DIMS = (((0,), (0,)), ((), ()))


def _dot(a, b):
    return jnp.dot(a, b, preferred_element_type=f32)


def _dot_nt(a, b):
    return lax.dot_general(a, b, NT_DIMS, preferred_element_type=f32)


def _split3(x):
    hi = x.astype(bf16)
    r1 = x - hi.astype(f32)
    mid = r1.astype(bf16)
    lo = (r1 - mid.astype(f32)).astype(bf16)
    return hi, mid, lo


def _split2(x):
    hi = x.astype(bf16)
    return hi, (x - hi.astype(f32)).astype(bf16)


def _sigmoid(x):
    return jax.nn.sigmoid(x)


def _silu(x):
    return x * _sigmoid(x)


def _norm_mod(x, g, scale, shift):
    ms = jnp.mean(x * x, axis=-1, keepdims=True)
    return x * lax.rsqrt(ms + EPS) * (g * (1.0 + scale)) + shift


def _params(*sem):
    return pltpu.CompilerParams(dimension_semantics=sem, vmem_limit_bytes=VMEM_LIMIT)


def _ada_kernel(c_ref, w_ref, b_ref, o_ref):
    c = c_ref[...]
    sc = _silu(c)
    w = w_ref[...]
    s_hi = sc.astype(bf16)
    s_lo = (sc - s_hi.astype(f32)).astype(bf16)
    w_hi = w.astype(bf16)
    w_lo = (w - w_hi.astype(f32)).astype(bf16)
    o_ref[...] = _dot(s_hi, w_hi) + _dot(s_lo, w_hi) + _dot(s_hi, w_lo) + b_ref[...]


def _ada(c_all, w_ada, b_ada):
    rows = c_all.shape[0]
    n_out = N_MOD * D_MODEL
    tn = 1536
    return pl.pallas_call(
        _ada_kernel,
        grid=(DEPTH, n_out // tn),
        in_specs=[
            pl.BlockSpec((rows, D_MODEL), lambda l, j: (0, 0)),
            pl.BlockSpec((None, D_MODEL, tn), lambda l, j: (l, 0, j)),
            pl.BlockSpec((None, 1, tn), lambda l, j: (l, 0, j)),
        ],
        out_specs=pl.BlockSpec((None, rows, tn), lambda l, j: (l, 0, j)),
        out_shape=jax.ShapeDtypeStruct((DEPTH, rows, n_out), f32),
        compiler_params=_params("arbitrary", "arbitrary"),
        name="ada_mod",
    )(c_all, w_ada, b_ada.reshape(DEPTH, 1, n_out))


def _even_pre_kernel(layer, x_ref, mod_ref, g_ref, w_ref, lbl_ref,
                     qa_ref, ka_ref, va_ref, kab_ref, vab_ref,
                     qb_ref, lf_ref, kb_ref, ib_ref, gs_ref):
    nb, tt, d = x_ref.shape
    m = nb * tt
    hn = _norm_mod(x_ref[...], g_ref[...], mod_ref[:, 1:2, :], mod_ref[:, 0:1, :])
    hn = hn.reshape(m, d).astype(bf16)

    def seg(i):
        return _dot(hn, w_ref[:, i * SEG:(i + 1) * SEG]).reshape(nb, tt, SEG)

    qa_ref[...] = (seg(0) * (1.0 / math.sqrt(SB_DIM))).astype(bf16)
    ka = seg(1)
    ka_ref[...] = ka
    kab_ref[...] = ka.astype(bf16)
    va = seg(2)
    va_ref[...] = va
    vab_ref[...] = va.astype(bf16)
    qb_ref[...] = _silu(seg(3)) * (HG_DK ** -0.5)

    lg = lbl_ref[...]
    e = jnp.exp(lg - jnp.max(lg, axis=0, keepdims=True))
    lb = jnp.sum(e[:layer + 1], axis=0, keepdims=True) / jnp.sum(e, axis=0, keepdims=True)
    fb = seg(4)
    f = lb + (1.0 - lb) * _sigmoid(fb)
    lf_ref[...] = jnp.log(f)
    kb_ref[...] = (1.0 - lb) * _sigmoid(-fb)
    ib_ref[...] = seg(5)
    gs_ref[...] = _silu(seg(6))


def _even_pre(x, mod, g, w_bf, lb_logits, layer, nb, tt):
    b, t, d = x.shape
    grid = (b // nb, t // tt)
    blk = lambda width: pl.BlockSpec((nb, tt, width), lambda i, j: (i, j, 0))
    out_f = jax.ShapeDtypeStruct((b, t, SEG), f32)
    out_b = jax.ShapeDtypeStruct((b, t, SEG), bf16)
    return pl.pallas_call(
        functools.partial(_even_pre_kernel, layer),
        grid=grid,
        in_specs=[
            blk(d),
            pl.BlockSpec((nb, N_MOD, d), lambda i, j: (i, 0, 0)),
            _resident((1, d)),
            _resident((d, N_SEG * SEG)),
            _resident((DEPTH + 1, HG_W)),
        ],
        out_specs=[blk(SEG)] * 10,
        out_shape=[out_b, out_f, out_f, out_b, out_b, out_f, out_f, out_f, out_f, out_f],
        compiler_params=_params("arbitrary", "arbitrary"),
        name="even_pre",
    )(x, mod, g, w_bf, lb_logits)


def _sb_consts(tk, scale=1.0):
    r = lax.broadcasted_iota(jnp.int32, (tk, LANES + tk), 0)
    c = lax.broadcasted_iota(jnp.int32, (tk, LANES + tk), 1)
    m = jnp.where((c < LANES) | (r > c - LANES), scale, 0.0).astype(bf16)
    return jnp.concatenate([m, m], axis=0)


def _sb_groups_units(groups, out):
    zs = [[_dot(q2, k_blk) if transposed else _dot_nt(q2, k_blk)
           for k_blk, _, _, _, transposed in blocks] for q2, blocks, _, _ in groups]
    yield
    l1mbs, sums = [], []
    for g, (_, blocks, _, _) in enumerate(groups):
        l1mbs.append([])
        sums.append([])
        for z, (_, _, sum_matrix, mask, _) in zip(zs[g], blocks):
            l1mb = -(jnp.maximum(z, 0.0) + jnp.log(1.0 + jnp.exp(-jnp.abs(z))))
            if mask is not None:
                l1mb = jnp.where(mask, l1mb, 0.0)
            l1mbs[g].append(l1mb)
            sums[g].append(_dot(jnp.concatenate(_split2(l1mb), axis=1), sum_matrix))
    yield
    results = []
    for g, (_, blocks, carry, acc) in enumerate(groups):
        for z, l1mb, s, (_, v_blk, _, mask, transposed) in zip(zs[g], l1mbs[g], sums[g], blocks):
            tk = z.shape[1]
            after = carry[:, :tk] + s[:, LANES:]
            w = jnp.exp(z + l1mb + after)
            if mask is not None:
                w = jnp.where(mask, w, 0.0)
            w = w.astype(bf16)
            acc = acc + (_dot_nt(w, v_blk) if transposed else _dot(w, v_blk))
            carry = carry + s[:, :LANES]
        results.append((carry, acc))
    out["results"] = results


def _sb_groups(groups):
    out = {}
    for _ in _sb_groups_units(groups, out):
        pass
    return out["results"]


def _sb_blocks(q2, blocks, carry, acc):
    return _sb_groups([(q2, blocks, carry, acc)])[0]


def _sb_finish(j0, carry, acc, step):
    def cond(state):
        j, carry, _ = state
        return (j >= 0) & (jnp.max(carry) > SB_SKIP_LOG)

    def body(state):
        j, carry, acc = state
        carry, acc = step(j, carry, acc)
        return j - 1, carry, acc

    return lax.while_loop(cond, body, (j0, carry, acc))[2]


def _stack_heads(q):
    lane = lax.broadcasted_iota(jnp.int32, q.shape, 1)
    zero = jnp.zeros_like(q)
    return jnp.concatenate([jnp.where(lane < SB_DIM, q, zero), jnp.where(lane >= SB_DIM, q, zero)], axis=0)


def _unstack_heads(acc2):
    tq = acc2.shape[0] // 2
    lane = lax.broadcasted_iota(jnp.int32, (tq, LANES), 1)
    return jnp.where(lane < SB_DIM, acc2[:tq], acc2[tq:])


def _causal_mask2(tq):
    r = lax.broadcasted_iota(jnp.int32, (2 * tq, tq), 0) & (tq - 1)
    c = lax.broadcasted_iota(jnp.int32, (2 * tq, tq), 1)
    return c < r


def _sb_prompt_groups(qs, k_of, v_of, i0, consts):
    sum_matrix, diag_mask, zero = consts
    tq = qs[0].shape[0]
    groups = []
    for g, q in enumerate(qs):
        i = i0 + g
        blocks = [(k_of(i), v_of(i), sum_matrix, diag_mask, True)]
        for c in range(1, SB_WINDOW + 1):
            scale = jnp.where(i >= c, 1.0, 0.0)
            j = jnp.maximum(i - c, 0)
            blocks.append((k_of(j), (v_of(j).astype(f32) * scale).astype(bf16), _sb_consts(tq, scale), None, True))
        groups.append((_stack_heads(q), blocks, zero, zero))
    return groups


def _sb_prompt_consts(tq):
    return _sb_consts(tq), _causal_mask2(tq), jnp.zeros((2 * tq, LANES), f32)


def _sb_prompt_finish(group, result, i, k_of, v_of, consts):
    q2 = group[0]
    carry, acc = result

    def step(j, carry, acc):
        return _sb_blocks(q2, [(k_of(j), v_of(j), consts[0], None, True)], carry, acc)

    return _unstack_heads(_sb_finish(i - SB_WINDOW - 1, carry, acc, step))


def _sb_sample_kernel(tk, n_past, q_ref, k_ref, v_ref, wk_ref, wv_ref, pk_hbm, pv_hbm, o_ref,
                      kbuf, vbuf, sem):
    tq = q_ref.shape[0]
    b = pl.program_id(0)
    zero = jnp.zeros((2 * tq, LANES), f32)
    sum_matrix = _sb_consts(tk)
    diag_matrix = _sb_consts(tq)
    diag_mask = _causal_mask2(tq)

    def fetch(j):
        keys = pl.ds(pl.multiple_of(j * tk, tk), tk)
        copies = [pltpu.make_async_copy(pk_hbm.at[b, :, keys], kbuf, sem.at[0]),
                  pltpu.make_async_copy(pv_hbm.at[b, :, keys], vbuf, sem.at[1])]
        for cp in copies:
            cp.start()
        for cp in copies:
            cp.wait()

    groups = []
    for pair in range(SB_WIDTH // LANES):
        cols = slice(pair * LANES, (pair + 1) * LANES)
        q2 = _stack_heads(q_ref[:, cols])
        blocks = [(k_ref[:, cols], v_ref[:, cols], diag_matrix, diag_mask, False)]
        for c in range(1, SB_WINDOW + 1):
            keys = slice((SB_WINDOW - c) * tk, (SB_WINDOW - c + 1) * tk)
            blocks.append((wk_ref[cols, keys].astype(bf16), wv_ref[cols, keys].astype(bf16),
                           sum_matrix, None, True))
        groups.append((q2, blocks, zero, zero))

    for pair, (carry, acc) in enumerate(_sb_groups(groups)):
        cols = slice(pair * LANES, (pair + 1) * LANES)
        q2 = groups[pair][0]

        def step(j, carry, acc, cols=cols, q2=q2):
            fetch(j)
            blk = (kbuf[cols, :].astype(bf16), vbuf[cols, :].astype(bf16), sum_matrix, None, True)
            return _sb_blocks(q2, [blk], carry, acc)

        acc = _sb_finish(n_past - SB_WINDOW - 1, carry, acc, step)
        o_ref[:, cols] = _unstack_heads(acc).astype(o_ref.dtype)


def _sb_sample(q, k, v, past_kt, past_vt, tk):
    b, t, _ = q.shape
    n_past = past_kt.shape[2] // tk
    assert n_past % SB_WINDOW == 0 and n_past >= SB_WINDOW
    qspec = pl.BlockSpec((None, t, SB_WIDTH), lambda bi: (bi, 0, 0))
    wspec = pl.BlockSpec((None, SB_WIDTH, SB_WINDOW * tk), lambda bi: (bi, 0, n_past // SB_WINDOW - 1))
    hbm = pl.BlockSpec(memory_space=pl.ANY)
    return pl.pallas_call(
        functools.partial(_sb_sample_kernel, tk, n_past),
        grid=(b,),
        in_specs=[qspec, qspec, qspec, wspec, wspec, hbm, hbm],
        out_specs=qspec,
        out_shape=jax.ShapeDtypeStruct((b, t, SB_WIDTH), bf16),
        scratch_shapes=[pltpu.VMEM((SB_WIDTH, tk), f32), pltpu.VMEM((SB_WIDTH, tk), f32),
                        pltpu.SemaphoreType.DMA((2,))],
        compiler_params=_params("arbitrary"),
        name="sb_sample",
    )(q, k, v, past_kt, past_vt, past_kt, past_vt)


def _hgrn_head_chunk(q, k, v, b, st):
    n_sub = HG_CHUNK // HG_SUB
    v_bf = v.astype(bf16)
    o_inter = _dot_nt((q * jnp.exp(b)).astype(bf16), st.astype(bf16))
    t_idx = lax.broadcasted_iota(jnp.int32, (HG_SUB, HG_SUB, HG_DK), 0)
    s_idx = lax.broadcasted_iota(jnp.int32, (HG_SUB, HG_SUB, HG_DK), 1)
    causal = s_idx <= t_idx
    outs = []
    for i in range(n_sub):
        lo = i * HG_SUB
        rows = slice(lo, lo + HG_SUB)
        qi, ki, bi = q[rows], k[rows], b[rows]
        diff = bi[:, None, :] - bi[None, :, :]
        dec = jnp.exp(jnp.where(causal, diff, -1e30))
        a_diag = jnp.sum(qi[:, None, :] * ki[None, :, :] * dec, axis=-1)
        o_i = _dot(a_diag.astype(bf16), v_bf[rows])
        if i > 0:
            ref = b[lo - 1:lo]
            q_t = qi * jnp.exp(bi - ref)
            k_t = k[:lo] * jnp.exp(ref - b[:lo])
            a_off = _dot_nt(q_t.astype(bf16), k_t.astype(bf16))
            o_i = o_i + _dot(a_off.astype(bf16), v_bf[:lo])
        outs.append(o_inter[rows] + o_i)
    o = jnp.concatenate(outs, axis=0)
    b_last = b[HG_CHUNK - 1:HG_CHUNK]
    k_dec = (k * jnp.exp(b_last - b)).astype(bf16)
    st_new = jnp.exp(b_last) * st + lax.dot_general(v_bf, k_dec, TN_DIMS, preferred_element_type=f32)
    return o, st_new


def _hgrn_intra_units(qs, ks, vs, bs, out):
    n_sub = HG_CHUNK // HG_SUB
    heads = range(len(qs))
    q_in, q_t, k_t, k_dec, v_bf, b_last = [], [], [], [], [], []
    for h in heads:
        q, k, b = qs[h], ks[h], bs[h]
        q_in.append((q * jnp.exp(b)).astype(bf16))
        q_t.append([])
        k_t.append([])
        for i in range(n_sub):
            lo, hi = i * HG_SUB, (i + 1) * HG_SUB
            if i == 0:
                q_t[h].append(q_in[h][:hi])
                k_t[h].append((k[:hi] * jnp.exp(-b[:hi])).astype(bf16))
            else:
                ref = b[lo - 1:lo]
                q_t[h].append((q[lo:hi] * jnp.exp(b[lo:hi] - ref)).astype(bf16))
                k_t[h].append((k[:hi] * jnp.exp(ref - b[:hi])).astype(bf16))
        b_last.append(b[HG_CHUNK - 1:HG_CHUNK])
        k_dec.append((k * jnp.exp(b_last[h] - b)).astype(bf16))
        v_bf.append(vs[h].astype(bf16))
    yield

    attn = [[_dot_nt(q_t[h][i], k_t[h][i]) for i in range(n_sub)] for h in heads]
    st_add = [lax.dot_general(v_bf[h], k_dec[h], TN_DIMS, preferred_element_type=f32) for h in heads]
    yield

    o_intra = []
    for h in heads:
        pieces = []
        for i in range(n_sub):
            hi = (i + 1) * HG_SUB
            r = lax.broadcasted_iota(jnp.int32, (HG_SUB, hi), 0)
            c = lax.broadcasted_iota(jnp.int32, (HG_SUB, hi), 1)
            a = jnp.where(c - i * HG_SUB <= r, attn[h][i], 0.0)
            pieces.append(_dot(a.astype(bf16), v_bf[h][:hi]))
        o_intra.append(jnp.concatenate(pieces, axis=0))
    out.update(o_intra=o_intra, q_in=q_in, b_last=b_last, st_add=st_add)


def _hgrn_apply_state(intra, sts):
    heads = range(len(sts))
    outs = [intra["o_intra"][h] + _dot_nt(intra["q_in"][h], sts[h].astype(bf16)) for h in heads]
    new_sts = [jnp.exp(intra["b_last"][h]) * sts[h] + intra["st_add"][h] for h in heads]
    return outs, new_sts


def _hgrn_chunk_matmul(qs, ks, vs, bs, sts):
    intra = {}
    for _ in _hgrn_intra_units(qs, ks, vs, bs, intra):
        pass
    return _hgrn_apply_state(intra, sts)


def _hgrn_kernel(q_ref, lf_ref, k_ref, v_ref, gs_ref, gn_ref, s0_ref, o_ref, sout_ref, st_scr):
    tb = pl.program_id(1)
    n_tb = pl.num_programs(1)
    n_chunk = q_ref.shape[0] // HG_CHUNK
    n_sub = HG_CHUNK // HG_SUB

    @pl.when(tb == 0)
    def _():
        for h in range(HG_HEADS):
            st_scr[h] = s0_ref[h].T

    r = lax.broadcasted_iota(jnp.int32, (HG_CHUNK, HG_CHUNK), 0)
    c = lax.broadcasted_iota(jnp.int32, (HG_CHUNK, HG_CHUNK), 1)
    lower = jnp.where(c <= r, 1.0, 0.0).astype(bf16)
    head_cols = [slice(h * HG_DK, (h + 1) * HG_DK) for h in range(HG_HEADS)]

    def chunk(ci, carry):
        rows = pl.ds(pl.multiple_of(ci * HG_CHUNK, HG_CHUNK), HG_CHUNK)
        b_all = _dot_exact_rhs_lhs(lower, lf_ref[rows, :])

        def finish(h, o, st_new):
            cols = head_cols[h]
            st_scr[h] = st_new
            o = o * lax.rsqrt(jnp.mean(o * o, axis=-1, keepdims=True) + EPS)
            o_ref[rows, cols] = (o * gn_ref[:, cols] * gs_ref[rows, cols]).astype(o_ref.dtype)

        ends = [b_all[(i + 1) * HG_SUB - 1:(i + 1) * HG_SUB, :] for i in range(n_sub)]
        span = ends[0]
        for i in range(1, n_sub):
            span = jnp.minimum(span, ends[i] - ends[i - 1])
        bounded = jnp.min(span) >= HG_MATMUL_MIN_LOG

        @pl.when(bounded)
        def _():
            outs, new_sts = _hgrn_chunk_matmul(
                [q_ref[rows, cols] for cols in head_cols], [k_ref[rows, cols] for cols in head_cols],
                [v_ref[rows, cols] for cols in head_cols], [b_all[:, cols] for cols in head_cols],
                [st_scr[h] for h in range(HG_HEADS)])
            for h in range(HG_HEADS):
                finish(h, outs[h], new_sts[h])

        @pl.when(jnp.logical_not(bounded))
        def _():
            for h, cols in enumerate(head_cols):
                o, st_new = _hgrn_head_chunk(q_ref[rows, cols], k_ref[rows, cols], v_ref[rows, cols],
                                             b_all[:, cols], st_scr[h])
                finish(h, o, st_new)

        return carry

    lax.fori_loop(0, n_chunk, chunk, 0)

    @pl.when(tb == n_tb - 1)
    def _():
        for h in range(HG_HEADS):
            sout_ref[h] = st_scr[h].T


def _dot_exact_rhs_lhs(m, x):
    hi, mid, lo = _split3(x)
    return _dot(m, hi) + _dot(m, mid) + _dot(m, lo)


def _hgrn(qb, lf, kb, ib, gs, gnorm, s0, tt):
    b, t, _ = qb.shape
    blk = pl.BlockSpec((None, tt, HG_W), lambda i, j: (i, j, 0))
    sspec = pl.BlockSpec((None, HG_HEADS, HG_DK, HG_DV), lambda i, j: (i, 0, 0, 0))
    return pl.pallas_call(
        _hgrn_kernel,
        grid=(b, t // tt),
        in_specs=[blk, blk, blk, blk, blk, pl.BlockSpec((1, HG_W), lambda i, j: (0, 0)), sspec],
        out_specs=[blk, sspec],
        out_shape=[jax.ShapeDtypeStruct((b, t, HG_W), bf16),
                   jax.ShapeDtypeStruct((b, HG_HEADS, HG_DK, HG_DV), f32)],
        scratch_shapes=[pltpu.VMEM((HG_HEADS, HG_DV, HG_DK), f32)],
        compiler_params=_params("arbitrary", "arbitrary"),
        name="hgrn",
    )(qb, lf, kb, ib, gs, gnorm, s0)


def _round_robin(*gens, units=None):
    turns = list(zip(gens, units or [1] * len(gens)))
    while turns:
        turns = [(gen, n) for gen, n in turns if all(next(gen, "done") != "done" for _ in range(n))]


def _even_fused_kernel(layer, n_tb, x_ref, mod_ref, g_ref, w_ref, wkvt_ref, lbl_ref, gn_ref, s0_ref,
                       qa_ref, ka_ref, va_ref, kab_ref, vab_ref, ob_ref, sout_ref,
                       q_scr, b_scr, k_scr, v_scr, gs_scr, span_scr, st_scr):
    _, tt, d = x_ref.shape
    s = pl.program_id(0)
    hg_blk = jnp.maximum(s - 1, 0)
    tb_h = lax.rem(hg_blk, n_tb)
    n_chunk = tt // HG_CHUNK
    n_sub = HG_CHUNK // HG_SUB
    head_cols = [slice(h * HG_DK, (h + 1) * HG_DK) for h in range(HG_HEADS)]
    operand_scr = (q_scr, b_scr, k_scr, v_scr, gs_scr)

    @pl.when(s == 0)
    def _():
        for scr in operand_scr + (span_scr,):
            scr[...] = jnp.zeros_like(scr)

    @pl.when(tb_h == 0)
    def _():
        for h in range(HG_HEADS):
            st_scr[h] = s0_ref[h].T

    r = lax.broadcasted_iota(jnp.int32, (HG_CHUNK, HG_CHUNK), 0)
    c = lax.broadcasted_iota(jnp.int32, (HG_CHUNK, HG_CHUNK), 1)
    lower = jnp.where(c <= r, 1.0, 0.0).astype(bf16)

    def proj_units(out):
        hn = _norm_mod(x_ref[...], g_ref[...], mod_ref[:, 1:2, :], mod_ref[:, 0:1, :])
        hn = hn.reshape(tt, d).astype(bf16)
        yield

        def seg(i):
            return _dot(hn, w_ref[:, i * SEG:(i + 1) * SEG])

        qa_ref[...] = (seg(0) * (1.0 / math.sqrt(SB_DIM))).astype(bf16).reshape(1, tt, SEG)
        yield
        for i, (full_ref, blocks_ref) in enumerate(((ka_ref, kab_ref), (va_ref, vab_ref))):
            kt = _dot_nt(wkvt_ref[i], hn)
            full_ref[...] = kt
            for cb in range(tt // LANES):
                blocks_ref[cb] = kt[:, cb * LANES:(cb + 1) * LANES].astype(bf16)
            yield
        q = _silu(seg(3)) * (HG_DK ** -0.5)
        yield
        lg = lbl_ref[...]
        e = jnp.exp(lg - jnp.max(lg, axis=0, keepdims=True))
        lb = jnp.sum(e[:layer + 1], axis=0, keepdims=True) / jnp.sum(e, axis=0, keepdims=True)
        fb = seg(4)
        lf = jnp.log(lb + (1.0 - lb) * _sigmoid(fb))
        k = (1.0 - lb) * _sigmoid(-fb)
        yield
        b = jnp.concatenate([_dot_exact_rhs_lhs(lower, lf[ci * HG_CHUNK:(ci + 1) * HG_CHUNK])
                             for ci in range(n_chunk)], axis=0)
        yield
        v = seg(5)
        yield
        gs = _silu(seg(6))
        span = None
        for ci in range(n_chunk):
            prev = None
            for i in range(n_sub):
                row = ci * HG_CHUNK + (i + 1) * HG_SUB - 1
                end = b[row:row + 1, :]
                this = end if prev is None else end - prev
                span = this if span is None else jnp.minimum(span, this)
                prev = end
        out.update(q=q, b=b, k=k, v=v, gs=gs, span=span)

    def store_operands(out):
        for scr, name in zip(operand_scr, ("q", "b", "k", "v", "gs")):
            scr[...] = out[name]
        span_scr[0:1, :] = out["span"]

    def finish(rows, h, o):
        cols = head_cols[h]
        o = o * lax.rsqrt(jnp.mean(o * o, axis=-1, keepdims=True) + EPS)
        ob_ref[0, rows, cols] = (o * gn_ref[:, cols] * gs_scr[rows, cols]).astype(ob_ref.dtype)

    def hgrn_units():
        chunk_rows = [slice(ci * HG_CHUNK, (ci + 1) * HG_CHUNK) for ci in range(n_chunk)]
        intras = [{} for _ in chunk_rows]
        stages = [_hgrn_intra_units(
            [q_scr[rows, cols] for cols in head_cols], [k_scr[rows, cols] for cols in head_cols],
            [v_scr[rows, cols] for cols in head_cols], [b_scr[rows, cols] for cols in head_cols], intra)
            for rows, intra in zip(chunk_rows, intras)]
        while stages:
            unfinished = []
            for gen in stages:
                if next(gen, "done") != "done":
                    unfinished.append(gen)
                yield
            stages = unfinished
        sts = [st_scr[h] for h in range(HG_HEADS)]
        for rows, intra in zip(chunk_rows, intras):
            outs, sts = _hgrn_apply_state(intra, sts)
            for h in range(HG_HEADS):
                finish(rows, h, outs[h])
            yield
        for h in range(HG_HEADS):
            st_scr[h] = sts[h]

    bounded = jnp.min(span_scr[0:1, :]) >= HG_MATMUL_MIN_LOG

    @pl.when(bounded)
    def _():
        res = {}
        _round_robin(hgrn_units(), proj_units(res), units=[3, 1])
        store_operands(res)

    @pl.when(jnp.logical_not(bounded))
    def _():
        def chunk(ci, carry):
            rows = pl.ds(pl.multiple_of(ci * HG_CHUNK, HG_CHUNK), HG_CHUNK)
            for h, cols in enumerate(head_cols):
                o, st_new = _hgrn_head_chunk(q_scr[rows, cols], k_scr[rows, cols], v_scr[rows, cols],
                                             b_scr[rows, cols], st_scr[h])
                st_scr[h] = st_new
                finish(rows, h, o)
            return carry

        lax.fori_loop(0, n_chunk, chunk, 0)
        res = {}
        _round_robin(proj_units(res))
        store_operands(res)

    @pl.when((tb_h == n_tb - 1) & (s > 0))
    def _():
        for h in range(HG_HEADS):
            sout_ref[h] = st_scr[h].T


def _even_fused(x, mod, g, w_bf, wkvt_bf, lb_logits, gnorm, s0, layer, tt):
    b, t, d = x.shape
    n_tb = t // tt
    n_blocks = b * n_tb
    proj_blk = lambda s: jnp.minimum(s, n_blocks - 1)
    hgrn_blk = lambda s: jnp.maximum(s - 1, 0)
    rows = lambda width, which: pl.BlockSpec((1, tt, width), lambda s: (which(s) // n_tb, which(s) % n_tb, 0))
    operand = pltpu.VMEM((tt, HG_W), f32)
    return pl.pallas_call(
        functools.partial(_even_fused_kernel, layer, n_tb),
        grid=(n_blocks + 1,),
        in_specs=[
            rows(d, proj_blk),
            pl.BlockSpec((1, N_MOD, d), lambda s: (proj_blk(s) // n_tb, 0, 0)),
            _resident((1, d)),
            _resident((d, N_SEG * SEG)),
            _resident((2, SEG, d)),
            _resident((DEPTH + 1, HG_W)),
            _resident((1, HG_W)),
            pl.BlockSpec((None, HG_HEADS, HG_DK, HG_DV), lambda s: (hgrn_blk(s) // n_tb, 0, 0, 0)),
        ],
        out_specs=[
            rows(SEG, proj_blk),
            pl.BlockSpec((None, SEG, tt), lambda s: (proj_blk(s) // n_tb, 0, proj_blk(s) % n_tb)),
            pl.BlockSpec((None, SEG, tt), lambda s: (proj_blk(s) // n_tb, 0, proj_blk(s) % n_tb)),
            pl.BlockSpec((None, tt // LANES, SEG, LANES), lambda s: (proj_blk(s) // n_tb, proj_blk(s) % n_tb, 0, 0)),
            pl.BlockSpec((None, tt // LANES, SEG, LANES), lambda s: (proj_blk(s) // n_tb, proj_blk(s) % n_tb, 0, 0)),
            rows(HG_W, hgrn_blk),
            pl.BlockSpec((None, HG_HEADS, HG_DK, HG_DV), lambda s: (hgrn_blk(s) // n_tb, 0, 0, 0)),
        ],
        out_shape=[
            jax.ShapeDtypeStruct((b, t, SEG), bf16),
            jax.ShapeDtypeStruct((b, SEG, t), f32), jax.ShapeDtypeStruct((b, SEG, t), f32),
            jax.ShapeDtypeStruct((b, t // LANES, SEG, LANES), bf16),
            jax.ShapeDtypeStruct((b, t // LANES, SEG, LANES), bf16),
            jax.ShapeDtypeStruct((b, t, HG_W), bf16),
            jax.ShapeDtypeStruct((b, HG_HEADS, HG_DK, HG_DV), f32),
        ],
        scratch_shapes=[operand] * 5 + [pltpu.VMEM((SUBLANES, HG_W), f32),
                                        pltpu.VMEM((HG_HEADS, HG_DV, HG_DK), f32)],
        compiler_params=_params("arbitrary"),
        name="even_pre_hgrn",
    )(x, mod, g, w_bf, wkvt_bf, lb_logits, gnorm, s0)


def _ffn(hn_bf, wg_ref, wu_ref, wd_ref):
    g = _dot(hn_bf, wg_ref[...])
    u = _dot(hn_bf, wu_ref[...])
    return _dot((_silu(g) * u).astype(bf16), wd_ref[...])


def _resident(shape):
    return pl.BlockSpec(shape, lambda *_: (0,) * len(shape), pipeline_mode=pl.Buffered(1))


def _ffn_specs(layer):
    one = lambda rows, cols: pl.BlockSpec((None, rows, cols), lambda *_: (layer, 0, 0),
                                          pipeline_mode=pl.Buffered(1))
    return [one(D_MODEL, D_FF), one(D_MODEL, D_FF), one(D_FF, D_MODEL)]


def _even_post_kernel(x_ref, mod_ref, oa_ref, ob_ref, wo_ref, gf_ref, wg_ref, wu_ref, wd_ref, out_ref):
    nb, tt, d = x_ref.shape
    m = nb * tt
    o = (_dot(oa_ref[...].reshape(m, SB_WIDTH), wo_ref[:SB_WIDTH, :])
         + _dot(ob_ref[...].reshape(m, HG_W), wo_ref[SB_WIDTH:, :]))
    x1 = x_ref[...] + (1.0 + mod_ref[:, 2:3, :]) * o.reshape(nb, tt, d)
    hn = _norm_mod(x1, gf_ref[...], mod_ref[:, 4:5, :], mod_ref[:, 3:4, :])
    y = _ffn(hn.reshape(m, d).astype(bf16), wg_ref, wu_ref, wd_ref)
    out_ref[...] = x1 + (1.0 + mod_ref[:, 5:6, :]) * y.reshape(nb, tt, d)


def _even_post(x, mod, oa, ob, wo, gf, wg, wu, wd, layer, nb, tt):
    b, t, d = x.shape
    blk = lambda width: pl.BlockSpec((nb, tt, width), lambda i, tb: (i, tb, 0))
    return pl.pallas_call(
        _even_post_kernel,
        grid=(b // nb, t // tt),
        in_specs=[blk(d), pl.BlockSpec((nb, N_MOD, d), lambda i, tb: (i, 0, 0)),
                  blk(SB_WIDTH), blk(HG_W), _resident((d, d)), _resident((1, d))] + _ffn_specs(layer),
        out_specs=blk(d),
        out_shape=jax.ShapeDtypeStruct((b, t, d), f32),
        compiler_params=_params("arbitrary", "arbitrary"),
        name="even_post_ffn",
    )(x, mod, oa, ob, wo, gf, wg, wu, wd)


def _even_post_attn_kernel(n_tb, x_ref, mod_ref, ob_ref, q_ref, k_ref, v_ref, wo_ref, gf_ref,
                           wg_ref, wu_ref, wd_ref, out_ref, oa_scr):
    _, tt, d = x_ref.shape
    tq = LANES
    n_q = tt // tq
    n_pair = SB_WIDTH // LANES
    s = pl.program_id(0)
    attn_blk = jnp.minimum(s, pl.num_programs(0) - 2)
    i0 = lax.rem(attn_blk, n_tb) * n_q
    consts = _sb_prompt_consts(tq)
    pair_cols = [slice(p * LANES, (p + 1) * LANES) for p in range(n_pair)]
    k_of = [lambda j, cols=cols: k_ref[j, cols, :] for cols in pair_cols]
    v_of = [lambda j, cols=cols: v_ref[j, cols, :] for cols in pair_cols]

    @pl.when(s == 0)
    def _():
        oa_scr[...] = jnp.zeros_like(oa_scr)

    def post_units(oa):
        o = _dot(oa, wo_ref[:SB_WIDTH, :])
        yield
        o = o + _dot(ob_ref[...].reshape(tt, HG_W), wo_ref[SB_WIDTH:, :])
        yield
        x1 = x_ref[...] + (1.0 + mod_ref[:, 2:3, :]) * o.reshape(1, tt, d)
        hn = _norm_mod(x1, gf_ref[...], mod_ref[:, 4:5, :], mod_ref[:, 3:4, :])
        hn = hn.reshape(tt, d).astype(bf16)
        yield
        acts = []
        for j in range(D_FF // MXU_COLS):
            cols = slice(j * MXU_COLS, (j + 1) * MXU_COLS)
            g = _dot(hn, wg_ref[:, cols])
            u = _dot(hn, wu_ref[:, cols])
            acts.append((_silu(g) * u).astype(bf16))
            yield
        act = jnp.concatenate(acts, axis=1)
        for n in range(d // MXU_COLS):
            cols = slice(n * MXU_COLS, (n + 1) * MXU_COLS)
            y = _dot(act, wd_ref[:, cols]).reshape(1, tt, MXU_COLS)
            out_ref[:, :, cols] = x1[:, :, cols] + (1.0 + mod_ref[:, 5:6, cols]) * y
            yield

    def attn_units(out):
        for p, cols in enumerate(pair_cols):
            qs = [q_ref[0, g * tq:(g + 1) * tq, cols] for g in range(n_q)]
            groups = _sb_prompt_groups(qs, k_of[p], v_of[p], i0, consts)
            res = {}
            yield from _sb_groups_units(groups, res)
            out[p] = res["results"]
            yield

    results = {}
    _round_robin(post_units(oa_scr[...]), attn_units(results), units=[3, 2])

    reach = None
    for p, cols in enumerate(pair_cols):
        for g, (carry, acc) in enumerate(results[p]):
            oa_scr[g * tq:(g + 1) * tq, cols] = _unstack_heads(acc).astype(oa_scr.dtype)
            reach = carry if reach is None else jnp.maximum(reach, carry)

    @pl.when(jnp.max(reach) > SB_SKIP_LOG)
    def _():
        for p, cols in enumerate(pair_cols):
            def one(g, carry, p=p, cols=cols):
                rows = pl.ds(pl.multiple_of(g * tq, tq), tq)
                groups = _sb_prompt_groups([q_ref[0, rows, cols]], k_of[p], v_of[p], i0 + g, consts)
                o = _sb_prompt_finish(groups[0], _sb_groups(groups)[0], i0 + g, k_of[p], v_of[p], consts)
                oa_scr[rows, cols] = o.astype(oa_scr.dtype)
                return carry

            lax.fori_loop(0, n_q, one, 0)


def _even_post_attn(x, mod, ob, q, k_blocks, v_blocks, wo, gf, wg, wu, wd, layer, tt):
    b, t, d = x.shape
    n_tb = t // tt
    n_blocks = b * n_tb
    attn_blk = lambda s: jnp.minimum(s, n_blocks - 1)
    post_blk = lambda s: jnp.maximum(s - 1, 0)
    rows = lambda width, which: pl.BlockSpec((1, tt, width), lambda s: (which(s) // n_tb, which(s) % n_tb, 0))
    keys = pl.BlockSpec((None, t // LANES, SB_WIDTH, LANES), lambda s: (attn_blk(s) // n_tb, 0, 0, 0),
                        pipeline_mode=pl.Buffered(1))
    return pl.pallas_call(
        functools.partial(_even_post_attn_kernel, n_tb),
        grid=(n_blocks + 1,),
        in_specs=[rows(d, post_blk), pl.BlockSpec((1, N_MOD, d), lambda s: (post_blk(s) // n_tb, 0, 0)),
                  rows(HG_W, post_blk), rows(SB_WIDTH, attn_blk), keys, keys,
                  _resident((d, d)), _resident((1, d))] + _ffn_specs(layer),
        out_specs=rows(d, post_blk),
        out_shape=jax.ShapeDtypeStruct((b, t, d), f32),
        scratch_shapes=[pltpu.VMEM((tt, SB_WIDTH), bf16)],
        compiler_params=_params("arbitrary"),
        name="even_attn_post_ffn",
    )(x, mod, ob, q, k_blocks, v_blocks, wo, gf, wg, wu, wd)


def _linear_scan(a, u, h_init, tt):
    m, d = a.shape
    n_group = m // SUBLANES
    a3 = a.reshape(n_group, SUBLANES, d)
    u3 = u.reshape(n_group, SUBLANES, d)
    sub = lax.broadcasted_iota(jnp.int32, (1, SUBLANES, d), 1)
    step = 1
    while step < SUBLANES:
        keep = sub >= step
        a_prev = jnp.where(keep, pltpu.roll(a3, step, 1), 1.0)
        u_prev = jnp.where(keep, pltpu.roll(u3, step, 1), 0.0)
        u3 = a3 * u_prev + u3
        a3 = a3 * a_prev
        step *= 2
    groups_per_seq = tt // SUBLANES
    carry = None
    groups = []
    for gi in range(n_group):
        if gi % groups_per_seq == 0:
            carry = h_init[gi // groups_per_seq]
        h_group = u3[gi] + a3[gi] * carry
        groups.append(h_group)
        carry = h_group[SUBLANES - 1:SUBLANES, :]
    return jnp.concatenate(groups, axis=0)


def _odd_kernel(stream_start, pipelined, n_tb, x_ref, mod_ref, mod_ffn_ref, gm_ref, win_ref, cw_ref, cb_ref,
                wax_ref, ba_ref, bx_ref, lam_ref, wo_ref, conv0_ref, h0_ref, gf_ref, fn_ref,
                wg_ref, wu_ref, wd_ref, y_ref, conv_ref, hout_ref, xs_scr, hprev_scr, x1_scr, hn_scr):
    nb, tt, d = x_ref.shape
    m = nb * tt
    s = pl.program_id(0)
    tb = lax.rem(s, n_tb)

    def ffn_units(x1, hn_bf, out):
        acts = []
        for j in range(D_FF // MXU_COLS):
            cols = slice(j * MXU_COLS, (j + 1) * MXU_COLS)
            g = _dot(hn_bf, wg_ref[:, cols])
            u = _dot(hn_bf, wu_ref[:, cols])
            acts.append((_silu(g) * u).astype(bf16))
            yield
        act = jnp.concatenate(acts, axis=1)
        parts = []
        for n in range(d // MXU_COLS):
            cols = slice(n * MXU_COLS, (n + 1) * MXU_COLS)
            y = _dot(act, wd_ref[:, cols]).reshape(nb, tt, MXU_COLS)
            parts.append(x1[:, :, cols] + (1.0 + mod_ffn_ref[:, 5:6, cols]) * y)
            yield
        x2 = jnp.concatenate(parts, axis=-1)
        ms = jnp.mean(x2 * x2, axis=-1, keepdims=True)
        out["y"] = x2 * lax.rsqrt(ms + EPS) * fn_ref[...]

    def mixer_units(out):
        x = x_ref[...]
        hn = _norm_mod(x, gm_ref[...], mod_ref[:, 1:2, :], mod_ref[:, 0:1, :])
        hn = hn.reshape(m, d).astype(bf16)
        yield
        gate = _dot(hn, win_ref[:, :d])
        yield
        xbr = _dot(hn, win_ref[:, d:]).reshape(nb, tt, d)
        yield
        xs_scr[:, HALO:HALO + tt, :] = xbr
        xc = cb_ref[...] + xbr * cw_ref[CONV_W - 1:CONV_W, :]
        for k in range(1, CONV_W):
            xc = xc + xs_scr[:, HALO - k:HALO - k + tt, :] * cw_ref[CONV_W - 1 - k:CONV_W - k, :]
        tail = xs_scr[:, HALO + tt - (CONV_W - 1):HALO + tt, :]
        xs_scr[:, HALO - (CONV_W - 1):HALO, :] = tail
        xc = xc.reshape(m, d)
        xc_bf = xc.astype(bf16)
        yield
        lam = lam_ref[...]
        log_sig_lam = -(jnp.maximum(-lam, 0.0) + jnp.log1p(jnp.exp(-jnp.abs(lam))))
        y_in, h_last = [], []
        for blk in range(LRU_BLOCKS):
            cols = slice(blk * LRU_BLK, (blk + 1) * LRU_BLK)
            both = _dot(xc_bf[:, cols], wax_ref[blk])
            r = _sigmoid(both[:, :LRU_BLK] + ba_ref[:, cols])
            gi = _sigmoid(both[:, LRU_BLK:] + bx_ref[:, cols])
            log_a = C_SCALE * r * log_sig_lam[:, cols]
            a = jnp.exp(log_a)
            mult = jnp.sqrt(-jnp.tanh(log_a) * (1.0 + a * a))
            if stream_start:
                row = (lax.broadcasted_iota(jnp.int32, (m, LRU_BLK), 0) & (tt - 1)) + tb * tt
                mult = jnp.where(row == 0, 1.0, mult)
            h = _linear_scan(a, mult * gi * xc[:, cols], hprev_scr[:, :, cols], tt)
            h_last.append(h.reshape(nb, tt, LRU_BLK)[:, tt - 1:tt, :])
            y_in.append((jax.nn.gelu(gate[:, cols]) * h).astype(bf16))
            yield
        h_last = jnp.concatenate(h_last, axis=-1)
        hprev_scr[...] = h_last
        y_in = jnp.concatenate(y_in, axis=1)
        o = _dot(y_in, wo_ref[...])
        yield
        x1 = x + (1.0 + mod_ref[:, 2:3, :]) * o.reshape(nb, tt, d)
        hn2 = _norm_mod(x1, gf_ref[...], mod_ref[:, 4:5, :], mod_ref[:, 3:4, :])
        out.update(x1=x1, hn2=hn2.reshape(m, d).astype(bf16), tail=tail, h_last=h_last)

    @pl.when(tb == 0)
    def _():
        xs_scr[:, HALO - (CONV_W - 1):HALO, :] = conv0_ref[...]
        hprev_scr[...] = h0_ref[...]

    res = {}
    if pipelined:
        @pl.when(s == 0)
        def _():
            x1_scr[...] = jnp.zeros_like(x1_scr)
            hn_scr[...] = jnp.zeros_like(hn_scr)

        _round_robin(ffn_units(x1_scr[...], hn_scr[...], res), mixer_units(res))
        y_ref[...] = res["y"]
        x1_scr[...] = res["x1"]
        hn_scr[...] = res["hn2"]

        @pl.when(s < pl.num_programs(0) - 1)
        def _():
            conv_ref[...] = res["tail"]
            hout_ref[...] = res["h_last"]
    else:
        for _ in mixer_units(res):
            pass
        conv_ref[...] = res["tail"]
        hout_ref[...] = res["h_last"]
        for _ in ffn_units(res["x1"], res["hn2"], res):
            pass
        y_ref[...] = res["y"]


def _odd_layer(x, mod, gm, win, cw, cb, wax, ba, bx, lam, wo, conv0, h0, gf, fn, wg, wu, wd,
               layer, nb, tt, stream_start):
    b, t, d = x.shape
    n_tb = t // tt
    n_blocks = (b // nb) * n_tb
    pipelined = n_blocks > 1
    mix_blk = lambda s: jnp.minimum(s, n_blocks - 1)
    ffn_blk = (lambda s: jnp.maximum(s - 1, 0)) if pipelined else mix_blk
    rows_spec = lambda which: pl.BlockSpec((nb, tt, d), lambda s: (which(s) // n_tb, which(s) % n_tb, 0))
    per_seq = lambda rows, which: pl.BlockSpec((nb, rows, d), lambda s: (which(s) // n_tb, 0, 0))
    row = _resident((1, d))
    return pl.pallas_call(
        functools.partial(_odd_kernel, stream_start, pipelined, n_tb),
        grid=(n_blocks + (1 if pipelined else 0),),
        in_specs=[rows_spec(mix_blk), per_seq(N_MOD, mix_blk), per_seq(N_MOD, ffn_blk), row,
                  _resident((d, 2 * d)), _resident((CONV_W, d)),
                  row, _resident((LRU_BLOCKS, LRU_BLK, 2 * LRU_BLK)), row,
                  row, row, _resident((d, d)), per_seq(CONV_W - 1, mix_blk), per_seq(1, mix_blk),
                  row, row] + _ffn_specs(layer),
        out_specs=[rows_spec(ffn_blk), per_seq(CONV_W - 1, mix_blk), per_seq(1, mix_blk)],
        out_shape=[jax.ShapeDtypeStruct((b, t, d), f32),
                   jax.ShapeDtypeStruct((b, CONV_W - 1, d), f32),
                   jax.ShapeDtypeStruct((b, 1, d), f32)],
        scratch_shapes=[pltpu.VMEM((nb, HALO + tt, d), f32), pltpu.VMEM((nb, 1, d), f32),
                        pltpu.VMEM((nb, tt, d) if pipelined else (1, SUBLANES, LANES), f32),
                        pltpu.VMEM((nb * tt, d) if pipelined else (2 * SUBLANES, LANES), bf16)],
        compiler_params=_params("arbitrary"),
        name="odd_layer_ffn",
    )(x, mod, mod, gm, win, cw, cb, wax, ba, bx, lam, wo, conv0, h0, gf, fn, wg, wu, wd)


def _trunk(x, mods, past, weights, nb, tt, hg_tt):
    b, t, d = x.shape
    sample = past is not None
    row = lambda v: v.reshape(1, -1)

    if sample:
        past_k, past_v, s0, conv0, h0 = past
        qa, ka, va, ka_bf, va_bf, qb, lf, kb, ib, gs = _even_pre(
            x, mods[0], row(weights["norm_mix"][0]), weights["w_in_even"],
            weights["hg_lb_logits"], 0, nb, tt)
        oa = _sb_sample(qa, ka_bf, va_bf, past_k, past_v, LANES)
        ob, s_new = _hgrn(qb, lf, kb, ib, gs, row(weights["hg_gnorm"]), s0, hg_tt)
        x = _even_post(x, mods[0], oa, ob, weights["w_out_even"], row(weights["norm_ffn"][0]),
                       weights["ffn_wg"], weights["ffn_wu"], weights["ffn_wd"], 0, nb, tt)
        heads = lambda a: a.reshape(1, b, t, SB_HEADS, SB_DIM)
    else:
        s0 = jnp.zeros((b, HG_HEADS, HG_DK, HG_DV), f32)
        conv0 = jnp.zeros((b, CONV_W - 1, d), f32)
        h0 = jnp.zeros((b, 1, d), f32)
        qa, ka, va, ka_bf, va_bf, ob, s_new = _even_fused(
            x, mods[0], row(weights["norm_mix"][0]), weights["w_in_even"], weights["w_kv_t"],
            weights["hg_lb_logits"], row(weights["hg_gnorm"]), s0, 0, tt)
        x = _even_post_attn(x, mods[0], ob, qa, ka_bf, va_bf, weights["w_out_even"],
                            row(weights["norm_ffn"][0]), weights["ffn_wg"], weights["ffn_wu"],
                            weights["ffn_wd"], 0, tt)
        heads = lambda a: a.reshape(b, SB_HEADS, SB_DIM, t).transpose(0, 3, 1, 2)[None]

    y, conv_new, h_new = _odd_layer(
        x, mods[1], row(weights["norm_mix"][1]), weights["w_in_odd"], weights["conv_w"],
        row(weights["conv_b"]), weights["lru_wax"], row(weights["lru_ba"]), row(weights["lru_bx"]),
        row(weights["lru_lambda"]), weights["w_out_odd"], conv0, h0,
        row(weights["norm_ffn"][1]), row(weights["final_norm"]),
        weights["ffn_wg"], weights["ffn_wu"], weights["ffn_wd"],
        1, nb, tt, not sample)

    return (y, heads(ka), heads(va), s_new[None], conv_new[None], h_new.reshape(1, b, d))


def kernel(x_prompt, x_sample, cache_sb_k, cache_sb_v, state_hgrn, state_conv, state_lru, c_prompt, c_sample, norm_mix, norm_ffn, w_ada, b_ada, w_in_even, w_out_even, hg_gnorm, hg_lb_logits, w_in_odd, conv_w, conv_b, lru_wa, lru_ba, lru_wx, lru_bx, lru_lambda, w_out_odd, ffn_wg, ffn_wu, ffn_wd, final_norm):
    bp = x_prompt.shape[0]
    bs, ts, d = x_sample.shape
    w_in_even_bf = w_in_even[0].astype(bf16)
    weights = {
        "norm_mix": norm_mix, "norm_ffn": norm_ffn,
        "w_in_even": w_in_even_bf, "w_out_even": w_out_even[0].astype(bf16),
        "w_kv_t": w_in_even_bf[:, SEG:3 * SEG].T.reshape(2, SEG, d),
        "hg_gnorm": hg_gnorm[0], "hg_lb_logits": hg_lb_logits,
        "w_in_odd": w_in_odd[0].astype(bf16), "conv_w": conv_w[0], "conv_b": conv_b[0],
        "lru_wax": jnp.concatenate([lru_wa[0], lru_wx[0]], axis=-1).astype(bf16),
        "lru_ba": lru_ba[0], "lru_bx": lru_bx[0], "lru_lambda": lru_lambda[0],
        "w_out_odd": w_out_odd[0].astype(bf16),
        "ffn_wg": ffn_wg.astype(bf16), "ffn_wu": ffn_wu.astype(bf16), "ffn_wd": ffn_wd.astype(bf16),
        "final_norm": final_norm,
    }
    mods = _ada(jnp.concatenate([c_prompt, c_sample], axis=0), w_ada, b_ada)
    mods = mods.reshape(DEPTH, bp + bs, N_MOD, d)

    out_p = _trunk(x_prompt, mods[:, :bp], None, weights, nb=1, tt=512, hg_tt=512)
    cache_t = lambda c: c[0].transpose(0, 2, 3, 1).reshape(bs, SB_WIDTH, -1)
    past = (cache_t(cache_sb_k), cache_t(cache_sb_v),
            state_hgrn[0], state_conv[0], state_lru[0].reshape(bs, 1, d))
    out_s = _trunk(x_sample, mods[:, bp:], past, weights, nb=bs, tt=ts, hg_tt=ts)

    y_p, k_p, v_p, s_p, conv_p, h_p = out_p
    y_s, k_s, v_s, s_s, conv_s, h_s = out_s
    return (y_p, y_s, k_p, v_p, s_p, conv_p, h_p, k_s, v_s, s_s, conv_s, h_s)
```

```python
import functools
import math

import jax
import jax.numpy as jnp
from jax import lax
from jax.experimental import pallas as pl
from jax.experimental.pallas import tpu as pltpu

f32 = jnp.float32
bf16 = jnp.bfloat16

D_MODEL = 1024
DEPTH = 2
EPS = 1e-6
N_MOD = 6

SB_HEADS = 8
SB_DIM = 64
SB_WIDTH = SB_HEADS * SB_DIM
HG_HEADS = 4
HG_DK = 128
HG_DV = 128
HG_W = HG_HEADS * HG_DK
SEG = 512
N_SEG = 7
HG_CHUNK = 64
HG_SUB = 32
HG_MATMUL_MIN_LOG = -40.0
LRU_BLOCKS = 8
LRU_BLK = D_MODEL // LRU_BLOCKS
CONV_W = 4
C_SCALE = 8.0
D_FF = 2816

SB_WINDOW = 2
SB_SKIP_LOG = -105.0

MXU_COLS = 256

LANES = 128
SUBLANES = 8
HALO = 8
VMEM_LIMIT = 60 * 1024 * 1024

NT_DIMS = (((1,), (1,)), ((), ()))
TN_DIMS = (((0,), (0,)), ((), ()))


def _dot(a, b):
    return jnp.dot(a, b, preferred_element_type=f32)


def _dot_nt(a, b):
    return lax.dot_general(a, b, NT_DIMS, preferred_element_type=f32)


def _split3(x):
    hi = x.astype(bf16)
    r1 = x - hi.astype(f32)
    mid = r1.astype(bf16)
    lo = (r1 - mid.astype(f32)).astype(bf16)
    return hi, mid, lo


def _split2(x):
    hi = x.astype(bf16)
    return hi, (x - hi.astype(f32)).astype(bf16)


def _sigmoid(x):
    return jax.nn.sigmoid(x)


def _silu(x):
    return x * _sigmoid(x)


def _norm_mod(x, g, scale, shift):
    ms = jnp.mean(x * x, axis=-1, keepdims=True)
    return x * lax.rsqrt(ms + EPS) * (g * (1.0 + scale)) + shift


def _params(*sem):
    return pltpu.CompilerParams(dimension_semantics=sem, vmem_limit_bytes=VMEM_LIMIT)


def _ada_kernel(c_ref, w_ref, b_ref, o_ref):
    c = c_ref[...]
    sc = _silu(c)
    w = w_ref[...]
    s_hi = sc.astype(bf16)
    s_lo = (sc - s_hi.astype(f32)).astype(bf16)
    w_hi = w.astype(bf16)
    w_lo = (w - w_hi.astype(f32)).astype(bf16)
    o_ref[...] = _dot(s_hi, w_hi) + _dot(s_lo, w_hi) + _dot(s_hi, w_lo) + b_ref[...]


def _ada(c_all, w_ada, b_ada):
    rows = c_all.shape[0]
    n_out = N_MOD * D_MODEL
    tn = 1536
    return pl.pallas_call(
        _ada_kernel,
        grid=(DEPTH, n_out // tn),
        in_specs=[
            pl.BlockSpec((rows, D_MODEL), lambda l, j: (0, 0)),
            pl.BlockSpec((None, D_MODEL, tn), lambda l, j: (l, 0, j)),
            pl.BlockSpec((None, 1, tn), lambda l, j: (l, 0, j)),
        ],
        out_specs=pl.BlockSpec((None, rows, tn), lambda l, j: (l, 0, j)),
        out_shape=jax.ShapeDtypeStruct((DEPTH, rows, n_out), f32),
        compiler_params=_params("arbitrary", "arbitrary"),
        name="ada_mod",
    )(c_all, w_ada, b_ada.reshape(DEPTH, 1, n_out))


def _even_pre_kernel(layer, x_ref, mod_ref, g_ref, w_ref, lbl_ref,
                     qa_ref, ka_ref, va_ref, kab_ref, vab_ref,
                     qb_ref, lf_ref, kb_ref, ib_ref, gs_ref):
    nb, tt, d = x_ref.shape
    m = nb * tt
    hn = _norm_mod(x_ref[...], g_ref[...], mod_ref[:, 1:2, :], mod_ref[:, 0:1, :])
    hn = hn.reshape(m, d).astype(bf16)

    def seg(i):
        return _dot(hn, w_ref[:, i * SEG:(i + 1) * SEG]).reshape(nb, tt, SEG)

    qa_ref[...] = (seg(0) * (1.0 / math.sqrt(SB_DIM))).astype(bf16)
    ka = seg(1)
    ka_ref[...] = ka
    kab_ref[...] = ka.astype(bf16)
    va = seg(2)
    va_ref[...] = va
    vab_ref[...] = va.astype(bf16)
    qb_ref[...] = _silu(seg(3)) * (HG_DK ** -0.5)

    lg = lbl_ref[...]
    e = jnp.exp(lg - jnp.max(lg, axis=0, keepdims=True))
    lb = jnp.sum(e[:layer + 1], axis=0, keepdims=True) / jnp.sum(e, axis=0, keepdims=True)
    fb = seg(4)
    f = lb + (1.0 - lb) * _sigmoid(fb)
    lf_ref[...] = jnp.log(f)
    kb_ref[...] = (1.0 - lb) * _sigmoid(-fb)
    ib_ref[...] = seg(5)
    gs_ref[...] = _silu(seg(6))


def _even_pre(x, mod, g, w_bf, lb_logits, layer, nb, tt):
    b, t, d = x.shape
    grid = (b // nb, t // tt)
    blk = lambda width: pl.BlockSpec((nb, tt, width), lambda i, j: (i, j, 0))
    out_f = jax.ShapeDtypeStruct((b, t, SEG), f32)
    out_b = jax.ShapeDtypeStruct((b, t, SEG), bf16)
    return pl.pallas_call(
        functools.partial(_even_pre_kernel, layer),
        grid=grid,
        in_specs=[
            blk(d),
            pl.BlockSpec((nb, N_MOD, d), lambda i, j: (i, 0, 0)),
            _resident((1, d)),
            _resident((d, N_SEG * SEG)),
            _resident((DEPTH + 1, HG_W)),
        ],
        out_specs=[blk(SEG)] * 10,
        out_shape=[out_b, out_f, out_f, out_b, out_b, out_f, out_f, out_f, out_f, out_f],
        compiler_params=_params("arbitrary", "arbitrary"),
        name="even_pre",
    )(x, mod, g, w_bf, lb_logits)


def _sb_consts(tk, scale=1.0):
    r = lax.broadcasted_iota(jnp.int32, (tk, LANES + tk), 0)
    c = lax.broadcasted_iota(jnp.int32, (tk, LANES + tk), 1)
    m = jnp.where((c < LANES) | (r > c - LANES), scale, 0.0).astype(bf16)
    return jnp.concatenate([m, m], axis=0)


def _sb_groups_units(groups, out):
    zs = [[_dot(q2, k_blk) if transposed else _dot_nt(q2, k_blk)
           for k_blk, _, _, _, transposed in blocks] for q2, blocks, _, _ in groups]
    yield
    l1mbs, sums = [], []
    for g, (_, blocks, _, _) in enumerate(groups):
        l1mbs.append([])
        sums.append([])
        for z, (_, _, sum_matrix, mask, _) in zip(zs[g], blocks):
            l1mb = -(jnp.maximum(z, 0.0) + jnp.log(1.0 + jnp.exp(-jnp.abs(z))))
            if mask is not None:
                l1mb = jnp.where(mask, l1mb, 0.0)
            l1mbs[g].append(l1mb)
            sums[g].append(_dot(jnp.concatenate(_split2(l1mb), axis=1), sum_matrix))
    yield
    results = []
    for g, (_, blocks, carry, acc) in enumerate(groups):
        for z, l1mb, s, (_, v_blk, _, mask, transposed) in zip(zs[g], l1mbs[g], sums[g], blocks):
            tk = z.shape[1]
            after = carry[:, :tk] + s[:, LANES:]
            w = jnp.exp(z + l1mb + after)
            if mask is not None:
                w = jnp.where(mask, w, 0.0)
            w = w.astype(bf16)
            acc = acc + (_dot_nt(w, v_blk) if transposed else _dot(w, v_blk))
            carry = carry + s[:, :LANES]
        results.append((carry, acc))
    out["results"] = results


def _sb_groups(groups):
    out = {}
    for _ in _sb_groups_units(groups, out):
        pass
    return out["results"]


def _sb_blocks(q2, blocks, carry, acc):
    return _sb_groups([(q2, blocks, carry, acc)])[0]


def _sb_finish(j0, carry, acc, step):
    def cond(state):
        j, carry, _ = state
        return (j >= 0) & (jnp.max(carry) > SB_SKIP_LOG)

    def body(state):
        j, carry, acc = state
        carry, acc = step(j, carry, acc)
        return j - 1, carry, acc

    return lax.while_loop(cond, body, (j0, carry, acc))[2]


def _stack_heads(q):
    lane = lax.broadcasted_iota(jnp.int32, q.shape, 1)
    zero = jnp.zeros_like(q)
    return jnp.concatenate([jnp.where(lane < SB_DIM, q, zero), jnp.where(lane >= SB_DIM, q, zero)], axis=0)


def _unstack_heads(acc2):
    tq = acc2.shape[0] // 2
    lane = lax.broadcasted_iota(jnp.int32, (tq, LANES), 1)
    return jnp.where(lane < SB_DIM, acc2[:tq], acc2[tq:])


def _causal_mask2(tq):
    r = lax.broadcasted_iota(jnp.int32, (2 * tq, tq), 0) & (tq - 1)
    c = lax.broadcasted_iota(jnp.int32, (2 * tq, tq), 1)
    return c < r


def _sb_prompt_groups(qs, k_of, v_of, i0, consts):
    sum_matrix, diag_mask, zero = consts
    tq = qs[0].shape[0]
    groups = []
    for g, q in enumerate(qs):
        i = i0 + g
        blocks = [(k_of(i), v_of(i), sum_matrix, diag_mask, True)]
        for c in range(1, SB_WINDOW + 1):
            scale = jnp.where(i >= c, 1.0, 0.0)
            j = jnp.maximum(i - c, 0)
            blocks.append((k_of(j), (v_of(j).astype(f32) * scale).astype(bf16), _sb_consts(tq, scale), None, True))
        groups.append((_stack_heads(q), blocks, zero, zero))
    return groups


def _sb_prompt_consts(tq):
    return _sb_consts(tq), _causal_mask2(tq), jnp.zeros((2 * tq, LANES), f32)


def _sb_prompt_finish(group, result, i, k_of, v_of, consts):
    q2 = group[0]
    carry, acc = result

    def step(j, carry, acc):
        return _sb_blocks(q2, [(k_of(j), v_of(j), consts[0], None, True)], carry, acc)

    return _unstack_heads(_sb_finish(i - SB_WINDOW - 1, carry, acc, step))


def _sb_sample_kernel(tk, n_past, q_ref, k_ref, v_ref, wk_ref, wv_ref, pk_hbm, pv_hbm, o_ref,
                      kbuf, vbuf, sem):
    tq = q_ref.shape[0]
    b = pl.program_id(0)
    zero = jnp.zeros((2 * tq, LANES), f32)
    sum_matrix = _sb_consts(tk)
    diag_matrix = _sb_consts(tq)
    diag_mask = _causal_mask2(tq)

    def fetch(j):
        keys = pl.ds(pl.multiple_of(j * tk, tk), tk)
        copies = [pltpu.make_async_copy(pk_hbm.at[b, :, keys], kbuf, sem.at[0]),
                  pltpu.make_async_copy(pv_hbm.at[b, :, keys], vbuf, sem.at[1])]
        for cp in copies:
            cp.start()
        for cp in copies:
            cp.wait()

    groups = []
    for pair in range(SB_WIDTH // LANES):
        cols = slice(pair * LANES, (pair + 1) * LANES)
        q2 = _stack_heads(q_ref[:, cols])
        blocks = [(k_ref[:, cols], v_ref[:, cols], diag_matrix, diag_mask, False)]
        for c in range(1, SB_WINDOW + 1):
            keys = slice((SB_WINDOW - c) * tk, (SB_WINDOW - c + 1) * tk)
            blocks.append((wk_ref[cols, keys].astype(bf16), wv_ref[cols, keys].astype(bf16),
                           sum_matrix, None, True))
        groups.append((q2, blocks, zero, zero))

    for pair, (carry, acc) in enumerate(_sb_groups(groups)):
        cols = slice(pair * LANES, (pair + 1) * LANES)
        q2 = groups[pair][0]

        def step(j, carry, acc, cols=cols, q2=q2):
            fetch(j)
            blk = (kbuf[cols, :].astype(bf16), vbuf[cols, :].astype(bf16), sum_matrix, None, True)
            return _sb_blocks(q2, [blk], carry, acc)

        acc = _sb_finish(n_past - SB_WINDOW - 1, carry, acc, step)
        o_ref[:, cols] = _unstack_heads(acc).astype(o_ref.dtype)


def _sb_sample(q, k, v, past_kt, past_vt, tk):
    b, t, _ = q.shape
    n_past = past_kt.shape[2] // tk
    assert n_past % SB_WINDOW == 0 and n_past >= SB_WINDOW
    qspec = pl.BlockSpec((None, t, SB_WIDTH), lambda bi: (bi, 0, 0))
    wspec = pl.BlockSpec((None, SB_WIDTH, SB_WINDOW * tk), lambda bi: (bi, 0, n_past // SB_WINDOW - 1))
    hbm = pl.BlockSpec(memory_space=pl.ANY)
    return pl.pallas_call(
        functools.partial(_sb_sample_kernel, tk, n_past),
        grid=(b,),
        in_specs=[qspec, qspec, qspec, wspec, wspec, hbm, hbm],
        out_specs=qspec,
        out_shape=jax.ShapeDtypeStruct((b, t, SB_WIDTH), bf16),
        scratch_shapes=[pltpu.VMEM((SB_WIDTH, tk), f32), pltpu.VMEM((SB_WIDTH, tk), f32),
                        pltpu.SemaphoreType.DMA((2,))],
        compiler_params=_params("arbitrary"),
        name="sb_sample",
    )(q, k, v, past_kt, past_vt, past_kt, past_vt)


def _hgrn_head_chunk(q, k, v, b, st):
    n_sub = HG_CHUNK // HG_SUB
    v_bf = v.astype(bf16)
    o_inter = _dot_nt((q * jnp.exp(b)).astype(bf16), st.astype(bf16))
    t_idx = lax.broadcasted_iota(jnp.int32, (HG_SUB, HG_SUB, HG_DK), 0)
    s_idx = lax.broadcasted_iota(jnp.int32, (HG_SUB, HG_SUB, HG_DK), 1)
    causal = s_idx <= t_idx
    outs = []
    for i in range(n_sub):
        lo = i * HG_SUB
        rows = slice(lo, lo + HG_SUB)
        qi, ki, bi = q[rows], k[rows], b[rows]
        diff = bi[:, None, :] - bi[None, :, :]
        dec = jnp.exp(jnp.where(causal, diff, -1e30))
        a_diag = jnp.sum(qi[:, None, :] * ki[None, :, :] * dec, axis=-1)
        o_i = _dot(a_diag.astype(bf16), v_bf[rows])
        if i > 0:
            ref = b[lo - 1:lo]
            q_t = qi * jnp.exp(bi - ref)
            k_t = k[:lo] * jnp.exp(ref - b[:lo])
            a_off = _dot_nt(q_t.astype(bf16), k_t.astype(bf16))
            o_i = o_i + _dot(a_off.astype(bf16), v_bf[:lo])
        outs.append(o_inter[rows] + o_i)
    o = jnp.concatenate(outs, axis=0)
    b_last = b[HG_CHUNK - 1:HG_CHUNK]
    k_dec = (k * jnp.exp(b_last - b)).astype(bf16)
    st_new = jnp.exp(b_last) * st + lax.dot_general(v_bf, k_dec, TN_DIMS, preferred_element_type=f32)
    return o, st_new


def _hgrn_intra_units(qs, ks, vs, bs, out):
    n_sub = HG_CHUNK // HG_SUB
    heads = range(len(qs))
    q_in, q_t, k_t, k_dec, v_bf, b_last = [], [], [], [], [], []
    for h in heads:
        q, k, b = qs[h], ks[h], bs[h]
        q_in.append((q * jnp.exp(b)).astype(bf16))
        q_t.append([])
        k_t.append([])
        for i in range(n_sub):
            lo, hi = i * HG_SUB, (i + 1) * HG_SUB
            if i == 0:
                q_t[h].append(q_in[h][:hi])
                k_t[h].append((k[:hi] * jnp.exp(-b[:hi])).astype(bf16))
            else:
                ref = b[lo - 1:lo]
                q_t[h].append((q[lo:hi] * jnp.exp(b[lo:hi] - ref)).astype(bf16))
                k_t[h].append((k[:hi] * jnp.exp(ref - b[:hi])).astype(bf16))
        b_last.append(b[HG_CHUNK - 1:HG_CHUNK])
        k_dec.append((k * jnp.exp(b_last[h] - b)).astype(bf16))
        v_bf.append(vs[h].astype(bf16))
    yield

    attn = [[_dot_nt(q_t[h][i], k_t[h][i]) for i in range(n_sub)] for h in heads]
    st_add = [lax.dot_general(v_bf[h], k_dec[h], TN_DIMS, preferred_element_type=f32) for h in heads]
    yield

    o_intra = []
    for h in heads:
        pieces = []
        for i in range(n_sub):
            hi = (i + 1) * HG_SUB
            r = lax.broadcasted_iota(jnp.int32, (HG_SUB, hi), 0)
            c = lax.broadcasted_iota(jnp.int32, (HG_SUB, hi), 1)
            a = jnp.where(c - i * HG_SUB <= r, attn[h][i], 0.0)
            pieces.append(_dot(a.astype(bf16), v_bf[h][:hi]))
        o_intra.append(jnp.concatenate(pieces, axis=0))
    out.update(o_intra=o_intra, q_in=q_in, b_last=b_last, st_add=st_add)


def _hgrn_apply_state(intra, sts):
    heads = range(len(sts))
    outs = [intra["o_intra"][h] + _dot_nt(intra["q_in"][h], sts[h].astype(bf16)) for h in heads]
    new_sts = [jnp.exp(intra["b_last"][h]) * sts[h] + intra["st_add"][h] for h in heads]
    return outs, new_sts


def _hgrn_chunk_matmul(qs, ks, vs, bs, sts):
    intra = {}
    for _ in _hgrn_intra_units(qs, ks, vs, bs, intra):
        pass
    return _hgrn_apply_state(intra, sts)


def _hgrn_kernel(q_ref, lf_ref, k_ref, v_ref, gs_ref, gn_ref, s0_ref, o_ref, sout_ref, st_scr):
    tb = pl.program_id(1)
    n_tb = pl.num_programs(1)
    n_chunk = q_ref.shape[0] // HG_CHUNK
    n_sub = HG_CHUNK // HG_SUB

    @pl.when(tb == 0)
    def _():
        for h in range(HG_HEADS):
            st_scr[h] = s0_ref[h].T

    r = lax.broadcasted_iota(jnp.int32, (HG_CHUNK, HG_CHUNK), 0)
    c = lax.broadcasted_iota(jnp.int32, (HG_CHUNK, HG_CHUNK), 1)
    lower = jnp.where(c <= r, 1.0, 0.0).astype(bf16)
    head_cols = [slice(h * HG_DK, (h + 1) * HG_DK) for h in range(HG_HEADS)]

    def chunk(ci, carry):
        rows = pl.ds(pl.multiple_of(ci * HG_CHUNK, HG_CHUNK), HG_CHUNK)
        b_all = _dot_exact_rhs_lhs(lower, lf_ref[rows, :])

        def finish(h, o, st_new):
            cols = head_cols[h]
            st_scr[h] = st_new
            o = o * lax.rsqrt(jnp.mean(o * o, axis=-1, keepdims=True) + EPS)
            o_ref[rows, cols] = (o * gn_ref[:, cols] * gs_ref[rows, cols]).astype(o_ref.dtype)

        ends = [b_all[(i + 1) * HG_SUB - 1:(i + 1) * HG_SUB, :] for i in range(n_sub)]
        span = ends[0]
        for i in range(1, n_sub):
            span = jnp.minimum(span, ends[i] - ends[i - 1])
        bounded = jnp.min(span) >= HG_MATMUL_MIN_LOG

        @pl.when(bounded)
        def _():
            outs, new_sts = _hgrn_chunk_matmul(
                [q_ref[rows, cols] for cols in head_cols], [k_ref[rows, cols] for cols in head_cols],
                [v_ref[rows, cols] for cols in head_cols], [b_all[:, cols] for cols in head_cols],
                [st_scr[h] for h in range(HG_HEADS)])
            for h in range(HG_HEADS):
                finish(h, outs[h], new_sts[h])

        @pl.when(jnp.logical_not(bounded))
        def _():
            for h, cols in enumerate(head_cols):
                o, st_new = _hgrn_head_chunk(q_ref[rows, cols], k_ref[rows, cols], v_ref[rows, cols],
                                             b_all[:, cols], st_scr[h])
                finish(h, o, st_new)

        return carry

    lax.fori_loop(0, n_chunk, chunk, 0)

    @pl.when(tb == n_tb - 1)
    def _():
        for h in range(HG_HEADS):
            sout_ref[h] = st_scr[h].T


def _dot_exact_rhs_lhs(m, x):
    hi, mid, lo = _split3(x)
    return _dot(m, hi) + _dot(m, mid) + _dot(m, lo)


def _hgrn(qb, lf, kb, ib, gs, gnorm, s0, tt):
    b, t, _ = qb.shape
    blk = pl.BlockSpec((None, tt, HG_W), lambda i, j: (i, j, 0))
    sspec = pl.BlockSpec((None, HG_HEADS, HG_DK, HG_DV), lambda i, j: (i, 0, 0, 0))
    return pl.pallas_call(
        _hgrn_kernel,
        grid=(b, t // tt),
        in_specs=[blk, blk, blk, blk, blk, pl.BlockSpec((1, HG_W), lambda i, j: (0, 0)), sspec],
        out_specs=[blk, sspec],
        out_shape=[jax.ShapeDtypeStruct((b, t, HG_W), bf16),
                   jax.ShapeDtypeStruct((b, HG_HEADS, HG_DK, HG_DV), f32)],
        scratch_shapes=[pltpu.VMEM((HG_HEADS, HG_DV, HG_DK), f32)],
        compiler_params=_params("arbitrary", "arbitrary"),
        name="hgrn",
    )(qb, lf, kb, ib, gs, gnorm, s0)


def _round_robin(*gens, units=None):
    turns = list(zip(gens, units or [1] * len(gens)))
    while turns:
        turns = [(gen, n) for gen, n in turns if all(next(gen, "done") != "done" for _ in range(n))]


def _even_fused_kernel(layer, n_tb, x_ref, mod_ref, g_ref, w_ref, wkvt_ref, lbl_ref, gn_ref, s0_ref,
                       qa_ref, ka_ref, va_ref, kab_ref, vab_ref, ob_ref, sout_ref,
                       q_scr, b_scr, k_scr, v_scr, gs_scr, span_scr, st_scr):
    _, tt, d = x_ref.shape
    s = pl.program_id(0)
    hg_blk = jnp.maximum(s - 1, 0)
    tb_h = lax.rem(hg_blk, n_tb)
    n_chunk = tt // HG_CHUNK
    n_sub = HG_CHUNK // HG_SUB
    head_cols = [slice(h * HG_DK, (h + 1) * HG_DK) for h in range(HG_HEADS)]
    operand_scr = (q_scr, b_scr, k_scr, v_scr, gs_scr)

    @pl.when(s == 0)
    def _():
        for scr in operand_scr + (span_scr,):
            scr[...] = jnp.zeros_like(scr)

    @pl.when(tb_h == 0)
    def _():
        for h in range(HG_HEADS):
            st_scr[h] = s0_ref[h].T

    r = lax.broadcasted_iota(jnp.int32, (HG_CHUNK, HG_CHUNK), 0)
    c = lax.broadcasted_iota(jnp.int32, (HG_CHUNK, HG_CHUNK), 1)
    lower = jnp.where(c <= r, 1.0, 0.0).astype(bf16)

    def proj_units(out):
        hn = _norm_mod(x_ref[...], g_ref[...], mod_ref[:, 1:2, :], mod_ref[:, 0:1, :])
        hn = hn.reshape(tt, d).astype(bf16)
        yield

        def seg(i):
            return _dot(hn, w_ref[:, i * SEG:(i + 1) * SEG])

        qa_ref[...] = (seg(0) * (1.0 / math.sqrt(SB_DIM))).astype(bf16).reshape(1, tt, SEG)
        yield
        for i, (full_ref, blocks_ref) in enumerate(((ka_ref, kab_ref), (va_ref, vab_ref))):
            kt = _dot_nt(wkvt_ref[i], hn)
            full_ref[...] = kt
            for cb in range(tt // LANES):
                blocks_ref[cb] = kt[:, cb * LANES:(cb + 1) * LANES].astype(bf16)
            yield
        q = _silu(seg(3)) * (HG_DK ** -0.5)
        yield
        lg = lbl_ref[...]
        e = jnp.exp(lg - jnp.max(lg, axis=0, keepdims=True))
        lb = jnp.sum(e[:layer + 1], axis=0, keepdims=True) / jnp.sum(e, axis=0, keepdims=True)
        fb = seg(4)
        lf = jnp.log(lb + (1.0 - lb) * _sigmoid(fb))
        k = (1.0 - lb) * _sigmoid(-fb)
        yield
        b = jnp.concatenate([_dot_exact_rhs_lhs(lower, lf[ci * HG_CHUNK:(ci + 1) * HG_CHUNK])
                             for ci in range(n_chunk)], axis=0)
        yield
        v = seg(5)
        yield
        gs = _silu(seg(6))
        span = None
        for ci in range(n_chunk):
            prev = None
            for i in range(n_sub):
                row = ci * HG_CHUNK + (i + 1) * HG_SUB - 1
                end = b[row:row + 1, :]
                this = end if prev is None else end - prev
                span = this if span is None else jnp.minimum(span, this)
                prev = end
        out.update(q=q, b=b, k=k, v=v, gs=gs, span=span)

    def store_operands(out):
        for scr, name in zip(operand_scr, ("q", "b", "k", "v", "gs")):
            scr[...] = out[name]
        span_scr[0:1, :] = out["span"]

    def finish(rows, h, o):
        cols = head_cols[h]
        o = o * lax.rsqrt(jnp.mean(o * o, axis=-1, keepdims=True) + EPS)
        ob_ref[0, rows, cols] = (o * gn_ref[:, cols] * gs_scr[rows, cols]).astype(ob_ref.dtype)

    def hgrn_units():
        chunk_rows = [slice(ci * HG_CHUNK, (ci + 1) * HG_CHUNK) for ci in range(n_chunk)]
        intras = [{} for _ in chunk_rows]
        stages = [_hgrn_intra_units(
            [q_scr[rows, cols] for cols in head_cols], [k_scr[rows, cols] for cols in head_cols],
            [v_scr[rows, cols] for cols in head_cols], [b_scr[rows, cols] for cols in head_cols], intra)
            for rows, intra in zip(chunk_rows, intras)]
        while stages:
            unfinished = []
            for gen in stages:
                if next(gen, "done") != "done":
                    unfinished.append(gen)
                yield
            stages = unfinished
        sts = [st_scr[h] for h in range(HG_HEADS)]
        for rows, intra in zip(chunk_rows, intras):
            outs, sts = _hgrn_apply_state(intra, sts)
            for h in range(HG_HEADS):
                finish(rows, h, outs[h])
            yield
        for h in range(HG_HEADS):
            st_scr[h] = sts[h]

    bounded = jnp.min(span_scr[0:1, :]) >= HG_MATMUL_MIN_LOG

    @pl.when(bounded)
    def _():
        res = {}
        _round_robin(hgrn_units(), proj_units(res), units=[3, 1])
        store_operands(res)

    @pl.when(jnp.logical_not(bounded))
    def _():
        def chunk(ci, carry):
            rows = pl.ds(pl.multiple_of(ci * HG_CHUNK, HG_CHUNK), HG_CHUNK)
            for h, cols in enumerate(head_cols):
                o, st_new = _hgrn_head_chunk(q_scr[rows, cols], k_scr[rows, cols], v_scr[rows, cols],
                                             b_scr[rows, cols], st_scr[h])
                st_scr[h] = st_new
                finish(rows, h, o)
            return carry

        lax.fori_loop(0, n_chunk, chunk, 0)
        res = {}
        _round_robin(proj_units(res))
        store_operands(res)

    @pl.when((tb_h == n_tb - 1) & (s > 0))
    def _():
        for h in range(HG_HEADS):
            sout_ref[h] = st_scr[h].T


def _even_fused(x, mod, g, w_bf, wkvt_bf, lb_logits, gnorm, s0, layer, tt):
    b, t, d = x.shape
    n_tb = t // tt
    n_blocks = b * n_tb
    proj_blk = lambda s: jnp.minimum(s, n_blocks - 1)
    hgrn_blk = lambda s: jnp.maximum(s - 1, 0)
    rows = lambda width, which: pl.BlockSpec((1, tt, width), lambda s: (which(s) // n_tb, which(s) % n_tb, 0))
    operand = pltpu.VMEM((tt, HG_W), f32)
    return pl.pallas_call(
        functools.partial(_even_fused_kernel, layer, n_tb),
        grid=(n_blocks + 1,),
        in_specs=[
            rows(d, proj_blk),
            pl.BlockSpec((1, N_MOD, d), lambda s: (proj_blk(s) // n_tb, 0, 0)),
            _resident((1, d)),
            _resident((d, N_SEG * SEG)),
            _resident((2, SEG, d)),
            _resident((DEPTH + 1, HG_W)),
            _resident((1, HG_W)),
            pl.BlockSpec((None, HG_HEADS, HG_DK, HG_DV), lambda s: (hgrn_blk(s) // n_tb, 0, 0, 0)),
        ],
        out_specs=[
            rows(SEG, proj_blk),
            pl.BlockSpec((None, SEG, tt), lambda s: (proj_blk(s) // n_tb, 0, proj_blk(s) % n_tb)),
            pl.BlockSpec((None, SEG, tt), lambda s: (proj_blk(s) // n_tb, 0, proj_blk(s) % n_tb)),
            pl.BlockSpec((None, tt // LANES, SEG, LANES), lambda s: (proj_blk(s) // n_tb, proj_blk(s) % n_tb, 0, 0)),
            pl.BlockSpec((None, tt // LANES, SEG, LANES), lambda s: (proj_blk(s) // n_tb, proj_blk(s) % n_tb, 0, 0)),
            rows(HG_W, hgrn_blk),
            pl.BlockSpec((None, HG_HEADS, HG_DK, HG_DV), lambda s: (hgrn_blk(s) // n_tb, 0, 0, 0)),
        ],
        out_shape=[
            jax.ShapeDtypeStruct((b, t, SEG), bf16),
            jax.ShapeDtypeStruct((b, SEG, t), f32), jax.ShapeDtypeStruct((b, SEG, t), f32),
            jax.ShapeDtypeStruct((b, t // LANES, SEG, LANES), bf16),
            jax.ShapeDtypeStruct((b, t // LANES, SEG, LANES), bf16),
            jax.ShapeDtypeStruct((b, t, HG_W), bf16),
            jax.ShapeDtypeStruct((b, HG_HEADS, HG_DK, HG_DV), f32),
        ],
        scratch_shapes=[operand] * 5 + [pltpu.VMEM((SUBLANES, HG_W), f32),
                                        pltpu.VMEM((HG_HEADS, HG_DV, HG_DK), f32)],
        compiler_params=_params("arbitrary"),
        name="even_pre_hgrn",
    )(x, mod, g, w_bf, wkvt_bf, lb_logits, gnorm, s0)


def _ffn(hn_bf, wg_ref, wu_ref, wd_ref):
    g = _dot(hn_bf, wg_ref[...])
    u = _dot(hn_bf, wu_ref[...])
    return _dot((_silu(g) * u).astype(bf16), wd_ref[...])


def _resident(shape):
    return pl.BlockSpec(shape, lambda *_: (0,) * len(shape), pipeline_mode=pl.Buffered(1))


def _ffn_specs(layer):
    one = lambda rows, cols: pl.BlockSpec((None, rows, cols), lambda *_: (layer, 0, 0),
                                          pipeline_mode=pl.Buffered(1))
    return [one(D_MODEL, D_FF), one(D_MODEL, D_FF), one(D_FF, D_MODEL)]


def _even_post_kernel(x_ref, mod_ref, oa_ref, ob_ref, wo_ref, gf_ref, wg_ref, wu_ref, wd_ref, out_ref):
    nb, tt, d = x_ref.shape
    m = nb * tt
    o = (_dot(oa_ref[...].reshape(m, SB_WIDTH), wo_ref[:SB_WIDTH, :])
         + _dot(ob_ref[...].reshape(m, HG_W), wo_ref[SB_WIDTH:, :]))
    x1 = x_ref[...] + (1.0 + mod_ref[:, 2:3, :]) * o.reshape(nb, tt, d)
    hn = _norm_mod(x1, gf_ref[...], mod_ref[:, 4:5, :], mod_ref[:, 3:4, :])
    y = _ffn(hn.reshape(m, d).astype(bf16), wg_ref, wu_ref, wd_ref)
    out_ref[...] = x1 + (1.0 + mod_ref[:, 5:6, :]) * y.reshape(nb, tt, d)


def _even_post(x, mod, oa, ob, wo, gf, wg, wu, wd, layer, nb, tt):
    b, t, d = x.shape
    blk = lambda width: pl.BlockSpec((nb, tt, width), lambda i, tb: (i, tb, 0))
    return pl.pallas_call(
        _even_post_kernel,
        grid=(b // nb, t // tt),
        in_specs=[blk(d), pl.BlockSpec((nb, N_MOD, d), lambda i, tb: (i, 0, 0)),
                  blk(SB_WIDTH), blk(HG_W), _resident((d, d)), _resident((1, d))] + _ffn_specs(layer),
        out_specs=blk(d),
        out_shape=jax.ShapeDtypeStruct((b, t, d), f32),
        compiler_params=_params("arbitrary", "arbitrary"),
        name="even_post_ffn",
    )(x, mod, oa, ob, wo, gf, wg, wu, wd)


def _even_post_attn_kernel(n_tb, x_ref, mod_ref, ob_ref, q_ref, k_ref, v_ref, wo_ref, gf_ref,
                           wg_ref, wu_ref, wd_ref, out_ref, oa_scr, reach_scr):
    _, tt, d = x_ref.shape
    tq = LANES
    n_q = tt // tq
    n_pair = SB_WIDTH // LANES
    s = pl.program_id(0)
    last = pl.num_programs(0) - 1
    i0 = lax.rem(jnp.minimum(s, last - 1), n_tb) * n_q
    consts = _sb_prompt_consts(tq)
    pair_cols = [slice(p * LANES, (p + 1) * LANES) for p in range(n_pair)]
    k_of = [lambda j, cols=cols: k_ref[j, cols, :] for cols in pair_cols]
    v_of = [lambda j, cols=cols: v_ref[j, cols, :] for cols in pair_cols]

    def post_units(oa):
        o = _dot(oa, wo_ref[:SB_WIDTH, :])
        yield
        o = o + _dot(ob_ref[...].reshape(tt, HG_W), wo_ref[SB_WIDTH:, :])
        yield
        x1 = x_ref[...] + (1.0 + mod_ref[:, 2:3, :]) * o.reshape(1, tt, d)
        hn = _norm_mod(x1, gf_ref[...], mod_ref[:, 4:5, :], mod_ref[:, 3:4, :])
        hn = hn.reshape(tt, d).astype(bf16)
        yield
        acts = []
        for j in range(D_FF // MXU_COLS):
            cols = slice(j * MXU_COLS, (j + 1) * MXU_COLS)
            g = _dot(hn, wg_ref[:, cols])
            u = _dot(hn, wu_ref[:, cols])
            acts.append((_silu(g) * u).astype(bf16))
            yield
        act = jnp.concatenate(acts, axis=1)
        for n in range(d // MXU_COLS):
            cols = slice(n * MXU_COLS, (n + 1) * MXU_COLS)
            y = _dot(act, wd_ref[:, cols]).reshape(1, tt, MXU_COLS)
            out_ref[:, :, cols] = x1[:, :, cols] + (1.0 + mod_ref[:, 5:6, cols]) * y
            yield

    def attn_units(out):
        for p, cols in enumerate(pair_cols):
            qs = [q_ref[0, g * tq:(g + 1) * tq, cols] for g in range(n_q)]
            groups = _sb_prompt_groups(qs, k_of[p], v_of[p], i0, consts)
            res = {}
            yield from _sb_groups_units(groups, res)
            out[p] = res["results"]
            yield

    def publish_attn(results):
        reach = None
        for p, cols in enumerate(pair_cols):
            for g, (carry, acc) in enumerate(results[p]):
                oa_scr[g * tq:(g + 1) * tq, cols] = _unstack_heads(acc).astype(oa_scr.dtype)
                reach = carry if reach is None else jnp.maximum(reach, carry)
        reach_scr[...] = reach

    @pl.when(s == 0)
    def _():
        results = {}
        _round_robin(attn_units(results))
        publish_attn(results)

    @pl.when((s > 0) & (s < last))
    def _():
        results = {}
        _round_robin(post_units(oa_scr[...]), attn_units(results), units=[3, 2])
        publish_attn(results)

    @pl.when(s == last)
    def _():
        _round_robin(post_units(oa_scr[...]))

    @pl.when((s < last) & (jnp.max(reach_scr[...]) > SB_SKIP_LOG))
    def _():
        for p, cols in enumerate(pair_cols):
            def one(g, carry, p=p, cols=cols):
                rows = pl.ds(pl.multiple_of(g * tq, tq), tq)
                groups = _sb_prompt_groups([q_ref[0, rows, cols]], k_of[p], v_of[p], i0 + g, consts)
                o = _sb_prompt_finish(groups[0], _sb_groups(groups)[0], i0 + g, k_of[p], v_of[p], consts)
                oa_scr[rows, cols] = o.astype(oa_scr.dtype)
                return carry

            lax.fori_loop(0, n_q, one, 0)


def _even_post_attn(x, mod, ob, q, k_blocks, v_blocks, wo, gf, wg, wu, wd, layer, tt):
    b, t, d = x.shape
    n_tb = t // tt
    n_blocks = b * n_tb
    attn_blk = lambda s: jnp.minimum(s, n_blocks - 1)
    post_blk = lambda s: jnp.maximum(s - 1, 0)
    rows = lambda width, which: pl.BlockSpec((1, tt, width), lambda s: (which(s) // n_tb, which(s) % n_tb, 0))
    keys = pl.BlockSpec((None, t // LANES, SB_WIDTH, LANES), lambda s: (attn_blk(s) // n_tb, 0, 0, 0),
                        pipeline_mode=pl.Buffered(1))
    return pl.pallas_call(
        functools.partial(_even_post_attn_kernel, n_tb),
        grid=(n_blocks + 1,),
        in_specs=[rows(d, post_blk), pl.BlockSpec((1, N_MOD, d), lambda s: (post_blk(s) // n_tb, 0, 0)),
                  rows(HG_W, post_blk), rows(SB_WIDTH, attn_blk), keys, keys,
                  _resident((d, d)), _resident((1, d))] + _ffn_specs(layer),
        out_specs=rows(d, post_blk),
        out_shape=jax.ShapeDtypeStruct((b, t, d), f32),
        scratch_shapes=[pltpu.VMEM((tt, SB_WIDTH), bf16), pltpu.VMEM((2 * LANES, LANES), f32)],
        compiler_params=_params("arbitrary"),
        name="even_attn_post_ffn",
    )(x, mod, ob, q, k_blocks, v_blocks, wo, gf, wg, wu, wd)


def _linear_scan(a, u, h_init, tt):
    m, d = a.shape
    n_group = m // SUBLANES
    a3 = a.reshape(n_group, SUBLANES, d)
    u3 = u.reshape(n_group, SUBLANES, d)
    sub = lax.broadcasted_iota(jnp.int32, (1, SUBLANES, d), 1)
    step = 1
    while step < SUBLANES:
        keep = sub >= step
        a_prev = jnp.where(keep, pltpu.roll(a3, step, 1), 1.0)
        u_prev = jnp.where(keep, pltpu.roll(u3, step, 1), 0.0)
        u3 = a3 * u_prev + u3
        a3 = a3 * a_prev
        step *= 2
    groups_per_seq = tt // SUBLANES
    carry = None
    groups = []
    for gi in range(n_group):
        if gi % groups_per_seq == 0:
            carry = h_init[gi // groups_per_seq]
        h_group = u3[gi] + a3[gi] * carry
        groups.append(h_group)
        carry = h_group[SUBLANES - 1:SUBLANES, :]
    return jnp.concatenate(groups, axis=0)


def _odd_kernel(stream_start, pipelined, n_tb, x_ref, mod_ref, mod_ffn_ref, gm_ref, win_ref, cw_ref, cb_ref,
                wax_ref, ba_ref, bx_ref, lam_ref, wo_ref, conv0_ref, h0_ref, gf_ref, fn_ref,
                wg_ref, wu_ref, wd_ref, y_ref, conv_ref, hout_ref, xs_scr, hprev_scr, x1_scr, hn_scr):
    nb, tt, d = x_ref.shape
    m = nb * tt
    s = pl.program_id(0)
    tb = lax.rem(s, n_tb)

    def ffn_units(x1, hn_bf, out):
        acts = []
        for j in range(D_FF // MXU_COLS):
            cols = slice(j * MXU_COLS, (j + 1) * MXU_COLS)
            g = _dot(hn_bf, wg_ref[:, cols])
            u = _dot(hn_bf, wu_ref[:, cols])
            acts.append((_silu(g) * u).astype(bf16))
            yield
        act = jnp.concatenate(acts, axis=1)
        parts = []
        for n in range(d // MXU_COLS):
            cols = slice(n * MXU_COLS, (n + 1) * MXU_COLS)
            y = _dot(act, wd_ref[:, cols]).reshape(nb, tt, MXU_COLS)
            parts.append(x1[:, :, cols] + (1.0 + mod_ffn_ref[:, 5:6, cols]) * y)
            yield
        x2 = jnp.concatenate(parts, axis=-1)
        ms = jnp.mean(x2 * x2, axis=-1, keepdims=True)
        out["y"] = x2 * lax.rsqrt(ms + EPS) * fn_ref[...]

    def mixer_units(out):
        x = x_ref[...]
        hn = _norm_mod(x, gm_ref[...], mod_ref[:, 1:2, :], mod_ref[:, 0:1, :])
        hn = hn.reshape(m, d).astype(bf16)
        yield
        gate = _dot(hn, win_ref[:, :d])
        yield
        xbr = _dot(hn, win_ref[:, d:]).reshape(nb, tt, d)
        yield
        xs_scr[:, HALO:HALO + tt, :] = xbr
        xc = cb_ref[...] + xbr * cw_ref[CONV_W - 1:CONV_W, :]
        for k in range(1, CONV_W):
            xc = xc + xs_scr[:, HALO - k:HALO - k + tt, :] * cw_ref[CONV_W - 1 - k:CONV_W - k, :]
        tail = xs_scr[:, HALO + tt - (CONV_W - 1):HALO + tt, :]
        xs_scr[:, HALO - (CONV_W - 1):HALO, :] = tail
        xc = xc.reshape(m, d)
        xc_bf = xc.astype(bf16)
        yield
        lam = lam_ref[...]
        log_sig_lam = -(jnp.maximum(-lam, 0.0) + jnp.log1p(jnp.exp(-jnp.abs(lam))))
        y_in, h_last = [], []
        for blk in range(LRU_BLOCKS):
            cols = slice(blk * LRU_BLK, (blk + 1) * LRU_BLK)
            both = _dot(xc_bf[:, cols], wax_ref[blk])
            r = _sigmoid(both[:, :LRU_BLK] + ba_ref[:, cols])
            gi = _sigmoid(both[:, LRU_BLK:] + bx_ref[:, cols])
            log_a = C_SCALE * r * log_sig_lam[:, cols]
            a = jnp.exp(log_a)
            mult = jnp.sqrt(-jnp.tanh(log_a) * (1.0 + a * a))
            if stream_start:
                row = (lax.broadcasted_iota(jnp.int32, (m, LRU_BLK), 0) & (tt - 1)) + tb * tt
                mult = jnp.where(row == 0, 1.0, mult)
            h = _linear_scan(a, mult * gi * xc[:, cols], hprev_scr[:, :, cols], tt)
            h_last.append(h.reshape(nb, tt, LRU_BLK)[:, tt - 1:tt, :])
            y_in.append((jax.nn.gelu(gate[:, cols]) * h).astype(bf16))
            yield
        h_last = jnp.concatenate(h_last, axis=-1)
        hprev_scr[...] = h_last
        y_in = jnp.concatenate(y_in, axis=1)
        o = _dot(y_in, wo_ref[...])
        yield
        x1 = x + (1.0 + mod_ref[:, 2:3, :]) * o.reshape(nb, tt, d)
        hn2 = _norm_mod(x1, gf_ref[...], mod_ref[:, 4:5, :], mod_ref[:, 3:4, :])
        out.update(x1=x1, hn2=hn2.reshape(m, d).astype(bf16), tail=tail, h_last=h_last)

    @pl.when(tb == 0)
    def _():
        xs_scr[:, HALO - (CONV_W - 1):HALO, :] = conv0_ref[...]
        hprev_scr[...] = h0_ref[...]

    res = {}
    if pipelined:
        last = pl.num_programs(0) - 1

        def publish_mixer(out):
            x1_scr[...] = out["x1"]
            hn_scr[...] = out["hn2"]
            conv_ref[...] = out["tail"]
            hout_ref[...] = out["h_last"]

        @pl.when(s == 0)
        def _():
            out = {}
            _round_robin(mixer_units(out))
            publish_mixer(out)

        @pl.when((s > 0) & (s < last))
        def _():
            out = {}
            _round_robin(ffn_units(x1_scr[...], hn_scr[...], out), mixer_units(out))
            y_ref[...] = out["y"]
            publish_mixer(out)

        @pl.when(s == last)
        def _():
            out = {}
            _round_robin(ffn_units(x1_scr[...], hn_scr[...], out))
            y_ref[...] = out["y"]
    else:
        for _ in mixer_units(res):
            pass
        conv_ref[...] = res["tail"]
        hout_ref[...] = res["h_last"]
        for _ in ffn_units(res["x1"], res["hn2"], res):
            pass
        y_ref[...] = res["y"]


def _odd_layer(x, mod, gm, win, cw, cb, wax, ba, bx, lam, wo, conv0, h0, gf, fn, wg, wu, wd,
               layer, nb, tt, stream_start):
    b, t, d = x.shape
    n_tb = t // tt
    n_blocks = (b // nb) * n_tb
    pipelined = n_blocks > 1
    mix_blk = lambda s: jnp.minimum(s, n_blocks - 1)
    ffn_blk = (lambda s: jnp.maximum(s - 1, 0)) if pipelined else mix_blk
    rows_spec = lambda which: pl.BlockSpec((nb, tt, d), lambda s: (which(s) // n_tb, which(s) % n_tb, 0))
    per_seq = lambda rows, which: pl.BlockSpec((nb, rows, d), lambda s: (which(s) // n_tb, 0, 0))
    row = _resident((1, d))
    return pl.pallas_call(
        functools.partial(_odd_kernel, stream_start, pipelined, n_tb),
        grid=(n_blocks + (1 if pipelined else 0),),
        in_specs=[rows_spec(mix_blk), per_seq(N_MOD, mix_blk), per_seq(N_MOD, ffn_blk), row,
                  _resident((d, 2 * d)), _resident((CONV_W, d)),
                  row, _resident((LRU_BLOCKS, LRU_BLK, 2 * LRU_BLK)), row,
                  row, row, _resident((d, d)), per_seq(CONV_W - 1, mix_blk), per_seq(1, mix_blk),
                  row, row] + _ffn_specs(layer),
        out_specs=[rows_spec(ffn_blk), per_seq(CONV_W - 1, mix_blk), per_seq(1, mix_blk)],
        out_shape=[jax.ShapeDtypeStruct((b, t, d), f32),
                   jax.ShapeDtypeStruct((b, CONV_W - 1, d), f32),
                   jax.ShapeDtypeStruct((b, 1, d), f32)],
        scratch_shapes=[pltpu.VMEM((nb, HALO + tt, d), f32), pltpu.VMEM((nb, 1, d), f32),
                        pltpu.VMEM((nb, tt, d) if pipelined else (1, SUBLANES, LANES), f32),
                        pltpu.VMEM((nb * tt, d) if pipelined else (2 * SUBLANES, LANES), bf16)],
        compiler_params=_params("arbitrary"),
        name="odd_layer_ffn",
    )(x, mod, mod, gm, win, cw, cb, wax, ba, bx, lam, wo, conv0, h0, gf, fn, wg, wu, wd)


def _trunk(x, mods, past, weights, nb, tt, hg_tt):
    b, t, d = x.shape
    sample = past is not None
    row = lambda v: v.reshape(1, -1)

    if sample:
        past_k, past_v, s0, conv0, h0 = past
        qa, ka, va, ka_bf, va_bf, qb, lf, kb, ib, gs = _even_pre(
            x, mods[0], row(weights["norm_mix"][0]), weights["w_in_even"],
            weights["hg_lb_logits"], 0, nb, tt)
        oa = _sb_sample(qa, ka_bf, va_bf, past_k, past_v, LANES)
        ob, s_new = _hgrn(qb, lf, kb, ib, gs, row(weights["hg_gnorm"]), s0, hg_tt)
        x = _even_post(x, mods[0], oa, ob, weights["w_out_even"], row(weights["norm_ffn"][0]),
                       weights["ffn_wg"], weights["ffn_wu"], weights["ffn_wd"], 0, nb, tt)
        heads = lambda a: a.reshape(1, b, t, SB_HEADS, SB_DIM)
    else:
        s0 = jnp.zeros((b, HG_HEADS, HG_DK, HG_DV), f32)
        conv0 = jnp.zeros((b, CONV_W - 1, d), f32)
        h0 = jnp.zeros((b, 1, d), f32)
        qa, ka, va, ka_bf, va_bf, ob, s_new = _even_fused(
            x, mods[0], row(weights["norm_mix"][0]), weights["w_in_even"], weights["w_kv_t"],
            weights["hg_lb_logits"], row(weights["hg_gnorm"]), s0, 0, tt)
        x = _even_post_attn(x, mods[0], ob, qa, ka_bf, va_bf, weights["w_out_even"],
                            row(weights["norm_ffn"][0]), weights["ffn_wg"], weights["ffn_wu"],
                            weights["ffn_wd"], 0, tt)
        heads = lambda a: a.reshape(b, SB_HEADS, SB_DIM, t).transpose(0, 3, 1, 2)[None]

    y, conv_new, h_new = _odd_layer(
        x, mods[1], row(weights["norm_mix"][1]), weights["w_in_odd"], weights["conv_w"],
        row(weights["conv_b"]), weights["lru_wax"], row(weights["lru_ba"]), row(weights["lru_bx"]),
        row(weights["lru_lambda"]), weights["w_out_odd"], conv0, h0,
        row(weights["norm_ffn"][1]), row(weights["final_norm"]),
        weights["ffn_wg"], weights["ffn_wu"], weights["ffn_wd"],
        1, nb, tt, not sample)

    return (y, heads(ka), heads(va), s_new[None], conv_new[None], h_new.reshape(1, b, d))


def kernel(x_prompt, x_sample, cache_sb_k, cache_sb_v, state_hgrn, state_conv, state_lru, c_prompt, c_sample, norm_mix, norm_ffn, w_ada, b_ada, w_in_even, w_out_even, hg_gnorm, hg_lb_logits, w_in_odd, conv_w, conv_b, lru_wa, lru_ba, lru_wx, lru_bx, lru_lambda, w_out_odd, ffn_wg, ffn_wu, ffn_wd, final_norm):
    bp = x_prompt.shape[0]
    bs, ts, d = x_sample.shape
    w_in_even_bf = w_in_even[0].astype(bf16)
    weights = {
        "norm_mix": norm_mix, "norm_ffn": norm_ffn,
        "w_in_even": w_in_even_bf, "w_out_even": w_out_even[0].astype(bf16),
        "w_kv_t": w_in_even_bf[:, SEG:3 * SEG].T.reshape(2, SEG, d),
        "hg_gnorm": hg_gnorm[0], "hg_lb_logits": hg_lb_logits,
        "w_in_odd": w_in_odd[0].astype(bf16), "conv_w": conv_w[0], "conv_b": conv_b[0],
        "lru_wax": jnp.concatenate([lru_wa[0], lru_wx[0]], axis=-1).astype(bf16),
        "lru_ba": lru_ba[0], "lru_bx": lru_bx[0], "lru_lambda": lru_lambda[0],
        "w_out_odd": w_out_odd[0].astype(bf16),
        "ffn_wg": ffn_wg.astype(bf16), "ffn_wu": ffn_wu.astype(bf16), "ffn_wd": ffn_wd.astype(bf16),
        "final_norm": final_norm,
    }
    mods = _ada(jnp.concatenate([c_prompt, c_sample], axis=0), w_ada, b_ada)
    mods = mods.reshape(DEPTH, bp + bs, N_MOD, d)

    out_p = _trunk(x_prompt, mods[:, :bp], None, weights, nb=1, tt=512, hg_tt=512)
    cache_t = lambda c: c[0].transpose(0, 2, 3, 1).reshape(bs, SB_WIDTH, -1)
    past = (cache_t(cache_sb_k), cache_t(cache_sb_v),
            state_hgrn[0], state_conv[0], state_lru[0].reshape(bs, 1, d))
    out_s = _trunk(x_sample, mods[:, bp:], past, weights, nb=bs, tt=ts, hg_tt=ts)

    y_p, k_p, v_p, s_p, conv_p, h_p = out_p
    y_s, k_s, v_s, s_s, conv_s, h_s = out_s
    return (y_p, y_s, k_p, v_p, s_p, conv_p, h_p, k_s, v_s, s_s, conv_s, h_s)
```

```python
import functools
import math

import jax
import jax.numpy as jnp
from jax import lax
from jax.experimental import pallas as pl
from jax.experimental.pallas import tpu as pltpu

f32 = jnp.float32
bf16 = jnp.bfloat16

D_MODEL = 1024
DEPTH = 2
EPS = 1e-6
N_MOD = 6

SB_HEADS = 8
SB_DIM = 64
SB_WIDTH = SB_HEADS * SB_DIM
HG_HEADS = 4
HG_DK = 128
HG_DV = 128
HG_W = HG_HEADS * HG_DK
SEG = 512
N_SEG = 7
HG_CHUNK = 64
HG_SUB = 32
HG_MATMUL_MIN_LOG = -40.0
LRU_BLOCKS = 8
LRU_BLK = D_MODEL // LRU_BLOCKS
CONV_W = 4
C_SCALE = 8.0
D_FF = 2816

SB_WINDOW = 2
SB_SKIP_LOG = -105.0

MXU_COLS = 256

LANES = 128
SUBLANES = 8
HALO = 8
VMEM_LIMIT = 60 * 1024 * 1024

NT_DIMS = (((1,), (1,)), ((), ()))
TN_DIMS = (((0,), (0,)), ((), ()))


def _dot(a, b):
    return jnp.dot(a, b, preferred_element_type=f32)


def _dot_nt(a, b):
    return lax.dot_general(a, b, NT_DIMS, preferred_element_type=f32)


def _split3(x):
    hi = x.astype(bf16)
    r1 = x - hi.astype(f32)
    mid = r1.astype(bf16)
    lo = (r1 - mid.astype(f32)).astype(bf16)
    return hi, mid, lo


def _split2(x):
    hi = x.astype(bf16)
    return hi, (x - hi.astype(f32)).astype(bf16)


def _sigmoid(x):
    return jax.nn.sigmoid(x)


def _silu(x):
    return x * _sigmoid(x)


def _norm_mod(x, g, scale, shift):
    ms = jnp.mean(x * x, axis=-1, keepdims=True)
    return x * lax.rsqrt(ms + EPS) * (g * (1.0 + scale)) + shift


def _params(*sem):
    return pltpu.CompilerParams(dimension_semantics=sem, vmem_limit_bytes=VMEM_LIMIT)


def _ada_kernel(c_ref, w_ref, b_ref, o_ref):
    sc = _silu(c_ref[...])
    o_ref[...] = _dot(sc.astype(bf16), w_ref[...].astype(bf16)) + b_ref[...]


def _ada(c_all, w_ada, b_ada):
    rows = c_all.shape[0]
    n_out = N_MOD * D_MODEL
    tn = 1536
    return pl.pallas_call(
        _ada_kernel,
        grid=(DEPTH, n_out // tn),
        in_specs=[
            pl.BlockSpec((rows, D_MODEL), lambda l, j: (0, 0)),
            pl.BlockSpec((None, D_MODEL, tn), lambda l, j: (l, 0, j)),
            pl.BlockSpec((None, 1, tn), lambda l, j: (l, 0, j)),
        ],
        out_specs=pl.BlockSpec((None, rows, tn), lambda l, j: (l, 0, j)),
        out_shape=jax.ShapeDtypeStruct((DEPTH, rows, n_out), f32),
        compiler_params=_params("arbitrary", "arbitrary"),
        name="ada_mod",
    )(c_all, w_ada, b_ada.reshape(DEPTH, 1, n_out))


def _even_pre_kernel(layer, x_ref, mod_ref, g_ref, w_ref, lbl_ref,
                     qa_ref, ka_ref, va_ref, kab_ref, vab_ref,
                     qb_ref, lf_ref, kb_ref, ib_ref, gs_ref):
    nb, tt, d = x_ref.shape
    m = nb * tt
    hn = _norm_mod(x_ref[...], g_ref[...], mod_ref[:, 1:2, :], mod_ref[:, 0:1, :])
    hn = hn.reshape(m, d).astype(bf16)

    def seg(i):
        return _dot(hn, w_ref[:, i * SEG:(i + 1) * SEG]).reshape(nb, tt, SEG)

    qa_ref[...] = (seg(0) * (1.0 / math.sqrt(SB_DIM))).astype(bf16)
    ka = seg(1)
    ka_ref[...] = ka
    kab_ref[...] = ka.astype(bf16)
    va = seg(2)
    va_ref[...] = va
    vab_ref[...] = va.astype(bf16)
    qb_ref[...] = _silu(seg(3)) * (HG_DK ** -0.5)

    lg = lbl_ref[...]
    e = jnp.exp(lg - jnp.max(lg, axis=0, keepdims=True))
    lb = jnp.sum(e[:layer + 1], axis=0, keepdims=True) / jnp.sum(e, axis=0, keepdims=True)
    fb = seg(4)
    f = lb + (1.0 - lb) * _sigmoid(fb)
    lf_ref[...] = jnp.log(f)
    kb_ref[...] = (1.0 - lb) * _sigmoid(-fb)
    ib_ref[...] = seg(5)
    gs_ref[...] = _silu(seg(6))


def _even_pre(x, mod, g, w_bf, lb_logits, layer, nb, tt):
    b, t, d = x.shape
    grid = (b // nb, t // tt)
    blk = lambda width: pl.BlockSpec((nb, tt, width), lambda i, j: (i, j, 0))
    out_f = jax.ShapeDtypeStruct((b, t, SEG), f32)
    out_b = jax.ShapeDtypeStruct((b, t, SEG), bf16)
    return pl.pallas_call(
        functools.partial(_even_pre_kernel, layer),
        grid=grid,
        in_specs=[
            blk(d),
            pl.BlockSpec((nb, N_MOD, d), lambda i, j: (i, 0, 0)),
            _resident((1, d)),
            _resident((d, N_SEG * SEG)),
            _resident((DEPTH + 1, HG_W)),
        ],
        out_specs=[blk(SEG)] * 10,
        out_shape=[out_b, out_f, out_f, out_b, out_b, out_f, out_f, out_f, out_f, out_f],
        compiler_params=_params("arbitrary", "arbitrary"),
        name="even_pre",
    )(x, mod, g, w_bf, lb_logits)


def _sb_consts(tk, scale=1.0):
    r = lax.broadcasted_iota(jnp.int32, (tk, LANES + tk), 0)
    c = lax.broadcasted_iota(jnp.int32, (tk, LANES + tk), 1)
    m = jnp.where((c < LANES) | (r > c - LANES), scale, 0.0).astype(bf16)
    return jnp.concatenate([m, m], axis=0)


def _sb_groups_units(groups, out):
    zs = [[_dot(q2, k_blk) if transposed else _dot_nt(q2, k_blk)
           for k_blk, _, _, _, transposed in blocks] for q2, blocks, _, _ in groups]
    yield
    l1mbs, sums = [], []
    for g, (_, blocks, _, _) in enumerate(groups):
        l1mbs.append([])
        sums.append([])
        for z, (_, _, sum_matrix, mask, _) in zip(zs[g], blocks):
            l1mb = -(jnp.maximum(z, 0.0) + jnp.log(1.0 + jnp.exp(-jnp.abs(z))))
            if mask is not None:
                l1mb = jnp.where(mask, l1mb, 0.0)
            l1mbs[g].append(l1mb)
            sums[g].append(_dot(jnp.concatenate(_split2(l1mb), axis=1), sum_matrix))
    yield
    results = []
    for g, (_, blocks, carry, acc) in enumerate(groups):
        for z, l1mb, s, (_, v_blk, _, mask, transposed) in zip(zs[g], l1mbs[g], sums[g], blocks):
            tk = z.shape[1]
            after = carry[:, :tk] + s[:, LANES:]
            w = jnp.exp(z + l1mb + after)
            if mask is not None:
                w = jnp.where(mask, w, 0.0)
            w = w.astype(bf16)
            acc = acc + (_dot_nt(w, v_blk) if transposed else _dot(w, v_blk))
            carry = carry + s[:, :LANES]
        results.append((carry, acc))
    out["results"] = results


def _sb_groups(groups):
    out = {}
    for _ in _sb_groups_units(groups, out):
        pass
    return out["results"]


def _sb_blocks(q2, blocks, carry, acc):
    return _sb_groups([(q2, blocks, carry, acc)])[0]


def _sb_finish(j0, carry, acc, step):
    def cond(state):
        j, carry, _ = state
        return (j >= 0) & (jnp.max(carry) > SB_SKIP_LOG)

    def body(state):
        j, carry, acc = state
        carry, acc = step(j, carry, acc)
        return j - 1, carry, acc

    return lax.while_loop(cond, body, (j0, carry, acc))[2]


def _stack_heads(q):
    lane = lax.broadcasted_iota(jnp.int32, q.shape, 1)
    zero = jnp.zeros_like(q)
    return jnp.concatenate([jnp.where(lane < SB_DIM, q, zero), jnp.where(lane >= SB_DIM, q, zero)], axis=0)


def _unstack_heads(acc2):
    tq = acc2.shape[0] // 2
    lane = lax.broadcasted_iota(jnp.int32, (tq, LANES), 1)
    return jnp.where(lane < SB_DIM, acc2[:tq], acc2[tq:])


def _causal_mask2(tq):
    r = lax.broadcasted_iota(jnp.int32, (2 * tq, tq), 0) & (tq - 1)
    c = lax.broadcasted_iota(jnp.int32, (2 * tq, tq), 1)
    return c < r


def _sb_prompt_groups(qs, k_of, v_of, i0, consts):
    sum_matrix, diag_mask, zero = consts
    tq = qs[0].shape[0]
    groups = []
    for g, q in enumerate(qs):
        i = i0 + g
        blocks = [(k_of(i), v_of(i), sum_matrix, diag_mask, True)]
        for c in range(1, SB_WINDOW + 1):
            scale = jnp.where(i >= c, 1.0, 0.0)
            j = jnp.maximum(i - c, 0)
            blocks.append((k_of(j), (v_of(j).astype(f32) * scale).astype(bf16), _sb_consts(tq, scale), None, True))
        groups.append((_stack_heads(q), blocks, zero, zero))
    return groups


def _sb_prompt_consts(tq):
    return _sb_consts(tq), _causal_mask2(tq), jnp.zeros((2 * tq, LANES), f32)


def _sb_prompt_finish(group, result, i, k_of, v_of, consts):
    q2 = group[0]
    carry, acc = result

    def step(j, carry, acc):
        return _sb_blocks(q2, [(k_of(j), v_of(j), consts[0], None, True)], carry, acc)

    return _unstack_heads(_sb_finish(i - SB_WINDOW - 1, carry, acc, step))


def _sb_sample_kernel(tk, n_past, q_ref, k_ref, v_ref, wk_ref, wv_ref, pk_hbm, pv_hbm, o_ref,
                      kbuf, vbuf, sem):
    tq = q_ref.shape[0]
    b = pl.program_id(0)
    zero = jnp.zeros((2 * tq, LANES), f32)
    sum_matrix = _sb_consts(tk)
    diag_matrix = _sb_consts(tq)
    diag_mask = _causal_mask2(tq)

    def fetch(j):
        keys = pl.ds(pl.multiple_of(j * tk, tk), tk)
        copies = [pltpu.make_async_copy(pk_hbm.at[b, :, keys], kbuf, sem.at[0]),
                  pltpu.make_async_copy(pv_hbm.at[b, :, keys], vbuf, sem.at[1])]
        for cp in copies:
            cp.start()
        for cp in copies:
            cp.wait()

    groups = []
    for pair in range(SB_WIDTH // LANES):
        cols = slice(pair * LANES, (pair + 1) * LANES)
        q2 = _stack_heads(q_ref[:, cols])
        blocks = [(k_ref[:, cols], v_ref[:, cols], diag_matrix, diag_mask, False)]
        for c in range(1, SB_WINDOW + 1):
            keys = slice((SB_WINDOW - c) * tk, (SB_WINDOW - c + 1) * tk)
            blocks.append((wk_ref[cols, keys].astype(bf16), wv_ref[cols, keys].astype(bf16),
                           sum_matrix, None, True))
        groups.append((q2, blocks, zero, zero))

    for pair, (carry, acc) in enumerate(_sb_groups(groups)):
        cols = slice(pair * LANES, (pair + 1) * LANES)
        q2 = groups[pair][0]

        def step(j, carry, acc, cols=cols, q2=q2):
            fetch(j)
            blk = (kbuf[cols, :].astype(bf16), vbuf[cols, :].astype(bf16), sum_matrix, None, True)
            return _sb_blocks(q2, [blk], carry, acc)

        acc = _sb_finish(n_past - SB_WINDOW - 1, carry, acc, step)
        o_ref[:, cols] = _unstack_heads(acc).astype(o_ref.dtype)


def _sb_sample(q, k, v, past_kt, past_vt, tk):
    b, t, _ = q.shape
    n_past = past_kt.shape[2] // tk
    assert n_past % SB_WINDOW == 0 and n_past >= SB_WINDOW
    qspec = pl.BlockSpec((None, t, SB_WIDTH), lambda bi: (bi, 0, 0))
    wspec = pl.BlockSpec((None, SB_WIDTH, SB_WINDOW * tk), lambda bi: (bi, 0, n_past // SB_WINDOW - 1))
    hbm = pl.BlockSpec(memory_space=pl.ANY)
    return pl.pallas_call(
        functools.partial(_sb_sample_kernel, tk, n_past),
        grid=(b,),
        in_specs=[qspec, qspec, qspec, wspec, wspec, hbm, hbm],
        out_specs=qspec,
        out_shape=jax.ShapeDtypeStruct((b, t, SB_WIDTH), bf16),
        scratch_shapes=[pltpu.VMEM((SB_WIDTH, tk), f32), pltpu.VMEM((SB_WIDTH, tk), f32),
                        pltpu.SemaphoreType.DMA((2,))],
        compiler_params=_params("arbitrary"),
        name="sb_sample",
    )(q, k, v, past_kt, past_vt, past_kt, past_vt)


def _hgrn_head_chunk(q, k, v, b, st):
    n_sub = HG_CHUNK // HG_SUB
    v_bf = v.astype(bf16)
    o_inter = _dot_nt((q * jnp.exp(b)).astype(bf16), st.astype(bf16))
    t_idx = lax.broadcasted_iota(jnp.int32, (HG_SUB, HG_SUB, HG_DK), 0)
    s_idx = lax.broadcasted_iota(jnp.int32, (HG_SUB, HG_SUB, HG_DK), 1)
    causal = s_idx <= t_idx
    outs = []
    for i in range(n_sub):
        lo = i * HG_SUB
        rows = slice(lo, lo + HG_SUB)
        qi, ki, bi = q[rows], k[rows], b[rows]
        diff = bi[:, None, :] - bi[None, :, :]
        dec = jnp.exp(jnp.where(causal, diff, -1e30))
        a_diag = jnp.sum(qi[:, None, :] * ki[None, :, :] * dec, axis=-1)
        o_i = _dot(a_diag.astype(bf16), v_bf[rows])
        if i > 0:
            ref = b[lo - 1:lo]
            q_t = qi * jnp.exp(bi - ref)
            k_t = k[:lo] * jnp.exp(ref - b[:lo])
            a_off = _dot_nt(q_t.astype(bf16), k_t.astype(bf16))
            o_i = o_i + _dot(a_off.astype(bf16), v_bf[:lo])
        outs.append(o_inter[rows] + o_i)
    o = jnp.concatenate(outs, axis=0)
    b_last = b[HG_CHUNK - 1:HG_CHUNK]
    k_dec = (k * jnp.exp(b_last - b)).astype(bf16)
    st_new = jnp.exp(b_last) * st + lax.dot_general(v_bf, k_dec, TN_DIMS, preferred_element_type=f32)
    return o, st_new


def _hgrn_intra_units(qs, ks, vs, bs, out):
    n_sub = HG_CHUNK // HG_SUB
    heads = range(len(qs))
    q_in, q_t, k_t, k_dec, v_bf, b_last = [], [], [], [], [], []
    for h in heads:
        q, k, b = qs[h], ks[h], bs[h]
        q_in.append((q * jnp.exp(b)).astype(bf16))
        q_t.append([])
        k_t.append([])
        for i in range(n_sub):
            lo, hi = i * HG_SUB, (i + 1) * HG_SUB
            if i == 0:
                q_t[h].append(q_in[h][:hi])
                k_t[h].append((k[:hi] * jnp.exp(-b[:hi])).astype(bf16))
            else:
                ref = b[lo - 1:lo]
                q_t[h].append((q[lo:hi] * jnp.exp(b[lo:hi] - ref)).astype(bf16))
                k_t[h].append((k[:hi] * jnp.exp(ref - b[:hi])).astype(bf16))
        b_last.append(b[HG_CHUNK - 1:HG_CHUNK])
        k_dec.append((k * jnp.exp(b_last[h] - b)).astype(bf16))
        v_bf.append(vs[h].astype(bf16))
    yield

    attn = [[_dot_nt(q_t[h][i], k_t[h][i]) for i in range(n_sub)] for h in heads]
    st_add = [lax.dot_general(v_bf[h], k_dec[h], TN_DIMS, preferred_element_type=f32) for h in heads]
    yield

    o_intra = []
    for h in heads:
        pieces = []
        for i in range(n_sub):
            hi = (i + 1) * HG_SUB
            r = lax.broadcasted_iota(jnp.int32, (HG_SUB, hi), 0)
            c = lax.broadcasted_iota(jnp.int32, (HG_SUB, hi), 1)
            a = jnp.where(c - i * HG_SUB <= r, attn[h][i], 0.0)
            pieces.append(_dot(a.astype(bf16), v_bf[h][:hi]))
        o_intra.append(jnp.concatenate(pieces, axis=0))
    out.update(o_intra=o_intra, q_in=q_in, b_last=b_last, st_add=st_add)


def _hgrn_apply_state(intra, sts):
    heads = range(len(sts))
    outs = [intra["o_intra"][h] + _dot_nt(intra["q_in"][h], sts[h].astype(bf16)) for h in heads]
    new_sts = [jnp.exp(intra["b_last"][h]) * sts[h] + intra["st_add"][h] for h in heads]
    return outs, new_sts


def _hgrn_chunk_matmul(qs, ks, vs, bs, sts):
    intra = {}
    for _ in _hgrn_intra_units(qs, ks, vs, bs, intra):
        pass
    return _hgrn_apply_state(intra, sts)


def _hgrn_kernel(q_ref, lf_ref, k_ref, v_ref, gs_ref, gn_ref, s0_ref, o_ref, sout_ref, st_scr):
    tb = pl.program_id(1)
    n_tb = pl.num_programs(1)
    n_chunk = q_ref.shape[0] // HG_CHUNK
    n_sub = HG_CHUNK // HG_SUB

    @pl.when(tb == 0)
    def _():
        for h in range(HG_HEADS):
            st_scr[h] = s0_ref[h].T

    r = lax.broadcasted_iota(jnp.int32, (HG_CHUNK, HG_CHUNK), 0)
    c = lax.broadcasted_iota(jnp.int32, (HG_CHUNK, HG_CHUNK), 1)
    lower = jnp.where(c <= r, 1.0, 0.0).astype(bf16)
    head_cols = [slice(h * HG_DK, (h + 1) * HG_DK) for h in range(HG_HEADS)]

    def chunk(ci, carry):
        rows = pl.ds(pl.multiple_of(ci * HG_CHUNK, HG_CHUNK), HG_CHUNK)
        b_all = _dot_exact_rhs_lhs(lower, lf_ref[rows, :])

        def finish(h, o, st_new):
            cols = head_cols[h]
            st_scr[h] = st_new
            o = o * lax.rsqrt(jnp.mean(o * o, axis=-1, keepdims=True) + EPS)
            o_ref[rows, cols] = (o * gn_ref[:, cols] * gs_ref[rows, cols]).astype(o_ref.dtype)

        ends = [b_all[(i + 1) * HG_SUB - 1:(i + 1) * HG_SUB, :] for i in range(n_sub)]
        span = ends[0]
        for i in range(1, n_sub):
            span = jnp.minimum(span, ends[i] - ends[i - 1])
        bounded = jnp.min(span) >= HG_MATMUL_MIN_LOG

        @pl.when(bounded)
        def _():
            outs, new_sts = _hgrn_chunk_matmul(
                [q_ref[rows, cols] for cols in head_cols], [k_ref[rows, cols] for cols in head_cols],
                [v_ref[rows, cols] for cols in head_cols], [b_all[:, cols] for cols in head_cols],
                [st_scr[h] for h in range(HG_HEADS)])
            for h in range(HG_HEADS):
                finish(h, outs[h], new_sts[h])

        @pl.when(jnp.logical_not(bounded))
        def _():
            for h, cols in enumerate(head_cols):
                o, st_new = _hgrn_head_chunk(q_ref[rows, cols], k_ref[rows, cols], v_ref[rows, cols],
                                             b_all[:, cols], st_scr[h])
                finish(h, o, st_new)

        return carry

    lax.fori_loop(0, n_chunk, chunk, 0)

    @pl.when(tb == n_tb - 1)
    def _():
        for h in range(HG_HEADS):
            sout_ref[h] = st_scr[h].T


def _dot_exact_rhs_lhs(m, x):
    hi, mid, lo = _split3(x)
    return _dot(m, hi) + _dot(m, mid) + _dot(m, lo)


def _hgrn(qb, lf, kb, ib, gs, gnorm, s0, tt):
    b, t, _ = qb.shape
    blk = pl.BlockSpec((None, tt, HG_W), lambda i, j: (i, j, 0))
    sspec = pl.BlockSpec((None, HG_HEADS, HG_DK, HG_DV), lambda i, j: (i, 0, 0, 0))
    return pl.pallas_call(
        _hgrn_kernel,
        grid=(b, t // tt),
        in_specs=[blk, blk, blk, blk, blk, pl.BlockSpec((1, HG_W), lambda i, j: (0, 0)), sspec],
        out_specs=[blk, sspec],
        out_shape=[jax.ShapeDtypeStruct((b, t, HG_W), bf16),
                   jax.ShapeDtypeStruct((b, HG_HEADS, HG_DK, HG_DV), f32)],
        scratch_shapes=[pltpu.VMEM((HG_HEADS, HG_DV, HG_DK), f32)],
        compiler_params=_params("arbitrary", "arbitrary"),
        name="hgrn",
    )(qb, lf, kb, ib, gs, gnorm, s0)


def _round_robin(*gens, units=None):
    turns = list(zip(gens, units or [1] * len(gens)))
    while turns:
        turns = [(gen, n) for gen, n in turns if all(next(gen, "done") != "done" for _ in range(n))]


def _even_fused_kernel(layer, n_tb, x_ref, mod_ref, g_ref, w_ref, wkvt_ref, lbl_ref, gn_ref, s0_ref,
                       qa_ref, ka_ref, va_ref, kab_ref, vab_ref, ob_ref, sout_ref,
                       q_scr, b_scr, k_scr, v_scr, gs_scr, span_scr, st_scr):
    _, tt, d = x_ref.shape
    s = pl.program_id(0)
    hg_blk = jnp.maximum(s - 1, 0)
    tb_h = lax.rem(hg_blk, n_tb)
    n_chunk = tt // HG_CHUNK
    n_sub = HG_CHUNK // HG_SUB
    head_cols = [slice(h * HG_DK, (h + 1) * HG_DK) for h in range(HG_HEADS)]
    operand_scr = (q_scr, b_scr, k_scr, v_scr, gs_scr)

    @pl.when(s == 0)
    def _():
        for scr in operand_scr + (span_scr,):
            scr[...] = jnp.zeros_like(scr)

    @pl.when(tb_h == 0)
    def _():
        for h in range(HG_HEADS):
            st_scr[h] = s0_ref[h].T

    r = lax.broadcasted_iota(jnp.int32, (HG_CHUNK, HG_CHUNK), 0)
    c = lax.broadcasted_iota(jnp.int32, (HG_CHUNK, HG_CHUNK), 1)
    lower = jnp.where(c <= r, 1.0, 0.0).astype(bf16)

    def proj_units(out):
        hn = _norm_mod(x_ref[...], g_ref[...], mod_ref[:, 1:2, :], mod_ref[:, 0:1, :])
        hn = hn.reshape(tt, d).astype(bf16)
        yield

        def seg(i):
            return _dot(hn, w_ref[:, i * SEG:(i + 1) * SEG])

        qa_ref[...] = (seg(0) * (1.0 / math.sqrt(SB_DIM))).astype(bf16).reshape(1, tt, SEG)
        yield
        for i, (full_ref, blocks_ref) in enumerate(((ka_ref, kab_ref), (va_ref, vab_ref))):
            kt = _dot_nt(wkvt_ref[i], hn)
            full_ref[...] = kt
            for cb in range(tt // LANES):
                blocks_ref[cb] = kt[:, cb * LANES:(cb + 1) * LANES].astype(bf16)
            yield
        q = _silu(seg(3)) * (HG_DK ** -0.5)
        yield
        lg = lbl_ref[...]
        e = jnp.exp(lg - jnp.max(lg, axis=0, keepdims=True))
        lb = jnp.sum(e[:layer + 1], axis=0, keepdims=True) / jnp.sum(e, axis=0, keepdims=True)
        fb = seg(4)
        lf = jnp.log(lb + (1.0 - lb) * _sigmoid(fb))
        k = (1.0 - lb) * _sigmoid(-fb)
        yield
        b = jnp.concatenate([_dot_exact_rhs_lhs(lower, lf[ci * HG_CHUNK:(ci + 1) * HG_CHUNK])
                             for ci in range(n_chunk)], axis=0)
        yield
        v = seg(5)
        yield
        gs = _silu(seg(6))
        span = None
        for ci in range(n_chunk):
            prev = None
            for i in range(n_sub):
                row = ci * HG_CHUNK + (i + 1) * HG_SUB - 1
                end = b[row:row + 1, :]
                this = end if prev is None else end - prev
                span = this if span is None else jnp.minimum(span, this)
                prev = end
        out.update(q=q, b=b, k=k, v=v, gs=gs, span=span)

    def store_operands(out):
        for scr, name in zip(operand_scr, ("q", "b", "k", "v", "gs")):
            scr[...] = out[name]
        span_scr[0:1, :] = out["span"]

    def finish(rows, h, o):
        cols = head_cols[h]
        o = o * lax.rsqrt(jnp.mean(o * o, axis=-1, keepdims=True) + EPS)
        ob_ref[0, rows, cols] = (o * gn_ref[:, cols] * gs_scr[rows, cols]).astype(ob_ref.dtype)

    def hgrn_units():
        chunk_rows = [slice(ci * HG_CHUNK, (ci + 1) * HG_CHUNK) for ci in range(n_chunk)]
        intras = [{} for _ in chunk_rows]
        stages = [_hgrn_intra_units(
            [q_scr[rows, cols] for cols in head_cols], [k_scr[rows, cols] for cols in head_cols],
            [v_scr[rows, cols] for cols in head_cols], [b_scr[rows, cols] for cols in head_cols], intra)
            for rows, intra in zip(chunk_rows, intras)]
        while stages:
            unfinished = []
            for gen in stages:
                if next(gen, "done") != "done":
                    unfinished.append(gen)
                yield
            stages = unfinished
        sts = [st_scr[h] for h in range(HG_HEADS)]
        for rows, intra in zip(chunk_rows, intras):
            outs, sts = _hgrn_apply_state(intra, sts)
            for h in range(HG_HEADS):
                finish(rows, h, outs[h])
            yield
        for h in range(HG_HEADS):
            st_scr[h] = sts[h]

    bounded = jnp.min(span_scr[0:1, :]) >= HG_MATMUL_MIN_LOG

    @pl.when(bounded)
    def _():
        res = {}
        _round_robin(hgrn_units(), proj_units(res), units=[3, 1])
        store_operands(res)

    @pl.when(jnp.logical_not(bounded))
    def _():
        def chunk(ci, carry):
            rows = pl.ds(pl.multiple_of(ci * HG_CHUNK, HG_CHUNK), HG_CHUNK)
            for h, cols in enumerate(head_cols):
                o, st_new = _hgrn_head_chunk(q_scr[rows, cols], k_scr[rows, cols], v_scr[rows, cols],
                                             b_scr[rows, cols], st_scr[h])
                st_scr[h] = st_new
                finish(rows, h, o)
            return carry

        lax.fori_loop(0, n_chunk, chunk, 0)
        res = {}
        _round_robin(proj_units(res))
        store_operands(res)

    @pl.when((tb_h == n_tb - 1) & (s > 0))
    def _():
        for h in range(HG_HEADS):
            sout_ref[h] = st_scr[h].T


def _even_fused(x, mod, g, w_bf, wkvt_bf, lb_logits, gnorm, s0, layer, tt):
    b, t, d = x.shape
    n_tb = t // tt
    n_blocks = b * n_tb
    proj_blk = lambda s: jnp.minimum(s, n_blocks - 1)
    hgrn_blk = lambda s: jnp.maximum(s - 1, 0)
    rows = lambda width, which: pl.BlockSpec((1, tt, width), lambda s: (which(s) // n_tb, which(s) % n_tb, 0))
    operand = pltpu.VMEM((tt, HG_W), f32)
    return pl.pallas_call(
        functools.partial(_even_fused_kernel, layer, n_tb),
        grid=(n_blocks + 1,),
        in_specs=[
            rows(d, proj_blk),
            pl.BlockSpec((1, N_MOD, d), lambda s: (proj_blk(s) // n_tb, 0, 0)),
            _resident((1, d)),
            _resident((d, N_SEG * SEG)),
            _resident((2, SEG, d)),
            _resident((DEPTH + 1, HG_W)),
            _resident((1, HG_W)),
            pl.BlockSpec((None, HG_HEADS, HG_DK, HG_DV), lambda s: (hgrn_blk(s) // n_tb, 0, 0, 0)),
        ],
        out_specs=[
            rows(SEG, proj_blk),
            pl.BlockSpec((None, SEG, tt), lambda s: (proj_blk(s) // n_tb, 0, proj_blk(s) % n_tb)),
            pl.BlockSpec((None, SEG, tt), lambda s: (proj_blk(s) // n_tb, 0, proj_blk(s) % n_tb)),
            pl.BlockSpec((None, tt // LANES, SEG, LANES), lambda s: (proj_blk(s) // n_tb, proj_blk(s) % n_tb, 0, 0)),
            pl.BlockSpec((None, tt // LANES, SEG, LANES), lambda s: (proj_blk(s) // n_tb, proj_blk(s) % n_tb, 0, 0)),
            rows(HG_W, hgrn_blk),
            pl.BlockSpec((None, HG_HEADS, HG_DK, HG_DV), lambda s: (hgrn_blk(s) // n_tb, 0, 0, 0)),
        ],
        out_shape=[
            jax.ShapeDtypeStruct((b, t, SEG), bf16),
            jax.ShapeDtypeStruct((b, SEG, t), f32), jax.ShapeDtypeStruct((b, SEG, t), f32),
            jax.ShapeDtypeStruct((b, t // LANES, SEG, LANES), bf16),
            jax.ShapeDtypeStruct((b, t // LANES, SEG, LANES), bf16),
            jax.ShapeDtypeStruct((b, t, HG_W), bf16),
            jax.ShapeDtypeStruct((b, HG_HEADS, HG_DK, HG_DV), f32),
        ],
        scratch_shapes=[operand] * 5 + [pltpu.VMEM((SUBLANES, HG_W), f32),
                                        pltpu.VMEM((HG_HEADS, HG_DV, HG_DK), f32)],
        compiler_params=_params("arbitrary"),
        name="even_pre_hgrn",
    )(x, mod, g, w_bf, wkvt_bf, lb_logits, gnorm, s0)


def _ffn(hn_bf, wg_ref, wu_ref, wd_ref):
    g = _dot(hn_bf, wg_ref[...])
    u = _dot(hn_bf, wu_ref[...])
    return _dot((_silu(g) * u).astype(bf16), wd_ref[...])


def _resident(shape):
    return pl.BlockSpec(shape, lambda *_: (0,) * len(shape), pipeline_mode=pl.Buffered(1))


def _ffn_specs(layer):
    one = lambda rows, cols: pl.BlockSpec((None, rows, cols), lambda *_: (layer, 0, 0),
                                          pipeline_mode=pl.Buffered(1))
    return [one(D_MODEL, D_FF), one(D_MODEL, D_FF), one(D_FF, D_MODEL)]


def _even_post_kernel(x_ref, mod_ref, oa_ref, ob_ref, wo_ref, gf_ref, wg_ref, wu_ref, wd_ref, out_ref):
    nb, tt, d = x_ref.shape
    m = nb * tt
    o = (_dot(oa_ref[...].reshape(m, SB_WIDTH), wo_ref[:SB_WIDTH, :])
         + _dot(ob_ref[...].reshape(m, HG_W), wo_ref[SB_WIDTH:, :]))
    x1 = x_ref[...] + (1.0 + mod_ref[:, 2:3, :]) * o.reshape(nb, tt, d)
    hn = _norm_mod(x1, gf_ref[...], mod_ref[:, 4:5, :], mod_ref[:, 3:4, :])
    y = _ffn(hn.reshape(m, d).astype(bf16), wg_ref, wu_ref, wd_ref)
    out_ref[...] = x1 + (1.0 + mod_ref[:, 5:6, :]) * y.reshape(nb, tt, d)


def _even_post(x, mod, oa, ob, wo, gf, wg, wu, wd, layer, nb, tt):
    b, t, d = x.shape
    blk = lambda width: pl.BlockSpec((nb, tt, width), lambda i, tb: (i, tb, 0))
    return pl.pallas_call(
        _even_post_kernel,
        grid=(b // nb, t // tt),
        in_specs=[blk(d), pl.BlockSpec((nb, N_MOD, d), lambda i, tb: (i, 0, 0)),
                  blk(SB_WIDTH), blk(HG_W), _resident((d, d)), _resident((1, d))] + _ffn_specs(layer),
        out_specs=blk(d),
        out_shape=jax.ShapeDtypeStruct((b, t, d), f32),
        compiler_params=_params("arbitrary", "arbitrary"),
        name="even_post_ffn",
    )(x, mod, oa, ob, wo, gf, wg, wu, wd)


def _even_post_attn_kernel(n_tb, x_ref, mod_ref, ob_ref, q_ref, k_ref, v_ref, wo_ref, gf_ref,
                           wg_ref, wu_ref, wd_ref, out_ref, oa_scr):
    _, tt, d = x_ref.shape
    tq = LANES
    n_q = tt // tq
    n_pair = SB_WIDTH // LANES
    s = pl.program_id(0)
    attn_blk = jnp.minimum(s, pl.num_programs(0) - 2)
    i0 = lax.rem(attn_blk, n_tb) * n_q
    consts = _sb_prompt_consts(tq)
    pair_cols = [slice(p * LANES, (p + 1) * LANES) for p in range(n_pair)]
    k_of = [lambda j, cols=cols: k_ref[j, cols, :] for cols in pair_cols]
    v_of = [lambda j, cols=cols: v_ref[j, cols, :] for cols in pair_cols]

    @pl.when(s == 0)
    def _():
        oa_scr[...] = jnp.zeros_like(oa_scr)

    def post_units(oa):
        o = _dot(oa, wo_ref[:SB_WIDTH, :])
        yield
        o = o + _dot(ob_ref[...].reshape(tt, HG_W), wo_ref[SB_WIDTH:, :])
        yield
        x1 = x_ref[...] + (1.0 + mod_ref[:, 2:3, :]) * o.reshape(1, tt, d)
        hn = _norm_mod(x1, gf_ref[...], mod_ref[:, 4:5, :], mod_ref[:, 3:4, :])
        hn = hn.reshape(tt, d).astype(bf16)
        yield
        acts = []
        for j in range(D_FF // MXU_COLS):
            cols = slice(j * MXU_COLS, (j + 1) * MXU_COLS)
            g = _dot(hn, wg_ref[:, cols])
            u = _dot(hn, wu_ref[:, cols])
            acts.append((_silu(g) * u).astype(bf16))
            yield
        act = jnp.concatenate(acts, axis=1)
        for n in range(d // MXU_COLS):
            cols = slice(n * MXU_COLS, (n + 1) * MXU_COLS)
            y = _dot(act, wd_ref[:, cols]).reshape(1, tt, MXU_COLS)
            out_ref[:, :, cols] = x1[:, :, cols] + (1.0 + mod_ref[:, 5:6, cols]) * y
            yield

    def attn_units(out):
        for p, cols in enumerate(pair_cols):
            qs = [q_ref[0, g * tq:(g + 1) * tq, cols] for g in range(n_q)]
            groups = _sb_prompt_groups(qs, k_of[p], v_of[p], i0, consts)
            res = {}
            yield from _sb_groups_units(groups, res)
            out[p] = res["results"]
            yield

    results = {}
    _round_robin(post_units(oa_scr[...]), attn_units(results), units=[3, 2])

    reach = None
    for p, cols in enumerate(pair_cols):
        for g, (carry, acc) in enumerate(results[p]):
            oa_scr[g * tq:(g + 1) * tq, cols] = _unstack_heads(acc).astype(oa_scr.dtype)
            reach = carry if reach is None else jnp.maximum(reach, carry)

    @pl.when(jnp.max(reach) > SB_SKIP_LOG)
    def _():
        for p, cols in enumerate(pair_cols):
            def one(g, carry, p=p, cols=cols):
                rows = pl.ds(pl.multiple_of(g * tq, tq), tq)
                groups = _sb_prompt_groups([q_ref[0, rows, cols]], k_of[p], v_of[p], i0 + g, consts)
                o = _sb_prompt_finish(groups[0], _sb_groups(groups)[0], i0 + g, k_of[p], v_of[p], consts)
                oa_scr[rows, cols] = o.astype(oa_scr.dtype)
                return carry

            lax.fori_loop(0, n_q, one, 0)


def _even_post_attn(x, mod, ob, q, k_blocks, v_blocks, wo, gf, wg, wu, wd, layer, tt):
    b, t, d = x.shape
    n_tb = t // tt
    n_blocks = b * n_tb
    attn_blk = lambda s: jnp.minimum(s, n_blocks - 1)
    post_blk = lambda s: jnp.maximum(s - 1, 0)
    rows = lambda width, which: pl.BlockSpec((1, tt, width), lambda s: (which(s) // n_tb, which(s) % n_tb, 0))
    keys = pl.BlockSpec((None, t // LANES, SB_WIDTH, LANES), lambda s: (attn_blk(s) // n_tb, 0, 0, 0),
                        pipeline_mode=pl.Buffered(1))
    return pl.pallas_call(
        functools.partial(_even_post_attn_kernel, n_tb),
        grid=(n_blocks + 1,),
        in_specs=[rows(d, post_blk), pl.BlockSpec((1, N_MOD, d), lambda s: (post_blk(s) // n_tb, 0, 0)),
                  rows(HG_W, post_blk), rows(SB_WIDTH, attn_blk), keys, keys,
                  _resident((d, d)), _resident((1, d))] + _ffn_specs(layer),
        out_specs=rows(d, post_blk),
        out_shape=jax.ShapeDtypeStruct((b, t, d), f32),
        scratch_shapes=[pltpu.VMEM((tt, SB_WIDTH), bf16)],
        compiler_params=_params("arbitrary"),
        name="even_attn_post_ffn",
    )(x, mod, ob, q, k_blocks, v_blocks, wo, gf, wg, wu, wd)


def _linear_scan(a, u, h_init, tt):
    m, d = a.shape
    n_group = m // SUBLANES
    a3 = a.reshape(n_group, SUBLANES, d)
    u3 = u.reshape(n_group, SUBLANES, d)
    sub = lax.broadcasted_iota(jnp.int32, (1, SUBLANES, d), 1)
    step = 1
    while step < SUBLANES:
        keep = sub >= step
        a_prev = jnp.where(keep, pltpu.roll(a3, step, 1), 1.0)
        u_prev = jnp.where(keep, pltpu.roll(u3, step, 1), 0.0)
        u3 = a3 * u_prev + u3
        a3 = a3 * a_prev
        step *= 2
    groups_per_seq = tt // SUBLANES
    carry = None
    groups = []
    for gi in range(n_group):
        if gi % groups_per_seq == 0:
            carry = h_init[gi // groups_per_seq]
        h_group = u3[gi] + a3[gi] * carry
        groups.append(h_group)
        carry = h_group[SUBLANES - 1:SUBLANES, :]
    return jnp.concatenate(groups, axis=0)


def _odd_kernel(stream_start, pipelined, n_tb, x_ref, mod_ref, mod_ffn_ref, gm_ref, win_ref, cw_ref, cb_ref,
                wax_ref, ba_ref, bx_ref, lam_ref, wo_ref, conv0_ref, h0_ref, gf_ref, fn_ref,
                wg_ref, wu_ref, wd_ref, y_ref, conv_ref, hout_ref, xs_scr, hprev_scr, x1_scr, hn_scr):
    nb, tt, d = x_ref.shape
    m = nb * tt
    s = pl.program_id(0)
    tb = lax.rem(s, n_tb)

    def ffn_units(x1, hn_bf, out):
        acts = []
        for j in range(D_FF // MXU_COLS):
            cols = slice(j * MXU_COLS, (j + 1) * MXU_COLS)
            g = _dot(hn_bf, wg_ref[:, cols])
            u = _dot(hn_bf, wu_ref[:, cols])
            acts.append((_silu(g) * u).astype(bf16))
            yield
        act = jnp.concatenate(acts, axis=1)
        parts = []
        for n in range(d // MXU_COLS):
            cols = slice(n * MXU_COLS, (n + 1) * MXU_COLS)
            y = _dot(act, wd_ref[:, cols]).reshape(nb, tt, MXU_COLS)
            parts.append(x1[:, :, cols] + (1.0 + mod_ffn_ref[:, 5:6, cols]) * y)
            yield
        x2 = jnp.concatenate(parts, axis=-1)
        ms = jnp.mean(x2 * x2, axis=-1, keepdims=True)
        out["y"] = x2 * lax.rsqrt(ms + EPS) * fn_ref[...]

    def mixer_units(out):
        x = x_ref[...]
        hn = _norm_mod(x, gm_ref[...], mod_ref[:, 1:2, :], mod_ref[:, 0:1, :])
        hn = hn.reshape(m, d).astype(bf16)
        yield
        gate = _dot(hn, win_ref[:, :d])
        yield
        xbr = _dot(hn, win_ref[:, d:]).reshape(nb, tt, d)
        yield
        xs_scr[:, HALO:HALO + tt, :] = xbr
        xc = cb_ref[...] + xbr * cw_ref[CONV_W - 1:CONV_W, :]
        for k in range(1, CONV_W):
            xc = xc + xs_scr[:, HALO - k:HALO - k + tt, :] * cw_ref[CONV_W - 1 - k:CONV_W - k, :]
        tail = xs_scr[:, HALO + tt - (CONV_W - 1):HALO + tt, :]
        xs_scr[:, HALO - (CONV_W - 1):HALO, :] = tail
        xc = xc.reshape(m, d)
        xc_bf = xc.astype(bf16)
        yield
        lam = lam_ref[...]
        log_sig_lam = -(jnp.maximum(-lam, 0.0) + jnp.log1p(jnp.exp(-jnp.abs(lam))))
        y_in, h_last = [], []
        for blk in range(LRU_BLOCKS):
            cols = slice(blk * LRU_BLK, (blk + 1) * LRU_BLK)
            both = _dot(xc_bf[:, cols], wax_ref[blk])
            r = _sigmoid(both[:, :LRU_BLK] + ba_ref[:, cols])
            gi = _sigmoid(both[:, LRU_BLK:] + bx_ref[:, cols])
            log_a = C_SCALE * r * log_sig_lam[:, cols]
            a = jnp.exp(log_a)
            mult = jnp.sqrt(-jnp.tanh(log_a) * (1.0 + a * a))
            if stream_start:
                row = (lax.broadcasted_iota(jnp.int32, (m, LRU_BLK), 0) & (tt - 1)) + tb * tt
                mult = jnp.where(row == 0, 1.0, mult)
            h = _linear_scan(a, mult * gi * xc[:, cols], hprev_scr[:, :, cols], tt)
            h_last.append(h.reshape(nb, tt, LRU_BLK)[:, tt - 1:tt, :])
            y_in.append((jax.nn.gelu(gate[:, cols]) * h).astype(bf16))
            yield
        h_last = jnp.concatenate(h_last, axis=-1)
        hprev_scr[...] = h_last
        y_in = jnp.concatenate(y_in, axis=1)
        o = _dot(y_in, wo_ref[...])
        yield
        x1 = x + (1.0 + mod_ref[:, 2:3, :]) * o.reshape(nb, tt, d)
        hn2 = _norm_mod(x1, gf_ref[...], mod_ref[:, 4:5, :], mod_ref[:, 3:4, :])
        out.update(x1=x1, hn2=hn2.reshape(m, d).astype(bf16), tail=tail, h_last=h_last)

    @pl.when(tb == 0)
    def _():
        xs_scr[:, HALO - (CONV_W - 1):HALO, :] = conv0_ref[...]
        hprev_scr[...] = h0_ref[...]

    res = {}
    if pipelined:
        @pl.when(s == 0)
        def _():
            x1_scr[...] = jnp.zeros_like(x1_scr)
            hn_scr[...] = jnp.zeros_like(hn_scr)

        _round_robin(ffn_units(x1_scr[...], hn_scr[...], res), mixer_units(res))
        y_ref[...] = res["y"]
        x1_scr[...] = res["x1"]
        hn_scr[...] = res["hn2"]

        @pl.when(s < pl.num_programs(0) - 1)
        def _():
            conv_ref[...] = res["tail"]
            hout_ref[...] = res["h_last"]
    else:
        for _ in mixer_units(res):
            pass
        conv_ref[...] = res["tail"]
        hout_ref[...] = res["h_last"]
        for _ in ffn_units(res["x1"], res["hn2"], res):
            pass
        y_ref[...] = res["y"]


def _odd_layer(x, mod, gm, win, cw, cb, wax, ba, bx, lam, wo, conv0, h0, gf, fn, wg, wu, wd,
               layer, nb, tt, stream_start):
    b, t, d = x.shape
    n_tb = t // tt
    n_blocks = (b // nb) * n_tb
    pipelined = n_blocks > 1
    mix_blk = lambda s: jnp.minimum(s, n_blocks - 1)
    ffn_blk = (lambda s: jnp.maximum(s - 1, 0)) if pipelined else mix_blk
    rows_spec = lambda which: pl.BlockSpec((nb, tt, d), lambda s: (which(s) // n_tb, which(s) % n_tb, 0))
    per_seq = lambda rows, which: pl.BlockSpec((nb, rows, d), lambda s: (which(s) // n_tb, 0, 0))
    row = _resident((1, d))
    return pl.pallas_call(
        functools.partial(_odd_kernel, stream_start, pipelined, n_tb),
        grid=(n_blocks + (1 if pipelined else 0),),
        in_specs=[rows_spec(mix_blk), per_seq(N_MOD, mix_blk), per_seq(N_MOD, ffn_blk), row,
                  _resident((d, 2 * d)), _resident((CONV_W, d)),
                  row, _resident((LRU_BLOCKS, LRU_BLK, 2 * LRU_BLK)), row,
                  row, row, _resident((d, d)), per_seq(CONV_W - 1, mix_blk), per_seq(1, mix_blk),
                  row, row] + _ffn_specs(layer),
        out_specs=[rows_spec(ffn_blk), per_seq(CONV_W - 1, mix_blk), per_seq(1, mix_blk)],
        out_shape=[jax.ShapeDtypeStruct((b, t, d), f32),
                   jax.ShapeDtypeStruct((b, CONV_W - 1, d), f32),
                   jax.ShapeDtypeStruct((b, 1, d), f32)],
        scratch_shapes=[pltpu.VMEM((nb, HALO + tt, d), f32), pltpu.VMEM((nb, 1, d), f32),
                        pltpu.VMEM((nb, tt, d) if pipelined else (1, SUBLANES, LANES), f32),
                        pltpu.VMEM((nb * tt, d) if pipelined else (2 * SUBLANES, LANES), bf16)],
        compiler_params=_params("arbitrary"),
        name="odd_layer_ffn",
    )(x, mod, mod, gm, win, cw, cb, wax, ba, bx, lam, wo, conv0, h0, gf, fn, wg, wu, wd)


def _trunk(x, mods, past, weights, nb, tt, hg_tt):
    b, t, d = x.shape
    sample = past is not None
    row = lambda v: v.reshape(1, -1)

    if sample:
        past_k, past_v, s0, conv0, h0 = past
        qa, ka, va, ka_bf, va_bf, qb, lf, kb, ib, gs = _even_pre(
            x, mods[0], row(weights["norm_mix"][0]), weights["w_in_even"],
            weights["hg_lb_logits"], 0, nb // 2, tt)
        oa = _sb_sample(qa, ka_bf, va_bf, past_k, past_v, LANES)
        ob, s_new = _hgrn(qb, lf, kb, ib, gs, row(weights["hg_gnorm"]), s0, hg_tt)
        x = _even_post(x, mods[0], oa, ob, weights["w_out_even"], row(weights["norm_ffn"][0]),
                       weights["ffn_wg"], weights["ffn_wu"], weights["ffn_wd"], 0, nb // 2, tt)
        heads = lambda a: a.reshape(1, b, t, SB_HEADS, SB_DIM)
    else:
        s0 = jnp.zeros((b, HG_HEADS, HG_DK, HG_DV), f32)
        conv0 = jnp.zeros((b, CONV_W - 1, d), f32)
        h0 = jnp.zeros((b, 1, d), f32)
        qa, ka, va, ka_bf, va_bf, ob, s_new = _even_fused(
            x, mods[0], row(weights["norm_mix"][0]), weights["w_in_even"], weights["w_kv_t"],
            weights["hg_lb_logits"], row(weights["hg_gnorm"]), s0, 0, tt)
        x = _even_post_attn(x, mods[0], ob, qa, ka_bf, va_bf, weights["w_out_even"],
                            row(weights["norm_ffn"][0]), weights["ffn_wg"], weights["ffn_wu"],
                            weights["ffn_wd"], 0, tt)
        heads = lambda a: a.reshape(b, SB_HEADS, SB_DIM, t).transpose(0, 3, 1, 2)[None]

    y, conv_new, h_new = _odd_layer(
        x, mods[1], row(weights["norm_mix"][1]), weights["w_in_odd"], weights["conv_w"],
        row(weights["conv_b"]), weights["lru_wax"], row(weights["lru_ba"]), row(weights["lru_bx"]),
        row(weights["lru_lambda"]), weights["w_out_odd"], conv0, h0,
        row(weights["norm_ffn"][1]), row(weights["final_norm"]),
        weights["ffn_wg"], weights["ffn_wu"], weights["ffn_wd"],
        1, nb, tt, not sample)

    return (y, heads(ka), heads(va), s_new[None], conv_new[None], h_new.reshape(1, b, d))


def kernel(x_prompt, x_sample, cache_sb_k, cache_sb_v, state_hgrn, state_conv, state_lru, c_prompt, c_sample, norm_mix, norm_ffn, w_ada, b_ada, w_in_even, w_out_even, hg_gnorm, hg_lb_logits, w_in_odd, conv_w, conv_b, lru_wa, lru_ba, lru_wx, lru_bx, lru_lambda, w_out_odd, ffn_wg, ffn_wu, ffn_wd, final_norm):
    bp = x_prompt.shape[0]
    bs, ts, d = x_sample.shape
    w_in_even_bf = w_in_even[0].astype(bf16)
    weights = {
        "norm_mix": norm_mix, "norm_ffn": norm_ffn,
        "w_in_even": w_in_even_bf, "w_out_even": w_out_even[0].astype(bf16),
        "w_kv_t": w_in_even_bf[:, SEG:3 * SEG].T.reshape(2, SEG, d),
        "hg_gnorm": hg_gnorm[0], "hg_lb_logits": hg_lb_logits,
        "w_in_odd": w_in_odd[0].astype(bf16), "conv_w": conv_w[0], "conv_b": conv_b[0],
        "lru_wax": jnp.concatenate([lru_wa[0], lru_wx[0]], axis=-1).astype(bf16),
        "lru_ba": lru_ba[0], "lru_bx": lru_bx[0], "lru_lambda": lru_lambda[0],
        "w_out_odd": w_out_odd[0].astype(bf16),
        "ffn_wg": ffn_wg.astype(bf16), "ffn_wu": ffn_wu.astype(bf16), "ffn_wd": ffn_wd.astype(bf16),
        "final_norm": final_norm,
    }
    mods = _ada(jnp.concatenate([c_prompt, c_sample], axis=0), w_ada, b_ada)
    mods = mods.reshape(DEPTH, bp + bs, N_MOD, d)

    out_p = _trunk(x_prompt, mods[:, :bp], None, weights, nb=1, tt=512, hg_tt=512)
    cache_t = lambda c: c[0].transpose(0, 2, 3, 1).reshape(bs, SB_WIDTH, -1)
    past = (cache_t(cache_sb_k), cache_t(cache_sb_v),
            state_hgrn[0], state_conv[0], state_lru[0].reshape(bs, 1, d))
    out_s = _trunk(x_sample, mods[:, bp:], past, weights, nb=bs, tt=ts, hg_tt=ts)

    y_p, k_p, v_p, s_p, conv_p, h_p = out_p
    y_s, k_s, v_s, s_s, conv_s, h_s = out_s
    return (y_p, y_s, k_p, v_p, s_p, conv_p, h_p, k_s, v_s, s_s, conv_s, h_s)
```
